```python
import math
import jax, jax.numpy as jnp
from jax import lax
import numpy as np

D_MODEL = 1024
BATCH = 4
SEQ = 4096
DEPTH = 2
DEC_BATCH = 32
DEC_SEQ = 2048
PAST_LEN = 128

GRID_W = 64
N_BRANCH = 3
BRANCH_WIDTH = D_MODEL // 2
ROPE_THETA = 10000.0
EPS = 1e-6
NEG = -1e30
A_QK_DIM = 64
A_V_DIM = 2 * A_QK_DIM
A_HEADS = BRANCH_WIDTH // A_V_DIM
A_QK_WIDTH = 2 * A_HEADS * A_QK_DIM
Q_BLOCK = 128
B_HEAD_DIM = 64
B_HEADS = BRANCH_WIDTH // B_HEAD_DIM
NA_KH = 8
NA_KW = 16
C_GROUPS = ((128, 1), (512, 4), (2048, 16))
C_HEAD_DIM = 128
C_HEADS_PER_GROUP = BRANCH_WIDTH // C_HEAD_DIM
C_HEADS = len(C_GROUPS) * C_HEADS_PER_GROUP
C_QKV_WIDTH = C_HEADS * C_HEAD_DIM
C_QBLOCK = 64
COL_A_Q = 0
COL_A_K = COL_A_Q + A_QK_WIDTH
COL_A_V = COL_A_K + A_QK_WIDTH
COL_B_Q = COL_A_V + BRANCH_WIDTH
COL_B_K = COL_B_Q + BRANCH_WIDTH
COL_B_V = COL_B_K + BRANCH_WIDTH
COL_C_Q = COL_B_V + BRANCH_WIDTH
COL_C_K = COL_C_Q + C_QKV_WIDTH
COL_C_V = COL_C_K + C_QKV_WIDTH
COL_Z = COL_C_V + C_QKV_WIDTH
COL_G = COL_Z + N_BRANCH * BRANCH_WIDTH
IN_WIDTH = COL_G + N_BRANCH * D_MODEL

kernel_name = 'hybrid_gated_encoder'


def rms_norm(x, g):
    xf = x.astype(jnp.float32)
    y = xf * lax.rsqrt(jnp.mean(xf * xf, axis=-1, keepdims=True) + EPS) * g.astype(jnp.float32)
    return y.astype(x.dtype)


def rope_tables(S, d):
    inv = ROPE_THETA ** (-jnp.arange(0, d, 2, dtype=jnp.float32) / d)
    ang = jnp.arange(S, dtype=jnp.float32)[:, None] * inv[None, :]
    return jnp.cos(ang), jnp.sin(ang)


def apply_rope(x, cos, sin):
    half = x.shape[-1] // 2
    xf = x.astype(jnp.float32)
    x1, x2 = xf[..., :half], xf[..., half:]
    return jnp.concatenate([x1 * cos - x2 * sin, x2 * cos + x1 * sin], axis=-1).astype(x.dtype)


def split_heads(t, n, d):
    B, S, _ = t.shape
    return t.reshape(B, S, n, d).transpose(0, 2, 1, 3)


def merge_heads(t):
    B, H, S, d = t.shape
    return t.transpose(0, 2, 1, 3).reshape(B, S, H * d)


def diff_attention(q, k, v, lam):
    B, H, _, S, dq = q.shape
    nblk = S // Q_BLOCK
    scale = dq ** -0.5
    q_blocks = q.reshape(B, H, 2, nblk, Q_BLOCK, dq).transpose(3, 0, 1, 2, 4, 5)

    def block(qb):
        s = jnp.einsum('bhmqd,bhmkd->bhmqk', qb, k).astype(jnp.float32) * scale
        p = jax.nn.softmax(s, axis=-1)
        w = p[:, :, 0] - lam * p[:, :, 1]
        return jnp.einsum('bhqk,bhkd->bhqd', w.astype(v.dtype), v)

    o = lax.map(block, q_blocks)
    return o.transpose(1, 2, 0, 3, 4).reshape(B, H, S, v.shape[-1])


def neighborhood_attention(q, k, v, rpb):
    B, H, S, d = q.shape
    rows = S // GRID_W
    kh = min(NA_KH, rows)
    scale = d ** -0.5
    q = q.reshape(B, H, rows, GRID_W, d)
    k = k.reshape(B, H, rows, GRID_W, d)
    v = v.reshape(B, H, rows, GRID_W, d)
    qc = jnp.arange(GRID_W)
    kc = jnp.arange(GRID_W)
    cs = jnp.clip(qc - NA_KW // 2, 0, GRID_W - NA_KW)
    col_ok = (kc[None, :] >= cs[:, None]) & (kc[None, :] < cs[:, None] + NA_KW)
    col_off = jnp.clip(kc[None, :] - qc[:, None], -(NA_KW - 1), NA_KW - 1) + NA_KW - 1
    rpb_cols = rpb.astype(jnp.float32)[:, :, col_off]

    def row(r):
        rs = jnp.clip(r - kh // 2, 0, rows - kh)
        qr = lax.dynamic_index_in_dim(q, r, axis=2, keepdims=False)
        kr = lax.dynamic_slice_in_dim(k, rs, kh, axis=2)
        vr = lax.dynamic_slice_in_dim(v, rs, kh, axis=2)
        row_off = rs + jnp.arange(kh) - r + NA_KH - 1
        bias = rpb_cols[:, row_off].transpose(0, 2, 1, 3)
        s = jnp.einsum('bhqd,bhrkd->bhqrk', qr, kr).astype(jnp.float32) * scale + bias[None]
        s = jnp.where(col_ok[:, None, :], s, NEG)
        p = jax.nn.softmax(s, axis=(-2, -1))
        return jnp.einsum('bhqrk,bhrkd->bhqd', p.astype(v.dtype), vr)

    o = lax.map(row, jnp.arange(rows))
    return o.transpose(1, 2, 0, 3, 4).reshape(B, H, S, d)


def dilated_group(q, k, v, window, dil):
    B, H, S, d = q.shape
    L = S // dil
    hw = window // (2 * dil)
    scale = d ** -0.5

    def sub(t):
        return t.reshape(B, H, L, dil, d).transpose(0, 1, 3, 2, 4)

    qs, ks, vs = sub(q), sub(k), sub(v)
    nb = -(-L // C_QBLOCK)
    Lp = nb * C_QBLOCK
    span = C_QBLOCK + 2 * hw
    qs = jnp.pad(qs, ((0, 0), (0, 0), (0, 0), (0, Lp - L), (0, 0)))
    kv_pad = ((0, 0), (0, 0), (0, 0), (hw, Lp - L + hw), (0, 0))
    ks = jnp.pad(ks, kv_pad)
    vs = jnp.pad(vs, kv_pad)
    qi = jnp.arange(C_QBLOCK)
    kj = jnp.arange(span)
    band = jnp.abs(kj[None, :] - hw - qi[:, None]) <= hw

    def blk(b):
        start = b * C_QBLOCK
        q_b = lax.dynamic_slice_in_dim(qs, start, C_QBLOCK, axis=3)
        k_b = lax.dynamic_slice_in_dim(ks, start, span, axis=3)
        v_b = lax.dynamic_slice_in_dim(vs, start, span, axis=3)
        key_idx = start + kj - hw
        ok = band & ((key_idx >= 0) & (key_idx < L))[None, :]
        s = jnp.einsum('bhrqd,bhrkd->bhrqk', q_b, k_b).astype(jnp.float32) * scale
        s = jnp.where(ok, s, NEG)
        m = jnp.max(s, axis=-1, keepdims=True)
        e = jnp.exp(s - m)
        den = jnp.sum(e, axis=-1, keepdims=True)
        o = jnp.einsum('bhrqk,bhrkd->bhrqd', (e / den).astype(v_b.dtype), v_b)
        return o, (m + jnp.log(den))[..., 0]

    o, lse = lax.map(blk, jnp.arange(nb))
    o = o.transpose(1, 2, 3, 0, 4, 5).reshape(B, H, dil, Lp, d)[:, :, :, :L]
    lse = lse.transpose(1, 2, 3, 0, 4).reshape(B, H, dil, Lp)[:, :, :, :L]
    o = o.transpose(0, 1, 3, 2, 4).reshape(B, H, S, d)
    lse = lse.transpose(0, 1, 3, 2).reshape(B, H, S)
    return o, lse


def dilated_mixture(q, k, v):
    outs, lses = [], []
    for g, (window, dil) in enumerate(C_GROUPS):
        sl = slice(g * C_HEADS_PER_GROUP, (g + 1) * C_HEADS_PER_GROUP)
        o, lse = dilated_group(q[:, sl], k[:, sl], v[:, sl], window, dil)
        outs.append(o)
        lses.append(lse)
    alpha = jax.nn.softmax(jnp.stack(lses, axis=0), axis=0)
    y = jnp.einsum('gbhs,gbhsd->bhsd', alpha, jnp.stack(outs, axis=0).astype(jnp.float32))
    return y.astype(q.dtype)


def encoder_layer(x, c, layer_idx, cos_a, sin_a, cos_c, sin_c, ln_g, w_ada, b_ada, w_in,
                  qn_a, kn_a, lam_q1, lam_k1, lam_q2, lam_k2, subln_a, qn_b, kn_b, rpb_b,
                  qn_c, kn_c, w_br, w_out):
    B, S, _ = x.shape
    mod = jax.nn.silu(c) @ w_ada + b_ada
    shift, scale, gate = jnp.split(mod, 3, axis=-1)
    h = rms_norm(x, ln_g) * (1.0 + scale[:, None, :]) + shift[:, None, :]

    def cols(lo, width):
        return h @ w_in[:, lo:lo + width]

    qa = cols(COL_A_Q, A_QK_WIDTH).reshape(B, S, A_HEADS, 2, A_QK_DIM).transpose(0, 2, 3, 1, 4)
    ka = cols(COL_A_K, A_QK_WIDTH).reshape(B, S, A_HEADS, 2, A_QK_DIM).transpose(0, 2, 3, 1, 4)
    va = split_heads(cols(COL_A_V, BRANCH_WIDTH), A_HEADS, A_V_DIM)
    qa = apply_rope(rms_norm(qa, qn_a), cos_a, sin_a)
    ka = apply_rope(rms_norm(ka, kn_a), cos_a, sin_a)
    lam_init = 0.8 - 0.6 * math.exp(-0.3 * layer_idx)
    f32 = jnp.float32
    lam = (jnp.exp(jnp.sum(lam_q1.astype(f32) * lam_k1.astype(f32)))
           - jnp.exp(jnp.sum(lam_q2.astype(f32) * lam_k2.astype(f32))) + lam_init)
    ya = diff_attention(qa, ka, va, lam)
    ya = merge_heads(rms_norm(ya, subln_a) * (1.0 - lam_init))

    qb = rms_norm(split_heads(cols(COL_B_Q, BRANCH_WIDTH), B_HEADS, B_HEAD_DIM), qn_b)
    kb = rms_norm(split_heads(cols(COL_B_K, BRANCH_WIDTH), B_HEADS, B_HEAD_DIM), kn_b)
    vb = split_heads(cols(COL_B_V, BRANCH_WIDTH), B_HEADS, B_HEAD_DIM)
    yb = merge_heads(neighborhood_attention(qb, kb, vb, rpb_b))

    qc = apply_rope(rms_norm(split_heads(cols(COL_C_Q, C_QKV_WIDTH), C_HEADS, C_HEAD_DIM), qn_c), cos_c, sin_c)
    kc = apply_rope(rms_norm(split_heads(cols(COL_C_K, C_QKV_WIDTH), C_HEADS, C_HEAD_DIM), kn_c), cos_c, sin_c)
    vc = split_heads(cols(COL_C_V, C_QKV_WIDTH), C_HEADS, C_HEAD_DIM)
    yc = merge_heads(dilated_mixture(qc, kc, vc))

    merged = None
    for i, y_i in enumerate((ya, yb, yc)):
        z = cols(COL_Z + i * BRANCH_WIDTH, BRANCH_WIDTH)
        g = cols(COL_G + i * D_MODEL, D_MODEL)
        u = jax.nn.sigmoid(g) * ((y_i * jax.nn.silu(z)) @ w_br[i])
        merged = u if merged is None else merged + u
    out = merged @ w_out
    return x + gate[:, None, :] * out


def setup_inputs(seed: int = 0) -> dict:
    key = jax.random.key(seed)
    ks = jax.random.split(key, 22)
    f32 = jnp.float32

    def nrm(k, shape, s):
        return jax.random.normal(k, shape, f32) * s

    def gain(k, shape):
        return 1.0 + 0.02 * jax.random.normal(k, shape, f32)

    return {
        'x_prompt': nrm(ks[0], (BATCH, SEQ, D_MODEL), 1.0),
        'x_sample': nrm(ks[1], (DEC_BATCH, DEC_SEQ, D_MODEL), 1.0),
        'c_prompt': nrm(ks[2], (BATCH, D_MODEL), 1.0),
        'c_sample': nrm(ks[3], (DEC_BATCH, D_MODEL), 1.0),
        'ln_g': gain(ks[4], (DEPTH, D_MODEL)),
        'w_ada': nrm(ks[5], (DEPTH, D_MODEL, 3 * D_MODEL), D_MODEL ** -0.5),
        'b_ada': nrm(ks[6], (DEPTH, 3 * D_MODEL), 0.02),
        'w_in': nrm(ks[7], (DEPTH, D_MODEL, IN_WIDTH), D_MODEL ** -0.5),
        'qn_a': gain(ks[8], (DEPTH, A_QK_DIM)),
        'kn_a': gain(ks[9], (DEPTH, A_QK_DIM)),
        'lam_q1': nrm(ks[10], (DEPTH, A_QK_DIM), 0.1),
        'lam_k1': nrm(ks[11], (DEPTH, A_QK_DIM), 0.1),
        'lam_q2': nrm(ks[12], (DEPTH, A_QK_DIM), 0.1),
        'lam_k2': nrm(ks[13], (DEPTH, A_QK_DIM), 0.1),
        'subln_a': gain(ks[14], (DEPTH, A_V_DIM)),
        'qn_b': gain(ks[15], (DEPTH, B_HEAD_DIM)),
        'kn_b': gain(ks[16], (DEPTH, B_HEAD_DIM)),
        'rpb_b': nrm(ks[17], (DEPTH, B_HEADS, 2 * NA_KH - 1, 2 * NA_KW - 1), 0.1),
        'qn_c': gain(ks[18], (DEPTH, C_HEAD_DIM)),
        'kn_c': gain(ks[19], (DEPTH, C_HEAD_DIM)),
        'w_br': nrm(ks[20], (DEPTH, N_BRANCH, BRANCH_WIDTH, D_MODEL), BRANCH_WIDTH ** -0.5),
        'w_out': nrm(ks[21], (DEPTH, D_MODEL, D_MODEL), D_MODEL ** -0.5),
    }


def reference(x_prompt, x_sample, c_prompt, c_sample, ln_g, w_ada, b_ada, w_in, qn_a, kn_a,
              lam_q1, lam_k1, lam_q2, lam_k2, subln_a, qn_b, kn_b, rpb_b, qn_c, kn_c, w_br, w_out):
    def run(x, c):
        S = x.shape[1]
        cos_a, sin_a = rope_tables(S, A_QK_DIM)
        cos_c, sin_c = rope_tables(S, C_HEAD_DIM)
        for l in range(DEPTH):
            x = encoder_layer(x, c, l, cos_a, sin_a, cos_c, sin_c, ln_g[l], w_ada[l], b_ada[l], w_in[l],
                              qn_a[l], kn_a[l], lam_q1[l], lam_k1[l], lam_q2[l], lam_k2[l], subln_a[l],
                              qn_b[l], kn_b[l], rpb_b[l], qn_c[l], kn_c[l], w_br[l], w_out[l])
        return x

    y_prompt = run(x_prompt, c_prompt)
    y_sample = run(x_sample, c_sample)
    return (y_prompt, y_sample)
```

```python
import functools
import math

import jax
import jax.numpy as jnp
from jax import lax
from jax.experimental import pallas as pl
from jax.experimental.pallas import tpu as pltpu

F32 = jnp.float32
BF16 = jnp.bfloat16

D_MODEL = 1024
GRID_W = 64
BRANCH_WIDTH = 512
ROPE_THETA = 10000.0
EPS = 1e-6
NEG = -1e30
A_QK_DIM = 64
A_HEADS = 4
B_HEAD_DIM = 64
B_HEADS = 8
NA_KH = 8
NA_KW = 16
C_DILATIONS = (1, 4, 16)
C_HALF_WINDOW = 64
C_HEAD_DIM = 128
C_HEADS_PER_GROUP = 4
IN_WIDTH = 12288
LANES = 128

PROJ_TILE = 512
N_PROJ_TILES = IN_WIDTH // PROJ_TILE
T_AQ, T_AK, T_AV, T_BQ, T_BK, T_BV = 0, 1, 2, 3, 4, 5
T_CQ, T_CK, T_CV, T_Z, T_G = 6, 9, 12, 15, 18
CB_AQ, CB_AK, CB_AV = 0, 4, 8
CB_BQ, CB_BK, CB_BV = 12, 16, 20
CB_CQ, CB_CK, CB_CV = 24, 36, 48

VMEM_LIMIT = 56 * 1024 * 1024

NA_QROWS = 8
NA_KROWS = 16
C_QBLK = 128


def _cparams(sem):
    return pltpu.CompilerParams(dimension_semantics=sem, vmem_limit_bytes=VMEM_LIMIT)


def _sigmoid(x):
    return 1.0 / (1.0 + jnp.exp(-x))


def _mod_kernel(c_ref, w_ref, b_ref, o_ref):
    c = c_ref[...]
    o_ref[0] = jnp.dot(c * _sigmoid(c), w_ref[0], preferred_element_type=F32) + b_ref[0]


def _modulation(c_all, w_ada, b_ada):
    depth = w_ada.shape[0]
    nb = c_all.shape[0]
    return pl.pallas_call(
        _mod_kernel,
        grid=(depth, 3),
        in_specs=[
            pl.BlockSpec((nb, D_MODEL), lambda l, j: (0, 0)),
            pl.BlockSpec((1, D_MODEL, D_MODEL), lambda l, j: (l, 0, j)),
            pl.BlockSpec((1, 1, D_MODEL), lambda l, j: (l, 0, j)),
        ],
        out_specs=pl.BlockSpec((1, nb, D_MODEL), lambda l, j: (l, 0, j)),
        out_shape=jax.ShapeDtypeStruct((depth, nb, 3 * D_MODEL), F32),
        compiler_params=_cparams(("arbitrary", "arbitrary")),
        name="adaln_mod",
    )(c_all, w_ada, b_ada.reshape(depth, 1, 3 * D_MODEL))


def _norm64(xb, gain):
    lo = lax.broadcasted_iota(jnp.int32, xb.shape, 1) < 64
    sq = xb * xb
    s_lo = jnp.sum(jnp.where(lo, sq, 0.0), axis=-1, keepdims=True)
    s_hi = jnp.sum(jnp.where(lo, 0.0, sq), axis=-1, keepdims=True)
    ms = jnp.where(lo, s_lo, s_hi) * (1.0 / 64.0)
    return xb * lax.rsqrt(ms + EPS) * gain


def _norm128(xb, gain):
    ms = jnp.mean(xb * xb, axis=-1, keepdims=True)
    return xb * lax.rsqrt(ms + EPS) * gain


def _rope64(xb, cos, sin_signed):
    first = (lax.broadcasted_iota(jnp.int32, xb.shape, 1) % 64) < 32
    partner = jnp.where(first, pltpu.roll(xb, 96, 1), pltpu.roll(xb, 32, 1))
    return xb * cos + partner * sin_signed


def _rope128(xb, cos, sin_signed):
    return xb * cos + pltpu.roll(xb, 64, 1) * sin_signed


def _proj_kernel(x_ref, mod_ref, lng_ref, w_ref, gain_ref, ca_ref, sa_ref, cc_ref, sc_ref,
                 o_ref, h_ref):
    j = pl.program_id(1)

    @pl.when(j == 0)
    def _():
        x = x_ref[...]
        ms = jnp.mean(x * x, axis=-1, keepdims=True)
        y = x * lax.rsqrt(ms + EPS) * lng_ref[...]
        mod = mod_ref[0]
        shift = mod[:, :D_MODEL]
        scale = mod[:, D_MODEL:2 * D_MODEL]
        h_ref[...] = (y * (1.0 + scale) + shift).astype(BF16)

    acc = jnp.dot(h_ref[...], w_ref[...], preferred_element_type=F32)
    gain = gain_ref[0]
    nblk = PROJ_TILE // LANES

    def blocks(fn):
        for cb in range(nblk):
            sl = slice(cb * LANES, (cb + 1) * LANES)
            o_ref[:, sl] = fn(acc[:, sl], gain[:, sl]).astype(BF16)

    @pl.when(j == T_AQ)
    def _():
        blocks(lambda xb, g: _rope64(_norm64(xb, g), ca_ref[...], sa_ref[...]) * 0.125)

    @pl.when(j == T_AK)
    def _():
        blocks(lambda xb, g: _rope64(_norm64(xb, g), ca_ref[...], sa_ref[...]))

    @pl.when(j == T_BQ)
    def _():
        blocks(lambda xb, g: _norm64(xb, g) * 0.125)

    @pl.when(j == T_BK)
    def _():
        blocks(_norm64)

    @pl.when((j >= T_CQ) & (j < T_CV))
    def _():
        blocks(lambda xb, g: _rope128(_norm128(xb, g), cc_ref[...], sc_ref[...]))

    @pl.when((j == T_AV) | (j == T_BV) | ((j >= T_CV) & (j < T_Z)))
    def _():
        o_ref[...] = acc.astype(BF16)

    @pl.when((j >= T_Z) & (j < T_G))
    def _():
        o_ref[...] = (acc * _sigmoid(acc)).astype(BF16)

    @pl.when(j >= T_G)
    def _():
        o_ref[...] = _sigmoid(acc).astype(BF16)


def _projection(x2, mod3, ln_g, w_in_bf, gains, tabs, seq, tm):
    ntok = x2.shape[0]
    tps = seq // tm
    ca, sa, cc, sc = tabs
    tab_spec = pl.BlockSpec((tm, LANES), lambda i, j: (i % tps, 0))
    return pl.pallas_call(
        _proj_kernel,
        grid=(ntok // tm, N_PROJ_TILES),
        in_specs=[
            pl.BlockSpec((tm, D_MODEL), lambda i, j: (i, 0)),
            pl.BlockSpec((1, 1, 3 * D_MODEL), lambda i, j: (i // tps, 0, 0)),
            pl.BlockSpec((1, D_MODEL), lambda i, j: (0, 0)),
            pl.BlockSpec((D_MODEL, PROJ_TILE), lambda i, j: (0, j)),
            pl.BlockSpec((1, 1, PROJ_TILE), lambda i, j: (j, 0, 0)),
            tab_spec, tab_spec, tab_spec, tab_spec,
        ],
        out_specs=pl.BlockSpec((tm, PROJ_TILE), lambda i, j: (i, j)),
        out_shape=jax.ShapeDtypeStruct((ntok, IN_WIDTH), BF16),
        scratch_shapes=[pltpu.VMEM((tm, D_MODEL), BF16)],
        compiler_params=_cparams(("arbitrary", "arbitrary")),
        name="in_proj",
    )(x2, mod3, ln_g, w_in_bf, gains, ca, sa, cc, sc)


def _diff_attn_kernel(q_ref, k_ref, v_ref, lq1_ref, lk1_ref, lq2_ref, lk2_ref, sub_ref, o_ref,
                      *, lam_init):
    q = q_ref[...]
    k = k_ref[...]
    lo = lax.broadcasted_iota(jnp.int32, q.shape, 1) < 64
    zero = jnp.zeros_like(q)
    dn = (((1,), (1,)), ((), ()))
    s0 = lax.dot_general(jnp.where(lo, q, zero), k, dn, preferred_element_type=F32)
    s1 = lax.dot_general(jnp.where(lo, zero, q), k, dn, preferred_element_type=F32)
    e0 = jnp.exp(s0 - jnp.max(s0, axis=-1, keepdims=True))
    e1 = jnp.exp(s1 - jnp.max(s1, axis=-1, keepdims=True))
    l0 = jnp.sum(e0, axis=-1, keepdims=True)
    l1 = jnp.sum(e1, axis=-1, keepdims=True)
    lam = (jnp.exp(jnp.sum(lq1_ref[...] * lk1_ref[...], axis=-1, keepdims=True))
           - jnp.exp(jnp.sum(lq2_ref[...] * lk2_ref[...], axis=-1, keepdims=True)) + lam_init)
    w = e0 * (1.0 / l0) - e1 * (lam / l1)
    o = jnp.dot(w.astype(BF16), v_ref[...], preferred_element_type=F32)
    ms = jnp.mean(o * o, axis=-1, keepdims=True)
    o_ref[...] = (o * lax.rsqrt(ms + EPS) * sub_ref[...] * (1.0 - lam_init)).astype(BF16)


def _diff_attention(proj3, lam_params, subln, lam_init, tq):
    nb, seq, _ = proj3.shape
    vec = pl.BlockSpec((1, A_QK_DIM), lambda b, h, i: (0, 0))
    return pl.pallas_call(
        functools.partial(_diff_attn_kernel, lam_init=lam_init),
        grid=(nb, A_HEADS, seq // tq),
        in_specs=[
            pl.BlockSpec((None, tq, LANES), lambda b, h, i: (b, i, CB_AQ + h)),
            pl.BlockSpec((None, seq, LANES), lambda b, h, i: (b, 0, CB_AK + h)),
            pl.BlockSpec((None, seq, LANES), lambda b, h, i: (b, 0, CB_AV + h)),
            vec, vec, vec, vec,
            pl.BlockSpec((1, LANES), lambda b, h, i: (0, 0)),
        ],
        out_specs=pl.BlockSpec((None, tq, LANES), lambda b, h, i: (b, i, h)),
        out_shape=jax.ShapeDtypeStruct((nb, seq, BRANCH_WIDTH), BF16),
        compiler_params=_cparams(("arbitrary", "arbitrary", "arbitrary")),
        name="diff_attn",
    )(proj3, proj3, proj3, *lam_params, subln)


def _na_bias_tables(rpb, rows):
    qc = jnp.arange(GRID_W)
    kc = jnp.arange(GRID_W)
    cs = jnp.clip(qc - NA_KW // 2, 0, GRID_W - NA_KW)
    col_ok = (kc[None, :] >= cs[:, None]) & (kc[None, :] < cs[:, None] + NA_KW)
    col_off = jnp.clip(kc[None, :] - qc[:, None], -(NA_KW - 1), NA_KW - 1) + NA_KW - 1
    cols = jnp.where(col_ok, rpb.astype(F32)[:, :, col_off], NEG)
    kh = min(NA_KH, rows)
    tables = []
    for r0, ws in ((0, 0), (NA_QROWS, NA_QROWS - kh // 2), (rows - NA_QROWS, rows - NA_KROWS)):
        r = r0 + jnp.arange(NA_QROWS)
        kr = ws + jnp.arange(NA_KROWS)
        rs = jnp.clip(r - kh // 2, 0, rows - kh)
        row_ok = (kr[None, :] >= rs[:, None]) & (kr[None, :] < rs[:, None] + kh)
        row_off = jnp.clip(kr[None, :] - r[:, None] + NA_KH - 1, 0, 2 * NA_KH - 2)
        t = jnp.where(row_ok[None, :, :, None, None], cols[:, row_off], NEG)
        tables.append(t.transpose(0, 1, 3, 2, 4).reshape(
            B_HEADS, NA_QROWS * GRID_W, NA_KROWS * GRID_W))
    return jnp.stack(tables)


def _na_kernel(q_ref, k_ref, v_ref, bias_ref, o_ref, *, rows):
    i = pl.program_id(2)
    last = pl.num_programs(2) - 1
    kind = jnp.where(i == 0, 0, jnp.where(i == last, 2, 1))
    ws = jnp.clip(i * NA_QROWS - NA_KH // 2, 0, rows - NA_KROWS)
    start = pl.multiple_of(ws * GRID_W, GRID_W)
    nkeys = NA_KROWS * GRID_W
    kw = k_ref[pl.ds(start, nkeys), :]
    vw = v_ref[pl.ds(start, nkeys), :]
    q = q_ref[...]
    lo = lax.broadcasted_iota(jnp.int32, q.shape, 1) < 64
    zero = jnp.zeros_like(q)
    dn = (((1,), (1,)), ((), ()))
    outs = []
    for e in range(2):
        qe = jnp.where(lo, q, zero) if e == 0 else jnp.where(lo, zero, q)
        s = lax.dot_general(qe, kw, dn, preferred_element_type=F32) + bias_ref[kind, e]
        ex = jnp.exp(s - jnp.max(s, axis=-1, keepdims=True))
        p = ex * (1.0 / jnp.sum(ex, axis=-1, keepdims=True))
        outs.append(jnp.dot(p.astype(BF16), vw, preferred_element_type=F32))
    o_ref[...] = jnp.where(lo, outs[0], outs[1]).astype(BF16)


def _neighborhood_attention(proj3, bias):
    nb, seq, _ = proj3.shape
    rows = seq // GRID_W
    nq = NA_QROWS * GRID_W
    nk = NA_KROWS * GRID_W
    return pl.pallas_call(
        functools.partial(_na_kernel, rows=rows),
        grid=(B_HEADS // 2, nb, rows // NA_QROWS),
        in_specs=[
            pl.BlockSpec((None, nq, LANES), lambda p, b, i: (b, i, CB_BQ + p)),
            pl.BlockSpec((None, seq, LANES), lambda p, b, i: (b, 0, CB_BK + p)),
            pl.BlockSpec((None, seq, LANES), lambda p, b, i: (b, 0, CB_BV + p)),
            pl.BlockSpec((3, 2, nq, nk), lambda p, b, i: (0, p, 0, 0)),
        ],
        out_specs=pl.BlockSpec((None, nq, LANES), lambda p, b, i: (b, i, p)),
        out_shape=jax.ShapeDtypeStruct((nb, seq, BRANCH_WIDTH), BF16),
        compiler_params=_cparams(("arbitrary", "arbitrary", "arbitrary")),
        name="nbr_attn",
    )(proj3, proj3, proj3, bias)


def _dil_kernel(q0_ref, q1_ref, q2_ref, k0_ref, k1_ref, k2_ref, v0_ref, v1_ref, v2_ref, o_ref,
                stage_ref, qd_ref, kd_ref, vd_ref, m_ref, l_ref, acc_ref, *, seq):
    scale = C_HEAD_DIM ** -0.5
    dn = (((1,), (1,)), ((), ()))
    groups = ((q0_ref, k0_ref, v0_ref), (q1_ref, k1_ref, v1_ref), (q2_ref, k2_ref, v2_ref))

    for g, dil in enumerate(C_DILATIONS):
        sub_len = seq // dil
        win = min(sub_len, C_QBLK + 2 * C_HALF_WINDOW)
        bps = sub_len // C_QBLK
        qs, ks, vs = groups[g]
        if dil > 1:
            for src, dst in ((qs, qd_ref), (ks, kd_ref), (vs, vd_ref)):
                stage_ref[...] = src[...].astype(F32)
                for rho in range(dil):
                    dst[rho * sub_len:(rho + 1) * sub_len, :] = stage_ref[
                        pl.ds(rho, sub_len, stride=dil), :].astype(BF16)
            qs, ks, vs = qd_ref, kd_ref, vd_ref

        def block(n, carry, qs=qs, ks=ks, vs=vs, dil=dil, sub_len=sub_len, win=win, bps=bps, g=g):
            rho = n // bps
            a = (n % bps) * C_QBLK
            base = rho * sub_len
            ws = jnp.clip(a - C_HALF_WINDOW, 0, sub_len - win)
            q = qs[pl.ds(pl.multiple_of(base + a, C_QBLK), C_QBLK), :]
            kw = ks[pl.ds(pl.multiple_of(base + ws, C_HALF_WINDOW), win), :]
            vw = vs[pl.ds(pl.multiple_of(base + ws, C_HALF_WINDOW), win), :]
            s = lax.dot_general(q, kw, dn, preferred_element_type=F32) * scale
            qi = a + lax.broadcasted_iota(jnp.int32, s.shape, 0)
            kj = ws + lax.broadcasted_iota(jnp.int32, s.shape, 1)
            s = jnp.where(jnp.abs(kj - qi) <= C_HALF_WINDOW, s, NEG)
            m = jnp.max(s, axis=-1, keepdims=True)
            e = jnp.exp(s - m)
            l = jnp.sum(e, axis=-1, keepdims=True)
            acc = jnp.dot(e.astype(BF16), vw, preferred_element_type=F32)
            m = jnp.broadcast_to(m, acc.shape)
            l = jnp.broadcast_to(l, acc.shape)
            if dil == 1:
                rows = pl.ds(pl.multiple_of(a, C_QBLK), C_QBLK)
            else:
                rows = pl.ds(a * dil + rho, C_QBLK, stride=dil)
            if g == 0:
                m_ref[rows, :] = m
                l_ref[rows, :] = l
                acc_ref[rows, :] = acc
            else:
                m_old = m_ref[rows, :]
                m_new = jnp.maximum(m_old, m)
                w_old = jnp.exp(m_old - m_new)
                w_cur = jnp.exp(m - m_new)
                m_ref[rows, :] = m_new
                l_ref[rows, :] = w_old * l_ref[rows, :] + w_cur * l
                acc_ref[rows, :] = w_old * acc_ref[rows, :] + w_cur * acc
            return carry

        lax.fori_loop(0, seq // C_QBLK, block, 0)

    o_ref[...] = (acc_ref[...] / l_ref[...]).astype(BF16)


def _dilated_mixture(proj3):
    nb, seq, _ = proj3.shape

    def spec(cb0, g):
        return pl.BlockSpec((None, seq, LANES),
                            lambda b, h, cb0=cb0, g=g: (b, 0, cb0 + g * C_HEADS_PER_GROUP + h))

    in_specs = [spec(cb0, g) for cb0 in (CB_CQ, CB_CK, CB_CV) for g in range(len(C_DILATIONS))]
    return pl.pallas_call(
        functools.partial(_dil_kernel, seq=seq),
        grid=(nb, C_HEADS_PER_GROUP),
        in_specs=in_specs,
        out_specs=pl.BlockSpec((None, seq, LANES), lambda b, h: (b, 0, h)),
        out_shape=jax.ShapeDtypeStruct((nb, seq, BRANCH_WIDTH), BF16),
        scratch_shapes=[
            pltpu.VMEM((seq, LANES), F32),
            pltpu.VMEM((seq, LANES), BF16),
            pltpu.VMEM((seq, LANES), BF16),
            pltpu.VMEM((seq, LANES), BF16),
            pltpu.VMEM((seq, LANES), F32),
            pltpu.VMEM((seq, LANES), F32),
            pltpu.VMEM((seq, LANES), F32),
        ],
        compiler_params=_cparams(("arbitrary", "arbitrary")),
        name="dil_attn",
    )(*([proj3] * 9))


def _tail_kernel(ya_ref, yb_ref, yc_ref, z0_ref, z1_ref, z2_ref, g0_ref, g1_ref, g2_ref,
                 x_ref, gate_ref, wbr_ref, wout_ref, o_ref):
    merged = None
    branches = ((ya_ref, z0_ref, g0_ref), (yb_ref, z1_ref, g1_ref), (yc_ref, z2_ref, g2_ref))
    for i, (y_ref, z_ref, g_ref) in enumerate(branches):
        yz = (y_ref[...].astype(F32) * z_ref[...].astype(F32)).astype(BF16)
        u = g_ref[...].astype(F32) * jnp.dot(yz, wbr_ref[i], preferred_element_type=F32)
        merged = u if merged is None else merged + u
    out = jnp.dot(merged.astype(BF16), wout_ref[...], preferred_element_type=F32)
    o_ref[...] = x_ref[...] + gate_ref[0] * out


def _tail(ya, yb, yc, proj2, x2, mod3, w_br_bf, w_out_bf, seq, tm):
    ntok = x2.shape[0]
    tps = seq // tm
    y_spec = pl.BlockSpec((tm, BRANCH_WIDTH), lambda i: (i, 0))
    z_specs = [pl.BlockSpec((tm, BRANCH_WIDTH), lambda i, t=t: (i, T_Z + t)) for t in range(3)]
    g0 = T_G * PROJ_TILE // D_MODEL
    g_specs = [pl.BlockSpec((tm, D_MODEL), lambda i, t=t: (i, g0 + t)) for t in range(3)]
    return pl.pallas_call(
        _tail_kernel,
        grid=(ntok // tm,),
        in_specs=[y_spec, y_spec, y_spec, *z_specs, *g_specs,
                  pl.BlockSpec((tm, D_MODEL), lambda i: (i, 0)),
                  pl.BlockSpec((1, 1, D_MODEL), lambda i: (i // tps, 0, 2)),
                  pl.BlockSpec((3, BRANCH_WIDTH, D_MODEL), lambda i: (0, 0, 0)),
                  pl.BlockSpec((D_MODEL, D_MODEL), lambda i: (0, 0))],
        out_specs=pl.BlockSpec((tm, D_MODEL), lambda i: (i, 0)),
        out_shape=jax.ShapeDtypeStruct((ntok, D_MODEL), F32),
        compiler_params=_cparams(("arbitrary",)),
        name="tail",
    )(ya, yb, yc, proj2, proj2, proj2, proj2, proj2, proj2, x2, mod3, w_br_bf, w_out_bf)


def _rope_tables(seq):
    def base(d):
        inv = ROPE_THETA ** (-jnp.arange(0, d, 2, dtype=F32) / d)
        ang = jnp.arange(seq, dtype=F32)[:, None] * inv[None, :]
        return jnp.cos(ang), jnp.sin(ang)

    cos_a, sin_a = base(A_QK_DIM)
    ca = jnp.tile(cos_a, (1, 4))
    sa = jnp.tile(jnp.concatenate([-sin_a, sin_a], axis=-1), (1, 2))
    cos_c, sin_c = base(C_HEAD_DIM)
    cc = jnp.tile(cos_c, (1, 2))
    sc = jnp.concatenate([-sin_c, sin_c], axis=-1)
    return ca, sa, cc, sc


def _proj_gains(qn_a, kn_a, qn_b, kn_b, qn_c, kn_c):
    ones = jnp.ones((PROJ_TILE,), F32)
    rows = [ones] * N_PROJ_TILES
    rows[T_AQ] = jnp.tile(qn_a, PROJ_TILE // A_QK_DIM)
    rows[T_AK] = jnp.tile(kn_a, PROJ_TILE // A_QK_DIM)
    rows[T_BQ] = jnp.tile(qn_b, PROJ_TILE // B_HEAD_DIM)
    rows[T_BK] = jnp.tile(kn_b, PROJ_TILE // B_HEAD_DIM)
    for t in range(3):
        rows[T_CQ + t] = jnp.tile(qn_c, PROJ_TILE // C_HEAD_DIM)
        rows[T_CK + t] = jnp.tile(kn_c, PROJ_TILE // C_HEAD_DIM)
    return jnp.stack(rows).astype(F32).reshape(N_PROJ_TILES, 1, PROJ_TILE)


def _token_tile(seq):
    return min(512, seq)


def _encoder_layer(x, mod3, layer_idx, tabs, ln_g, w_in_bf, gains, lam_params, subln, na_bias,
                   w_br_bf, w_out_bf):
    nb, seq, _ = x.shape
    tm = _token_tile(seq)
    x2 = x.reshape(nb * seq, D_MODEL)
    proj2 = _projection(x2, mod3, ln_g, w_in_bf, gains, tabs, seq, tm)
    proj3 = proj2.reshape(nb, seq, IN_WIDTH)
    lam_init = 0.8 - 0.6 * math.exp(-0.3 * layer_idx)
    ya = _diff_attention(proj3, lam_params, subln, lam_init, tq=min(256, seq))
    yb = _neighborhood_attention(proj3, na_bias)
    yc = _dilated_mixture(proj3)
    flat = lambda y: y.reshape(nb * seq, BRANCH_WIDTH)
    y2 = _tail(flat(ya), flat(yb), flat(yc), proj2, x2, mod3, w_br_bf, w_out_bf, seq, tm)
    return y2.reshape(nb, seq, D_MODEL)


def kernel(x_prompt, x_sample, c_prompt, c_sample, ln_g, w_ada, b_ada, w_in, qn_a, kn_a, lam_q1, lam_k1, lam_q2, lam_k2, subln_a, qn_b, kn_b, rpb_b, qn_c, kn_c, w_br, w_out):
    depth = w_in.shape[0]
    n_prompt = c_prompt.shape[0]
    mod_all = _modulation(jnp.concatenate([c_prompt, c_sample], axis=0), w_ada, b_ada)
    w_in_bf = w_in.astype(BF16)
    w_br_bf = w_br.astype(BF16)
    w_out_bf = w_out.astype(BF16)

    def run(x, mod):
        nb, seq, _ = x.shape
        tabs = _rope_tables(seq)
        for l in range(depth):
            gains = _proj_gains(qn_a[l], kn_a[l], qn_b[l], kn_b[l], qn_c[l], kn_c[l])
            lam_params = tuple(p[l].reshape(1, A_QK_DIM) for p in (lam_q1, lam_k1, lam_q2, lam_k2))
            na_bias = _na_bias_tables(rpb_b[l], seq // GRID_W)
            x = _encoder_layer(x, mod[l].reshape(nb, 1, 3 * D_MODEL), l, tabs,
                               ln_g[l].reshape(1, D_MODEL), w_in_bf[l], gains, lam_params,
                               subln_a[l].reshape(1, LANES), na_bias, w_br_bf[l], w_out_bf[l])
        return x

    y_prompt = run(x_prompt, mod_all[:, :n_prompt])
    y_sample = run(x_sample, mod_all[:, n_prompt:])
    return (y_prompt, y_sample)
```

```python
import functools
import math

import jax
import jax.numpy as jnp
from jax import lax
from jax.experimental import pallas as pl
from jax.experimental.pallas import tpu as pltpu

F32 = jnp.float32
BF16 = jnp.bfloat16

D_MODEL = 1024
GRID_W = 64
BRANCH_WIDTH = 512
ROPE_THETA = 10000.0
EPS = 1e-6
NEG = -1e30
A_QK_DIM = 64
A_HEADS = 4
B_HEAD_DIM = 64
B_HEADS = 8
NA_KH = 8
NA_KW = 16
C_DILATIONS = (1, 4, 16)
C_HALF_WINDOW = 64
C_HEAD_DIM = 128
C_HEADS_PER_GROUP = 4
IN_WIDTH = 12288
LANES = 128

PROJ_TILE = 512
N_PROJ_TILES = IN_WIDTH // PROJ_TILE
T_AQ, T_AK, T_AV, T_BQ, T_BK, T_BV = 0, 1, 2, 3, 4, 5
T_CQ, T_CK, T_CV, T_Z, T_G = 6, 9, 12, 15, 18
CB_AQ, CB_AK, CB_AV = 0, 4, 8
CB_BQ, CB_BK, CB_BV = 12, 16, 20
CB_CQ, CB_CK, CB_CV = 24, 36, 48

VMEM_LIMIT = 56 * 1024 * 1024

NA_QROWS = 8
NA_KROWS = 16
NA_CHUNK = 256
C_QBLK = 128
C_UNROLL = 8


def _cparams(sem):
    return pltpu.CompilerParams(dimension_semantics=sem, vmem_limit_bytes=VMEM_LIMIT)


def _sigmoid(x):
    return 1.0 / (1.0 + jnp.exp(-x))


def _mod_kernel(c_ref, w_ref, b_ref, o_ref):
    c = c_ref[...]
    o_ref[0] = jnp.dot(c * _sigmoid(c), w_ref[0], preferred_element_type=F32) + b_ref[0]


def _modulation(c_all, w_ada, b_ada):
    depth = w_ada.shape[0]
    nb = c_all.shape[0]
    return pl.pallas_call(
        _mod_kernel,
        grid=(depth, 3),
        in_specs=[
            pl.BlockSpec((nb, D_MODEL), lambda l, j: (0, 0)),
            pl.BlockSpec((1, D_MODEL, D_MODEL), lambda l, j: (l, 0, j)),
            pl.BlockSpec((1, 1, D_MODEL), lambda l, j: (l, 0, j)),
        ],
        out_specs=pl.BlockSpec((1, nb, D_MODEL), lambda l, j: (l, 0, j)),
        out_shape=jax.ShapeDtypeStruct((depth, nb, 3 * D_MODEL), F32),
        compiler_params=_cparams(("arbitrary", "arbitrary")),
        name="adaln_mod",
    )(c_all, w_ada, b_ada.reshape(depth, 1, 3 * D_MODEL))


def _norm64(xb, gain):
    lo = lax.broadcasted_iota(jnp.int32, xb.shape, 1) < 64
    sq = xb * xb
    s_lo = jnp.sum(jnp.where(lo, sq, 0.0), axis=-1, keepdims=True)
    s_hi = jnp.sum(jnp.where(lo, 0.0, sq), axis=-1, keepdims=True)
    ms = jnp.where(lo, s_lo, s_hi) * (1.0 / 64.0)
    return xb * lax.rsqrt(ms + EPS) * gain


def _norm128(xb, gain):
    ms = jnp.mean(xb * xb, axis=-1, keepdims=True)
    return xb * lax.rsqrt(ms + EPS) * gain


def _rope64(xb, cos, sin_signed):
    first = (lax.broadcasted_iota(jnp.int32, xb.shape, 1) % 64) < 32
    partner = jnp.where(first, pltpu.roll(xb, 96, 1), pltpu.roll(xb, 32, 1))
    return xb * cos + partner * sin_signed


def _rope128(xb, cos, sin_signed):
    return xb * cos + pltpu.roll(xb, 64, 1) * sin_signed


def _proj_kernel(x_ref, mod_ref, lng_ref, w_ref, gain_ref, ca_ref, sa_ref, cc_ref, sc_ref,
                 o_ref, h_ref):
    j = pl.program_id(1)

    @pl.when(j == 0)
    def _():
        x = x_ref[...]
        ms = jnp.mean(x * x, axis=-1, keepdims=True)
        y = x * lax.rsqrt(ms + EPS) * lng_ref[...]
        mod = mod_ref[0]
        shift = mod[:, :D_MODEL]
        scale = mod[:, D_MODEL:2 * D_MODEL]
        h_ref[...] = (y * (1.0 + scale) + shift).astype(BF16)

    acc = jnp.dot(h_ref[...], w_ref[...], preferred_element_type=F32)
    gain = gain_ref[0]
    nblk = PROJ_TILE // LANES

    def blocks(fn):
        for cb in range(nblk):
            sl = slice(cb * LANES, (cb + 1) * LANES)
            o_ref[:, sl] = fn(acc[:, sl], gain[:, sl]).astype(BF16)

    @pl.when(j == T_AQ)
    def _():
        blocks(lambda xb, g: _rope64(_norm64(xb, g), ca_ref[...], sa_ref[...]) * 0.125)

    @pl.when(j == T_AK)
    def _():
        blocks(lambda xb, g: _rope64(_norm64(xb, g), ca_ref[...], sa_ref[...]))

    @pl.when(j == T_BQ)
    def _():
        blocks(lambda xb, g: _norm64(xb, g) * 0.125)

    @pl.when(j == T_BK)
    def _():
        blocks(_norm64)

    @pl.when((j >= T_CQ) & (j < T_CV))
    def _():
        blocks(lambda xb, g: _rope128(_norm128(xb, g), cc_ref[...], sc_ref[...]))

    @pl.when((j == T_AV) | (j == T_BV) | ((j >= T_CV) & (j < T_Z)))
    def _():
        o_ref[...] = acc.astype(BF16)

    @pl.when((j >= T_Z) & (j < T_G))
    def _():
        o_ref[...] = (acc * _sigmoid(acc)).astype(BF16)

    @pl.when(j >= T_G)
    def _():
        o_ref[...] = _sigmoid(acc).astype(BF16)


def _projection(x2, mod3, ln_g, w_in_bf, gains, tabs, seq, tm):
    ntok = x2.shape[0]
    tps = seq // tm
    ca, sa, cc, sc = tabs
    tab_spec = pl.BlockSpec((tm, LANES), lambda i, j: (i % tps, 0))
    return pl.pallas_call(
        _proj_kernel,
        grid=(ntok // tm, N_PROJ_TILES),
        in_specs=[
            pl.BlockSpec((tm, D_MODEL), lambda i, j: (i, 0)),
            pl.BlockSpec((1, 1, 3 * D_MODEL), lambda i, j: (i // tps, 0, 0)),
            pl.BlockSpec((1, D_MODEL), lambda i, j: (0, 0)),
            pl.BlockSpec((D_MODEL, PROJ_TILE), lambda i, j: (0, j)),
            pl.BlockSpec((1, 1, PROJ_TILE), lambda i, j: (j, 0, 0)),
            tab_spec, tab_spec, tab_spec, tab_spec,
        ],
        out_specs=pl.BlockSpec((tm, PROJ_TILE), lambda i, j: (i, j)),
        out_shape=jax.ShapeDtypeStruct((ntok, IN_WIDTH), BF16),
        scratch_shapes=[pltpu.VMEM((tm, D_MODEL), BF16)],
        compiler_params=_cparams(("arbitrary", "arbitrary")),
        name="in_proj",
    )(x2, mod3, ln_g, w_in_bf, gains, ca, sa, cc, sc)


def _diff_attn_kernel(q_ref, k_ref, v_ref, lq1_ref, lk1_ref, lq2_ref, lk2_ref, sub_ref, o_ref,
                      *, lam_init, chunk):
    k = k_ref[...]
    v = v_ref[...]
    lam = (jnp.exp(jnp.sum(lq1_ref[...] * lk1_ref[...], axis=-1, keepdims=True))
           - jnp.exp(jnp.sum(lq2_ref[...] * lk2_ref[...], axis=-1, keepdims=True)) + lam_init)
    dn = (((1,), (1,)), ((), ()))
    for c in range(q_ref.shape[0] // chunk):
        rows = slice(c * chunk, (c + 1) * chunk)
        q = q_ref[rows, :]
        lo = lax.broadcasted_iota(jnp.int32, q.shape, 1) < 64
        zero = jnp.zeros_like(q)
        s0 = lax.dot_general(jnp.where(lo, q, zero), k, dn, preferred_element_type=F32)
        s1 = lax.dot_general(jnp.where(lo, zero, q), k, dn, preferred_element_type=F32)
        e0 = jnp.exp(s0 - jnp.max(s0, axis=-1, keepdims=True))
        e1 = jnp.exp(s1 - jnp.max(s1, axis=-1, keepdims=True))
        l0 = jnp.sum(e0, axis=-1, keepdims=True)
        l1 = jnp.sum(e1, axis=-1, keepdims=True)
        w = e0 * (1.0 / l0) - e1 * (lam / l1)
        o = jnp.dot(w.astype(BF16), v, preferred_element_type=F32)
        ms = jnp.mean(o * o, axis=-1, keepdims=True)
        o_ref[rows, :] = (o * lax.rsqrt(ms + EPS) * sub_ref[...] * (1.0 - lam_init)).astype(BF16)


def _diff_attention(proj3, lam_params, subln, lam_init, tq, chunk):
    nb, seq, _ = proj3.shape
    vec = pl.BlockSpec((1, A_QK_DIM), lambda b, h, i: (0, 0))
    return pl.pallas_call(
        functools.partial(_diff_attn_kernel, lam_init=lam_init, chunk=chunk),
        grid=(nb, A_HEADS, seq // tq),
        in_specs=[
            pl.BlockSpec((None, tq, LANES), lambda b, h, i: (b, i, CB_AQ + h)),
            pl.BlockSpec((None, seq, LANES), lambda b, h, i: (b, 0, CB_AK + h)),
            pl.BlockSpec((None, seq, LANES), lambda b, h, i: (b, 0, CB_AV + h)),
            vec, vec, vec, vec,
            pl.BlockSpec((1, LANES), lambda b, h, i: (0, 0)),
        ],
        out_specs=pl.BlockSpec((None, tq, LANES), lambda b, h, i: (b, i, h)),
        out_shape=jax.ShapeDtypeStruct((nb, seq, BRANCH_WIDTH), BF16),
        compiler_params=_cparams(("arbitrary", "arbitrary", "arbitrary")),
        name="diff_attn",
    )(proj3, proj3, proj3, *lam_params, subln)


def _na_bias_tables(rpb, rows):
    qc = jnp.arange(GRID_W)
    kc = jnp.arange(GRID_W)
    cs = jnp.clip(qc - NA_KW // 2, 0, GRID_W - NA_KW)
    col_ok = (kc[None, :] >= cs[:, None]) & (kc[None, :] < cs[:, None] + NA_KW)
    col_off = jnp.clip(kc[None, :] - qc[:, None], -(NA_KW - 1), NA_KW - 1) + NA_KW - 1
    cols = jnp.where(col_ok, rpb.astype(F32)[:, :, col_off], NEG)
    kh = min(NA_KH, rows)
    tables = []
    for r0, ws in ((0, 0), (NA_QROWS, NA_QROWS - kh // 2), (rows - NA_QROWS, rows - NA_KROWS)):
        r = r0 + jnp.arange(NA_QROWS)
        kr = ws + jnp.arange(NA_KROWS)
        rs = jnp.clip(r - kh // 2, 0, rows - kh)
        row_ok = (kr[None, :] >= rs[:, None]) & (kr[None, :] < rs[:, None] + kh)
        row_off = jnp.clip(kr[None, :] - r[:, None] + NA_KH - 1, 0, 2 * NA_KH - 2)
        t = jnp.where(row_ok[None, :, :, None, None], cols[:, row_off], NEG)
        tables.append(t.transpose(0, 1, 3, 2, 4).reshape(
            B_HEADS, NA_QROWS * GRID_W, NA_KROWS * GRID_W))
    return jnp.stack(tables)


def _na_kernel(q_ref, k_ref, v_ref, bias_ref, o_ref, *, rows):
    i = pl.program_id(2)
    last = pl.num_programs(2) - 1
    kind = jnp.where(i == 0, 0, jnp.where(i == last, 2, 1))
    ws = jnp.clip(i * NA_QROWS - NA_KH // 2, 0, rows - NA_KROWS)
    start = pl.multiple_of(ws * GRID_W, GRID_W)
    nkeys = NA_KROWS * GRID_W
    kw = k_ref[pl.ds(start, nkeys), :]
    vw = v_ref[pl.ds(start, nkeys), :]
    dn = (((1,), (1,)), ((), ()))
    for c in range(q_ref.shape[0] // NA_CHUNK):
        rows = slice(c * NA_CHUNK, (c + 1) * NA_CHUNK)
        q = q_ref[rows, :]
        lo = lax.broadcasted_iota(jnp.int32, q.shape, 1) < 64
        zero = jnp.zeros_like(q)
        outs = []
        for e in range(2):
            qe = jnp.where(lo, q, zero) if e == 0 else jnp.where(lo, zero, q)
            s = lax.dot_general(qe, kw, dn, preferred_element_type=F32) + bias_ref[kind, e, rows, :]
            ex = jnp.exp(s - jnp.max(s, axis=-1, keepdims=True))
            p = ex * (1.0 / jnp.sum(ex, axis=-1, keepdims=True))
            outs.append(jnp.dot(p.astype(BF16), vw, preferred_element_type=F32))
        o_ref[rows, :] = jnp.where(lo, outs[0], outs[1]).astype(BF16)


def _neighborhood_attention(proj3, bias):
    nb, seq, _ = proj3.shape
    rows = seq // GRID_W
    nq = NA_QROWS * GRID_W
    nk = NA_KROWS * GRID_W
    return pl.pallas_call(
        functools.partial(_na_kernel, rows=rows),
        grid=(B_HEADS // 2, nb, rows // NA_QROWS),
        in_specs=[
            pl.BlockSpec((None, nq, LANES), lambda p, b, i: (b, i, CB_BQ + p)),
            pl.BlockSpec((None, seq, LANES), lambda p, b, i: (b, 0, CB_BK + p)),
            pl.BlockSpec((None, seq, LANES), lambda p, b, i: (b, 0, CB_BV + p)),
            pl.BlockSpec((3, 2, nq, nk), lambda p, b, i: (0, p, 0, 0)),
        ],
        out_specs=pl.BlockSpec((None, nq, LANES), lambda p, b, i: (b, i, p)),
        out_shape=jax.ShapeDtypeStruct((nb, seq, BRANCH_WIDTH), BF16),
        compiler_params=_cparams(("arbitrary", "arbitrary", "arbitrary")),
        name="nbr_attn",
    )(proj3, proj3, proj3, bias)


def _dil_kernel(q0_ref, q1_ref, q2_ref, k0_ref, k1_ref, k2_ref, v0_ref, v1_ref, v2_ref, o_ref,
                stage_ref, qd_ref, kd_ref, vd_ref, og_ref, lg_ref, on_ref, ln_ref, oacc_ref,
                lacc_ref, band_ref, *, seq):
    scale = C_HEAD_DIM ** -0.5
    dn = (((1,), (1,)), ((), ()))
    groups = ((q0_ref, k0_ref, v0_ref), (q1_ref, k1_ref, v1_ref), (q2_ref, k2_ref, v2_ref))

    full_win = C_QBLK + 2 * C_HALF_WINDOW
    qi = lax.broadcasted_iota(jnp.int32, (C_QBLK, full_win), 0)
    kj = lax.broadcasted_iota(jnp.int32, (C_QBLK, full_win), 1)
    for lead in range(3):
        band_ref[lead] = jnp.where(
            jnp.abs(kj - qi - lead * C_HALF_WINDOW) <= C_HALF_WINDOW, 0.0, NEG)

    for g, dil in enumerate(C_DILATIONS):
        sub_len = seq // dil
        win = min(sub_len, C_QBLK + 2 * C_HALF_WINDOW)
        bps = sub_len // C_QBLK
        qs, ks, vs = groups[g]
        if dil > 1:
            for src, dst in ((qs, qd_ref), (ks, kd_ref), (vs, vd_ref)):
                stage_ref[...] = src[...].astype(F32)
                for rho in range(dil):
                    dst[rho * sub_len:(rho + 1) * sub_len, :] = stage_ref[
                        pl.ds(rho, sub_len, stride=dil), :].astype(BF16)
            qs, ks, vs = qd_ref, kd_ref, vd_ref
        o_dst, l_dst = (oacc_ref, lacc_ref) if dil == 1 else (og_ref, lg_ref)

        def attend(n, qs=qs, ks=ks, vs=vs, sub_len=sub_len, win=win, bps=bps, o_dst=o_dst,
                   l_dst=l_dst):
            a = (n % bps) * C_QBLK
            base = (n // bps) * sub_len
            ws = jnp.clip(a - C_HALF_WINDOW, 0, sub_len - win)
            rows = pl.ds(pl.multiple_of(n * C_QBLK, C_QBLK), C_QBLK)
            q = qs[rows, :]
            kw = ks[pl.ds(pl.multiple_of(base + ws, C_HALF_WINDOW), win), :]
            vw = vs[pl.ds(pl.multiple_of(base + ws, C_HALF_WINDOW), win), :]
            band = band_ref[(a - ws) // C_HALF_WINDOW, :, :win]
            s = lax.dot_general(q, kw, dn, preferred_element_type=F32) * scale + band
            m = jnp.max(s, axis=-1, keepdims=True)
            e = jnp.exp(s - m)
            l = jnp.sum(e, axis=-1, keepdims=True)
            acc = jnp.dot(e.astype(BF16), vw, preferred_element_type=F32)
            o_dst[rows, :] = acc * (1.0 / l)
            l_dst[rows, :] = jnp.broadcast_to(m + jnp.log(l), acc.shape)

        def blocks(it, carry, attend=attend):
            for u in range(C_UNROLL):
                attend(it * C_UNROLL + u)
            return carry

        lax.fori_loop(0, seq // (C_QBLK * C_UNROLL), blocks, 0)

        if dil > 1:
            top = C_DILATIONS[-1]
            run = seq // top
            pairs = ((og_ref, on_ref), (lg_ref, ln_ref))
            if dil < top:
                for src, dst in pairs:
                    for rho in range(dil):
                        for c in range(top // dil):
                            r = c * dil + rho
                            dst[r * run:(r + 1) * run, :] = src[
                                pl.ds(rho * sub_len + c, run, stride=top // dil), :]
                pairs = ((on_ref, og_ref), (ln_ref, lg_ref))

            def to_token_order(j, carry, pairs=pairs, top=top, run=run):
                for src, dst in pairs:
                    dst[pl.ds(pl.multiple_of(j * top, top), top), :] = src[
                        pl.ds(j, top, stride=run), :]
                return carry

            lax.fori_loop(0, run, to_token_order, 0, unroll=8)
            o_cur_ref, l_cur_ref = pairs[0][1], pairs[1][1]
            l_old = lacc_ref[...]
            l_cur = l_cur_ref[...]
            l_max = jnp.maximum(l_old, l_cur)
            w_old = jnp.exp(l_old - l_max)
            w_cur = jnp.exp(l_cur - l_max)
            inv = 1.0 / (w_old + w_cur)
            oacc_ref[...] = (w_old * oacc_ref[...] + w_cur * o_cur_ref[...]) * inv
            lacc_ref[...] = l_max + jnp.log(w_old + w_cur)

    o_ref[...] = oacc_ref[...].astype(BF16)


def _dilated_mixture(proj3):
    nb, seq, _ = proj3.shape

    def spec(cb0, g):
        return pl.BlockSpec((None, seq, LANES),
                            lambda b, h, cb0=cb0, g=g: (b, 0, cb0 + g * C_HEADS_PER_GROUP + h))

    in_specs = [spec(cb0, g) for cb0 in (CB_CQ, CB_CK, CB_CV) for g in range(len(C_DILATIONS))]
    return pl.pallas_call(
        functools.partial(_dil_kernel, seq=seq),
        grid=(nb, C_HEADS_PER_GROUP),
        in_specs=in_specs,
        out_specs=pl.BlockSpec((None, seq, LANES), lambda b, h: (b, 0, h)),
        out_shape=jax.ShapeDtypeStruct((nb, seq, BRANCH_WIDTH), BF16),
        scratch_shapes=[
            pltpu.VMEM((seq, LANES), F32),
            pltpu.VMEM((seq, LANES), BF16),
            pltpu.VMEM((seq, LANES), BF16),
            pltpu.VMEM((seq, LANES), BF16),
            pltpu.VMEM((seq, LANES), F32),
            pltpu.VMEM((seq, LANES), F32),
            pltpu.VMEM((seq, LANES), F32),
            pltpu.VMEM((seq, LANES), F32),
            pltpu.VMEM((seq, LANES), F32),
            pltpu.VMEM((seq, LANES), F32),
            pltpu.VMEM((3, C_QBLK, C_QBLK + 2 * C_HALF_WINDOW), F32),
        ],
        compiler_params=_cparams(("arbitrary", "arbitrary")),
        name="dil_attn",
    )(*([proj3] * 9))


def _tail_kernel(ya_ref, yb_ref, yc_ref, z0_ref, z1_ref, z2_ref, g0_ref, g1_ref, g2_ref,
                 x_ref, gate_ref, wbr_ref, wout_ref, o_ref):
    merged = None
    branches = ((ya_ref, z0_ref, g0_ref), (yb_ref, z1_ref, g1_ref), (yc_ref, z2_ref, g2_ref))
    for i, (y_ref, z_ref, g_ref) in enumerate(branches):
        yz = (y_ref[...].astype(F32) * z_ref[...].astype(F32)).astype(BF16)
        u = g_ref[...].astype(F32) * jnp.dot(yz, wbr_ref[i], preferred_element_type=F32)
        merged = u if merged is None else merged + u
    out = jnp.dot(merged.astype(BF16), wout_ref[...], preferred_element_type=F32)
    o_ref[...] = x_ref[...] + gate_ref[0] * out


def _tail(ya, yb, yc, proj2, x2, mod3, w_br_bf, w_out_bf, seq, tm):
    ntok = x2.shape[0]
    tps = seq // tm
    y_spec = pl.BlockSpec((tm, BRANCH_WIDTH), lambda i: (i, 0))
    z_specs = [pl.BlockSpec((tm, BRANCH_WIDTH), lambda i, t=t: (i, T_Z + t)) for t in range(3)]
    g0 = T_G * PROJ_TILE // D_MODEL
    g_specs = [pl.BlockSpec((tm, D_MODEL), lambda i, t=t: (i, g0 + t)) for t in range(3)]
    return pl.pallas_call(
        _tail_kernel,
        grid=(ntok // tm,),
        in_specs=[y_spec, y_spec, y_spec, *z_specs, *g_specs,
                  pl.BlockSpec((tm, D_MODEL), lambda i: (i, 0)),
                  pl.BlockSpec((1, 1, D_MODEL), lambda i: (i // tps, 0, 2)),
                  pl.BlockSpec((3, BRANCH_WIDTH, D_MODEL), lambda i: (0, 0, 0)),
                  pl.BlockSpec((D_MODEL, D_MODEL), lambda i: (0, 0))],
        out_specs=pl.BlockSpec((tm, D_MODEL), lambda i: (i, 0)),
        out_shape=jax.ShapeDtypeStruct((ntok, D_MODEL), F32),
        compiler_params=_cparams(("arbitrary",)),
        name="tail",
    )(ya, yb, yc, proj2, proj2, proj2, proj2, proj2, proj2, x2, mod3, w_br_bf, w_out_bf)


def _rope_tables(seq):
    def base(d):
        inv = ROPE_THETA ** (-jnp.arange(0, d, 2, dtype=F32) / d)
        ang = jnp.arange(seq, dtype=F32)[:, None] * inv[None, :]
        return jnp.cos(ang), jnp.sin(ang)

    cos_a, sin_a = base(A_QK_DIM)
    ca = jnp.tile(cos_a, (1, 4))
    sa = jnp.tile(jnp.concatenate([-sin_a, sin_a], axis=-1), (1, 2))
    cos_c, sin_c = base(C_HEAD_DIM)
    cc = jnp.tile(cos_c, (1, 2))
    sc = jnp.concatenate([-sin_c, sin_c], axis=-1)
    return ca, sa, cc, sc


def _proj_gains(qn_a, kn_a, qn_b, kn_b, qn_c, kn_c):
    ones = jnp.ones((PROJ_TILE,), F32)
    rows = [ones] * N_PROJ_TILES
    rows[T_AQ] = jnp.tile(qn_a, PROJ_TILE // A_QK_DIM)
    rows[T_AK] = jnp.tile(kn_a, PROJ_TILE // A_QK_DIM)
    rows[T_BQ] = jnp.tile(qn_b, PROJ_TILE // B_HEAD_DIM)
    rows[T_BK] = jnp.tile(kn_b, PROJ_TILE // B_HEAD_DIM)
    for t in range(3):
        rows[T_CQ + t] = jnp.tile(qn_c, PROJ_TILE // C_HEAD_DIM)
        rows[T_CK + t] = jnp.tile(kn_c, PROJ_TILE // C_HEAD_DIM)
    return jnp.stack(rows).astype(F32).reshape(N_PROJ_TILES, 1, PROJ_TILE)


def _token_tile(seq):
    return min(512, seq)


def _encoder_layer(x, mod3, layer_idx, tabs, ln_g, w_in_bf, gains, lam_params, subln, na_bias,
                   w_br_bf, w_out_bf):
    nb, seq, _ = x.shape
    tm = _token_tile(seq)
    x2 = x.reshape(nb * seq, D_MODEL)
    proj2 = _projection(x2, mod3, ln_g, w_in_bf, gains, tabs, seq, tm)
    proj3 = proj2.reshape(nb, seq, IN_WIDTH)
    lam_init = 0.8 - 0.6 * math.exp(-0.3 * layer_idx)
    a_chunk = max(128, 256 * 2048 // seq)
    ya = _diff_attention(proj3, lam_params, subln, lam_init, tq=4 * a_chunk, chunk=a_chunk)
    yb = _neighborhood_attention(proj3, na_bias)
    yc = _dilated_mixture(proj3)
    flat = lambda y: y.reshape(nb * seq, BRANCH_WIDTH)
    y2 = _tail(flat(ya), flat(yb), flat(yc), proj2, x2, mod3, w_br_bf, w_out_bf, seq, tm)
    return y2.reshape(nb, seq, D_MODEL)


def kernel(x_prompt, x_sample, c_prompt, c_sample, ln_g, w_ada, b_ada, w_in, qn_a, kn_a, lam_q1, lam_k1, lam_q2, lam_k2, subln_a, qn_b, kn_b, rpb_b, qn_c, kn_c, w_br, w_out):
    depth = w_in.shape[0]
    n_prompt = c_prompt.shape[0]
    mod_all = _modulation(jnp.concatenate([c_prompt, c_sample], axis=0), w_ada, b_ada)
    w_in_bf = w_in.astype(BF16)
    w_br_bf = w_br.astype(BF16)
    w_out_bf = w_out.astype(BF16)

    def run(x, mod):
        nb, seq, _ = x.shape
        tabs = _rope_tables(seq)
        for l in range(depth):
            gains = _proj_gains(qn_a[l], kn_a[l], qn_b[l], kn_b[l], qn_c[l], kn_c[l])
            lam_params = tuple(p[l].reshape(1, A_QK_DIM) for p in (lam_q1, lam_k1, lam_q2, lam_k2))
            na_bias = _na_bias_tables(rpb_b[l], seq // GRID_W)
            x = _encoder_layer(x, mod[l].reshape(nb, 1, 3 * D_MODEL), l, tabs,
                               ln_g[l].reshape(1, D_MODEL), w_in_bf[l], gains, lam_params,
                               subln_a[l].reshape(1, LANES), na_bias, w_br_bf[l], w_out_bf[l])
        return x

    y_prompt = run(x_prompt, mod_all[:, :n_prompt])
    y_sample = run(x_sample, mod_all[:, n_prompt:])
    return (y_prompt, y_sample)
```

```python
import functools
import math

import jax
import jax.numpy as jnp
import numpy as np
from jax import lax
from jax.experimental import pallas as pl
from jax.experimental.pallas import tpu as pltpu

F32 = jnp.float32
BF16 = jnp.bfloat16

D_MODEL = 1024
GRID_W = 64
BRANCH_WIDTH = 512
ROPE_THETA = 10000.0
EPS = 1e-6
NEG = -1e30
A_QK_DIM = 64
A_HEADS = 4
B_HEAD_DIM = 64
B_HEADS = 8
NA_KH = 8
NA_KW = 16
C_DILATIONS = (1, 4, 16)
C_HALF_WINDOW = 64
C_HEAD_DIM = 128
C_HEADS_PER_GROUP = 4
IN_WIDTH = 12288
LANES = 128

PROJ_TILE = 512
TILE_NAMES = ("AQ", "AK", "AV", "BQ", "BK", "BV", "CQ0", "CQ1", "CQ2", "CK0", "CK1", "CK2",
              "CV0", "CV1", "CV2", "Z0", "Z1", "Z2", "G0a", "G0b", "G1a", "G1b", "G2a", "G2b")
PROJ_GROUPS = (("AQ", "Z0", "BQ", "AV"), ("AK", "Z1", "BK", "BV"),
               ("CQ0", "G0a", "CK0", "CV0"), ("CQ1", "G0b", "CK1", "CV1"),
               ("CQ2", "G1a", "CK2", "CV2"), ("Z2", "G1b", "G2a", "G2b"))
GROUP_TILES = len(PROJ_GROUPS[0])
GROUP_WIDTH = GROUP_TILES * PROJ_TILE
SLOT = {name: i for i, name in enumerate(n for grp in PROJ_GROUPS for n in grp)}


def _col_block(name, sub=0):
    return SLOT[name] * (PROJ_TILE // LANES) + sub

VMEM_LIMIT = 56 * 1024 * 1024

NA_QROWS = 8
NA_KROWS = 16
NA_CHUNK = 512
C_QBLK = 128
C_UNROLL = 8


def _cparams(sem):
    return pltpu.CompilerParams(dimension_semantics=sem, vmem_limit_bytes=VMEM_LIMIT)


def _sigmoid(x):
    return 1.0 / (1.0 + jnp.exp(-x))


def _mod_kernel(c_ref, w_ref, b_ref, o_ref):
    c = c_ref[...]
    o_ref[0] = jnp.dot(c * _sigmoid(c), w_ref[0], preferred_element_type=F32) + b_ref[0]


def _modulation(c_all, w_ada, b_ada):
    depth = w_ada.shape[0]
    nb = c_all.shape[0]
    return pl.pallas_call(
        _mod_kernel,
        grid=(depth, 3),
        in_specs=[
            pl.BlockSpec((nb, D_MODEL), lambda l, j: (0, 0)),
            pl.BlockSpec((1, D_MODEL, D_MODEL), lambda l, j: (l, 0, j)),
            pl.BlockSpec((1, 1, D_MODEL), lambda l, j: (l, 0, j)),
        ],
        out_specs=pl.BlockSpec((1, nb, D_MODEL), lambda l, j: (l, 0, j)),
        out_shape=jax.ShapeDtypeStruct((depth, nb, 3 * D_MODEL), F32),
        compiler_params=_cparams(("arbitrary", "arbitrary")),
        name="adaln_mod",
    )(c_all, w_ada, b_ada.reshape(depth, 1, 3 * D_MODEL))


def _first_map_lanes(shape):
    return (lax.broadcasted_iota(jnp.int32, shape, 1) & 32) == 0


def _first_head_lanes(shape):
    return lax.broadcasted_iota(jnp.int32, shape, 1) < 64


def _rms_halves(xb, first):
    sq = xb * xb
    s_a = jnp.sum(jnp.where(first, sq, 0.0), axis=-1, keepdims=True)
    s_b = jnp.sum(jnp.where(first, 0.0, sq), axis=-1, keepdims=True)
    return lax.rsqrt(jnp.where(first, s_a, s_b) * (1.0 / 64.0) + EPS)


def _rms128(xb):
    return lax.rsqrt(jnp.mean(xb * xb, axis=-1, keepdims=True) + EPS)


def _rope(xb, cos, sin_signed):
    return xb * cos + pltpu.roll(xb, 64, 1) * sin_signed


def _tile_epilogue(name, xb, gain, tabs):
    ca_ref, sa_ref, cc_ref, sc_ref = tabs
    if name == "AQ":
        rms = _rms_halves(xb, _first_map_lanes(xb.shape))
        return _rope(xb * (rms * gain), ca_ref[...], sa_ref[...]) * 0.125
    if name == "AK":
        rms = _rms_halves(xb, _first_map_lanes(xb.shape))
        return _rope(xb * (rms * gain), ca_ref[...], sa_ref[...])
    if name == "BQ":
        return xb * gain * (_rms_halves(xb, _first_head_lanes(xb.shape)) * 0.125)
    if name == "BK":
        return xb * gain * _rms_halves(xb, _first_head_lanes(xb.shape))
    if name[:2] in ("CQ", "CK"):
        return _rope(xb * gain, cc_ref[...], sc_ref[...]) * _rms128(xb)
    if name[0] == "Z":
        return xb * _sigmoid(xb)
    if name[0] == "G":
        return _sigmoid(xb)
    return xb


def _proj_kernel(x_ref, mod_ref, lng_ref, w_ref, gain_ref, ca_ref, sa_ref, cc_ref, sc_ref,
                 o_ref, h_ref, acc_ref):
    grp = pl.program_id(1)

    @pl.when(grp == 0)
    def _():
        x = x_ref[...]
        ms = jnp.mean(x * x, axis=-1, keepdims=True)
        y = x * lax.rsqrt(ms + EPS) * lng_ref[...]
        mod = mod_ref[0]
        shift = mod[:, :D_MODEL]
        scale = mod[:, D_MODEL:2 * D_MODEL]
        h_ref[...] = (y * (1.0 + scale) + shift).astype(BF16)

    tabs = (ca_ref, sa_ref, cc_ref, sc_ref)
    for gi, names in enumerate(PROJ_GROUPS):
        @pl.when(grp == gi)
        def _(names=names):
            gain = gain_ref[0]
            for t, name in enumerate(names):
                acc = acc_ref.at[t % 2]
                acc[...] = jnp.dot(h_ref[...], w_ref[:, t * PROJ_TILE:(t + 1) * PROJ_TILE],
                                   preferred_element_type=F32)
                for cb in range(PROJ_TILE // LANES):
                    sl = slice(cb * LANES, (cb + 1) * LANES)
                    out = slice(t * PROJ_TILE + cb * LANES, t * PROJ_TILE + (cb + 1) * LANES)
                    o_ref[:, out] = _tile_epilogue(name, acc[:, sl], gain[:, out], tabs).astype(BF16)


def _projection(x2, mod3, ln_g, w_in_bf, gains, tabs, seq, tm):
    ntok = x2.shape[0]
    tps = seq // tm
    ca, sa, cc, sc = tabs
    tab_spec = pl.BlockSpec((tm, LANES), lambda i, j: (i % tps, 0))
    return pl.pallas_call(
        _proj_kernel,
        grid=(ntok // tm, len(PROJ_GROUPS)),
        in_specs=[
            pl.BlockSpec((tm, D_MODEL), lambda i, j: (i, 0)),
            pl.BlockSpec((1, 1, 3 * D_MODEL), lambda i, j: (i // tps, 0, 0)),
            pl.BlockSpec((1, D_MODEL), lambda i, j: (0, 0)),
            pl.BlockSpec((D_MODEL, GROUP_WIDTH), lambda i, j: (0, j)),
            pl.BlockSpec((1, 1, GROUP_WIDTH), lambda i, j: (j, 0, 0)),
            tab_spec, tab_spec, tab_spec, tab_spec,
        ],
        out_specs=pl.BlockSpec((tm, GROUP_WIDTH), lambda i, j: (i, j)),
        out_shape=jax.ShapeDtypeStruct((ntok, IN_WIDTH), BF16),
        scratch_shapes=[pltpu.VMEM((tm, D_MODEL), BF16), pltpu.VMEM((2, tm, PROJ_TILE), F32)],
        compiler_params=_cparams(("arbitrary", "arbitrary")),
        name="in_proj",
    )(x2, mod3, ln_g, w_in_bf, gains, ca, sa, cc, sc)


def _diff_attn_kernel(q_ref, k_ref, v_ref, lq1_ref, lk1_ref, lq2_ref, lk2_ref, sub_ref, o_ref,
                      *, lam_init, chunk):
    k = k_ref[...]
    v = v_ref[...]
    lam = (jnp.exp(jnp.sum(lq1_ref[...] * lk1_ref[...], axis=-1, keepdims=True))
           - jnp.exp(jnp.sum(lq2_ref[...] * lk2_ref[...], axis=-1, keepdims=True)) + lam_init)
    dn = (((1,), (1,)), ((), ()))
    for c in range(q_ref.shape[0] // chunk):
        rows = slice(c * chunk, (c + 1) * chunk)
        q = q_ref[rows, :]
        lo = _first_map_lanes(q.shape)
        zero = jnp.zeros_like(q)
        s0 = lax.dot_general(jnp.where(lo, q, zero), k, dn, preferred_element_type=F32)
        s1 = lax.dot_general(jnp.where(lo, zero, q), k, dn, preferred_element_type=F32)
        e0 = jnp.exp(s0 - jnp.max(s0, axis=-1, keepdims=True))
        e1 = jnp.exp(s1 - jnp.max(s1, axis=-1, keepdims=True))
        l0 = jnp.sum(e0, axis=-1, keepdims=True)
        l1 = jnp.sum(e1, axis=-1, keepdims=True)
        w = e0 * (1.0 / l0) - e1 * (lam / l1)
        o = jnp.dot(w.astype(BF16), v, preferred_element_type=F32)
        ms = jnp.mean(o * o, axis=-1, keepdims=True)
        o_ref[rows, :] = (o * lax.rsqrt(ms + EPS) * sub_ref[...] * (1.0 - lam_init)).astype(BF16)


def _diff_attention(proj3, lam_params, subln, lam_init, tq, chunk):
    nb, seq, _ = proj3.shape
    vec = pl.BlockSpec((1, A_QK_DIM), lambda b, h, i: (0, 0))
    return pl.pallas_call(
        functools.partial(_diff_attn_kernel, lam_init=lam_init, chunk=chunk),
        grid=(nb, A_HEADS, seq // tq),
        in_specs=[
            pl.BlockSpec((None, tq, LANES), lambda b, h, i: (b, i, _col_block("AQ") + h)),
            pl.BlockSpec((None, seq, LANES), lambda b, h, i: (b, 0, _col_block("AK") + h)),
            pl.BlockSpec((None, seq, LANES), lambda b, h, i: (b, 0, _col_block("AV") + h)),
            vec, vec, vec, vec,
            pl.BlockSpec((1, LANES), lambda b, h, i: (0, 0)),
        ],
        out_specs=pl.BlockSpec((None, tq, LANES), lambda b, h, i: (b, i, h)),
        out_shape=jax.ShapeDtypeStruct((nb, seq, BRANCH_WIDTH), BF16),
        compiler_params=_cparams(("arbitrary", "arbitrary", "arbitrary")),
        name="diff_attn",
    )(proj3, proj3, proj3, *lam_params, subln)


def _na_bias_tables(rpb, rows):
    qc = jnp.arange(GRID_W)
    kc = jnp.arange(GRID_W)
    cs = jnp.clip(qc - NA_KW // 2, 0, GRID_W - NA_KW)
    col_ok = (kc[None, :] >= cs[:, None]) & (kc[None, :] < cs[:, None] + NA_KW)
    col_off = jnp.clip(kc[None, :] - qc[:, None], -(NA_KW - 1), NA_KW - 1) + NA_KW - 1
    cols = jnp.where(col_ok, rpb.astype(F32)[:, :, col_off], NEG)
    kh = min(NA_KH, rows)
    tables = []
    for r0, ws in ((0, 0), (NA_QROWS, NA_QROWS - kh // 2), (rows - NA_QROWS, rows - NA_KROWS)):
        r = r0 + jnp.arange(NA_QROWS)
        kr = ws + jnp.arange(NA_KROWS)
        rs = jnp.clip(r - kh // 2, 0, rows - kh)
        row_ok = (kr[None, :] >= rs[:, None]) & (kr[None, :] < rs[:, None] + kh)
        row_off = jnp.clip(kr[None, :] - r[:, None] + NA_KH - 1, 0, 2 * NA_KH - 2)
        t = jnp.where(row_ok[None, :, :, None, None], cols[:, row_off], NEG)
        tables.append(t.transpose(0, 1, 3, 2, 4).reshape(
            B_HEADS, NA_QROWS * GRID_W, NA_KROWS * GRID_W))
    return jnp.stack(tables)


def _na_kernel(q_ref, k_ref, v_ref, bias_ref, o_ref, *, rows):
    i = pl.program_id(2)
    last = pl.num_programs(2) - 1
    kind = jnp.where(i == 0, 0, jnp.where(i == last, 2, 1))
    ws = jnp.clip(i * NA_QROWS - NA_KH // 2, 0, rows - NA_KROWS)
    start = pl.multiple_of(ws * GRID_W, GRID_W)
    nkeys = NA_KROWS * GRID_W
    kw = k_ref[pl.ds(start, nkeys), :]
    vw = v_ref[pl.ds(start, nkeys), :]
    dn = (((1,), (1,)), ((), ()))
    for c in range(q_ref.shape[0] // NA_CHUNK):
        rows = slice(c * NA_CHUNK, (c + 1) * NA_CHUNK)
        q = q_ref[rows, :]
        lo = lax.broadcasted_iota(jnp.int32, q.shape, 1) < 64
        zero = jnp.zeros_like(q)
        outs = []
        for e in range(2):
            qe = jnp.where(lo, q, zero) if e == 0 else jnp.where(lo, zero, q)
            s = lax.dot_general(qe, kw, dn, preferred_element_type=F32) + bias_ref[kind, e, rows, :]
            ex = jnp.exp(s - jnp.max(s, axis=-1, keepdims=True))
            p = ex * (1.0 / jnp.sum(ex, axis=-1, keepdims=True))
            outs.append(jnp.dot(p.astype(BF16), vw, preferred_element_type=F32))
        o_ref[rows, :] = jnp.where(lo, outs[0], outs[1]).astype(BF16)


def _neighborhood_attention(proj3, bias):
    nb, seq, _ = proj3.shape
    rows = seq // GRID_W
    nq = NA_QROWS * GRID_W
    nk = NA_KROWS * GRID_W
    return pl.pallas_call(
        functools.partial(_na_kernel, rows=rows),
        grid=(B_HEADS // 2, nb, rows // NA_QROWS),
        in_specs=[
            pl.BlockSpec((None, nq, LANES), lambda p, b, i: (b, i, _col_block("BQ") + p)),
            pl.BlockSpec((None, seq, LANES), lambda p, b, i: (b, 0, _col_block("BK") + p)),
            pl.BlockSpec((None, seq, LANES), lambda p, b, i: (b, 0, _col_block("BV") + p)),
            pl.BlockSpec((3, 2, nq, nk), lambda p, b, i: (0, p, 0, 0)),
        ],
        out_specs=pl.BlockSpec((None, nq, LANES), lambda p, b, i: (b, i, p)),
        out_shape=jax.ShapeDtypeStruct((nb, seq, BRANCH_WIDTH), BF16),
        compiler_params=_cparams(("arbitrary", "arbitrary", "arbitrary")),
        name="nbr_attn",
    )(proj3, proj3, proj3, bias)


def _dil_kernel(q0_ref, q1_ref, q2_ref, k0_ref, k1_ref, k2_ref, v0_ref, v1_ref, v2_ref, o_ref,
                stage_ref, qd_ref, kd_ref, vd_ref, og_ref, lg_ref, on_ref, ln_ref, oacc_ref,
                lacc_ref, band_ref, *, seq):
    scale = C_HEAD_DIM ** -0.5
    dn = (((1,), (1,)), ((), ()))
    groups = ((q0_ref, k0_ref, v0_ref), (q1_ref, k1_ref, v1_ref), (q2_ref, k2_ref, v2_ref))

    full_win = C_QBLK + 2 * C_HALF_WINDOW
    qi = lax.broadcasted_iota(jnp.int32, (C_QBLK, full_win), 0)
    kj = lax.broadcasted_iota(jnp.int32, (C_QBLK, full_win), 1)
    for lead in range(3):
        band_ref[lead] = jnp.where(
            jnp.abs(kj - qi - lead * C_HALF_WINDOW) <= C_HALF_WINDOW, 0.0, NEG)

    for g, dil in enumerate(C_DILATIONS):
        sub_len = seq // dil
        win = min(sub_len, C_QBLK + 2 * C_HALF_WINDOW)
        bps = sub_len // C_QBLK
        qs, ks, vs = groups[g]
        if dil > 1:
            for src, dst in ((qs, qd_ref), (ks, kd_ref), (vs, vd_ref)):
                stage_ref[...] = src[...].astype(F32)
                for rho in range(dil):
                    dst[rho * sub_len:(rho + 1) * sub_len, :] = stage_ref[
                        pl.ds(rho, sub_len, stride=dil), :].astype(BF16)
            qs, ks, vs = qd_ref, kd_ref, vd_ref
        o_dst, l_dst = (oacc_ref, lacc_ref) if dil == 1 else (og_ref, lg_ref)

        def attend(n, qs=qs, ks=ks, vs=vs, sub_len=sub_len, win=win, bps=bps, o_dst=o_dst,
                   l_dst=l_dst):
            a = (n % bps) * C_QBLK
            base = (n // bps) * sub_len
            ws = jnp.clip(a - C_HALF_WINDOW, 0, sub_len - win)
            rows = pl.ds(pl.multiple_of(n * C_QBLK, C_QBLK), C_QBLK)
            q = qs[rows, :]
            kw = ks[pl.ds(pl.multiple_of(base + ws, C_HALF_WINDOW), win), :]
            vw = vs[pl.ds(pl.multiple_of(base + ws, C_HALF_WINDOW), win), :]
            band = band_ref[(a - ws) // C_HALF_WINDOW, :, :win]
            s = lax.dot_general(q, kw, dn, preferred_element_type=F32) * scale + band
            m = jnp.max(s, axis=-1, keepdims=True)
            e = jnp.exp(s - m)
            l = jnp.sum(e, axis=-1, keepdims=True)
            acc = jnp.dot(e.astype(BF16), vw, preferred_element_type=F32)
            o_dst[rows, :] = acc * (1.0 / l)
            l_dst[rows, :] = jnp.broadcast_to(m + jnp.log(l), acc.shape)

        def blocks(it, carry, attend=attend):
            for u in range(C_UNROLL):
                attend(it * C_UNROLL + u)
            return carry

        lax.fori_loop(0, seq // (C_QBLK * C_UNROLL), blocks, 0)

        if dil > 1:
            top = C_DILATIONS[-1]
            run = seq // top
            pairs = ((og_ref, on_ref), (lg_ref, ln_ref))
            if dil < top:
                for src, dst in pairs:
                    for rho in range(dil):
                        for c in range(top // dil):
                            r = c * dil + rho
                            dst[r * run:(r + 1) * run, :] = src[
                                pl.ds(rho * sub_len + c, run, stride=top // dil), :]
                pairs = ((on_ref, og_ref), (ln_ref, lg_ref))

            def to_token_order(j, carry, pairs=pairs, top=top, run=run):
                for src, dst in pairs:
                    dst[pl.ds(pl.multiple_of(j * top, top), top), :] = src[
                        pl.ds(j, top, stride=run), :]
                return carry

            lax.fori_loop(0, run, to_token_order, 0, unroll=8)
            o_cur_ref, l_cur_ref = pairs[0][1], pairs[1][1]
            l_old = lacc_ref[...]
            l_cur = l_cur_ref[...]
            l_max = jnp.maximum(l_old, l_cur)
            w_old = jnp.exp(l_old - l_max)
            w_cur = jnp.exp(l_cur - l_max)
            inv = 1.0 / (w_old + w_cur)
            oacc_ref[...] = (w_old * oacc_ref[...] + w_cur * o_cur_ref[...]) * inv
            lacc_ref[...] = l_max + jnp.log(w_old + w_cur)

    o_ref[...] = oacc_ref[...].astype(BF16)


def _dilated_mixture(proj3):
    nb, seq, _ = proj3.shape

    def spec(kind, g):
        cb0 = _col_block(f"{kind}{g}")
        return pl.BlockSpec((None, seq, LANES), lambda b, h, cb0=cb0: (b, 0, cb0 + h))

    in_specs = [spec(kind, g) for kind in ("CQ", "CK", "CV") for g in range(len(C_DILATIONS))]
    return pl.pallas_call(
        functools.partial(_dil_kernel, seq=seq),
        grid=(nb, C_HEADS_PER_GROUP),
        in_specs=in_specs,
        out_specs=pl.BlockSpec((None, seq, LANES), lambda b, h: (b, 0, h)),
        out_shape=jax.ShapeDtypeStruct((nb, seq, BRANCH_WIDTH), BF16),
        scratch_shapes=[
            pltpu.VMEM((seq, LANES), F32),
            pltpu.VMEM((seq, LANES), BF16),
            pltpu.VMEM((seq, LANES), BF16),
            pltpu.VMEM((seq, LANES), BF16),
            pltpu.VMEM((seq, LANES), F32),
            pltpu.VMEM((seq, LANES), F32),
            pltpu.VMEM((seq, LANES), F32),
            pltpu.VMEM((seq, LANES), F32),
            pltpu.VMEM((seq, LANES), F32),
            pltpu.VMEM((seq, LANES), F32),
            pltpu.VMEM((3, C_QBLK, C_QBLK + 2 * C_HALF_WINDOW), F32),
        ],
        compiler_params=_cparams(("arbitrary", "arbitrary")),
        name="dil_attn",
    )(*([proj3] * 9))


def _tail_kernel(ya_ref, yb_ref, yc_ref, z0_ref, z1_ref, z2_ref, g0a_ref, g0b_ref, g1a_ref,
                 g1b_ref, g2a_ref, g2b_ref, x_ref, gate_ref, wbr_ref, wout_ref, o_ref):
    halves = [None, None]
    branches = ((ya_ref, z0_ref, (g0a_ref, g0b_ref)), (yb_ref, z1_ref, (g1a_ref, g1b_ref)),
                (yc_ref, z2_ref, (g2a_ref, g2b_ref)))
    for i, (y_ref, z_ref, g_refs) in enumerate(branches):
        yz = (y_ref[...].astype(F32) * z_ref[...].astype(F32)).astype(BF16)
        for half, g_ref in enumerate(g_refs):
            cols = slice(half * PROJ_TILE, (half + 1) * PROJ_TILE)
            u = g_ref[...].astype(F32) * jnp.dot(yz, wbr_ref[i, :, cols], preferred_element_type=F32)
            halves[half] = u if halves[half] is None else halves[half] + u
    merged = jnp.concatenate(halves, axis=1).astype(BF16)
    out = jnp.dot(merged, wout_ref[...], preferred_element_type=F32)
    o_ref[...] = x_ref[...] + gate_ref[0] * out


def _tail(ya, yb, yc, proj2, x2, mod3, w_br_bf, w_out_bf, seq, tm):
    ntok = x2.shape[0]
    tps = seq // tm
    y_spec = pl.BlockSpec((tm, BRANCH_WIDTH), lambda i: (i, 0))

    def tile_spec(name):
        return pl.BlockSpec((tm, PROJ_TILE), lambda i, s=SLOT[name]: (i, s))

    z_specs = [tile_spec(f"Z{t}") for t in range(3)]
    g_specs = [tile_spec(f"G{t}{half}") for t in range(3) for half in "ab"]
    return pl.pallas_call(
        _tail_kernel,
        grid=(ntok // tm,),
        in_specs=[y_spec, y_spec, y_spec, *z_specs, *g_specs,
                  pl.BlockSpec((tm, D_MODEL), lambda i: (i, 0)),
                  pl.BlockSpec((1, 1, D_MODEL), lambda i: (i // tps, 0, 2)),
                  pl.BlockSpec((3, BRANCH_WIDTH, D_MODEL), lambda i: (0, 0, 0)),
                  pl.BlockSpec((D_MODEL, D_MODEL), lambda i: (0, 0))],
        out_specs=pl.BlockSpec((tm, D_MODEL), lambda i: (i, 0)),
        out_shape=jax.ShapeDtypeStruct((ntok, D_MODEL), F32),
        compiler_params=_cparams(("arbitrary",)),
        name="tail",
    )(ya, yb, yc, *([proj2] * 9), x2, mod3, w_br_bf, w_out_bf)


def _rope_tables(seq):
    def base(d):
        inv = ROPE_THETA ** (-jnp.arange(0, d, 2, dtype=F32) / d)
        ang = jnp.arange(seq, dtype=F32)[:, None] * inv[None, :]
        return jnp.cos(ang), jnp.sin(ang)

    cos_a, sin_a = base(A_QK_DIM)
    ca = jnp.tile(cos_a, (1, 4))
    sa = jnp.concatenate([-sin_a, -sin_a, sin_a, sin_a], axis=-1)
    cos_c, sin_c = base(C_HEAD_DIM)
    cc = jnp.tile(cos_c, (1, 2))
    sc = jnp.concatenate([-sin_c, sin_c], axis=-1)
    return ca, sa, cc, sc


def _stored_column_order():
    lane = np.arange(LANES)
    seg, i = lane // 32, lane % 32
    a_lane = (seg % 2) * A_QK_DIM + (seg // 2) * 32 + i
    order = []
    for grp in PROJ_GROUPS:
        for name in grp:
            cols = TILE_NAMES.index(name) * PROJ_TILE + np.arange(PROJ_TILE)
            if name in ("AQ", "AK"):
                cols = cols.reshape(-1, LANES)[:, a_lane].reshape(-1)
            order.append(cols)
    return np.concatenate(order)


def _proj_gains(qn_a, kn_a, qn_b, kn_b, qn_c, kn_c, order):
    per_tile = {"AQ": jnp.tile(qn_a, PROJ_TILE // A_QK_DIM), "AK": jnp.tile(kn_a, PROJ_TILE // A_QK_DIM),
                "BQ": jnp.tile(qn_b, PROJ_TILE // B_HEAD_DIM), "BK": jnp.tile(kn_b, PROJ_TILE // B_HEAD_DIM)}
    for g in range(len(C_DILATIONS)):
        per_tile[f"CQ{g}"] = jnp.tile(qn_c, PROJ_TILE // C_HEAD_DIM)
        per_tile[f"CK{g}"] = jnp.tile(kn_c, PROJ_TILE // C_HEAD_DIM)
    ones = jnp.ones((PROJ_TILE,), F32)
    full = jnp.concatenate([per_tile.get(name, ones).astype(F32) for name in TILE_NAMES])
    return full[order].reshape(len(PROJ_GROUPS), 1, GROUP_WIDTH)


def _proj_token_tile(seq):
    return min(1024, seq)


def _tail_token_tile(seq):
    return min(512, seq)


def _encoder_layer(x, mod3, layer_idx, tabs, ln_g, w_in_bf, gains, lam_params, subln, na_bias,
                   w_br_bf, w_out_bf):
    nb, seq, _ = x.shape
    x2 = x.reshape(nb * seq, D_MODEL)
    proj2 = _projection(x2, mod3, ln_g, w_in_bf, gains, tabs, seq, _proj_token_tile(seq))
    proj3 = proj2.reshape(nb, seq, IN_WIDTH)
    lam_init = 0.8 - 0.6 * math.exp(-0.3 * layer_idx)
    a_chunks = 4 if seq <= 2048 else 1
    ya = _diff_attention(proj3, lam_params, subln, lam_init, tq=256 * a_chunks, chunk=256)
    yb = _neighborhood_attention(proj3, na_bias)
    yc = _dilated_mixture(proj3)
    flat = lambda y: y.reshape(nb * seq, BRANCH_WIDTH)
    y2 = _tail(flat(ya), flat(yb), flat(yc), proj2, x2, mod3, w_br_bf, w_out_bf, seq,
               _tail_token_tile(seq))
    return y2.reshape(nb, seq, D_MODEL)


def kernel(x_prompt, x_sample, c_prompt, c_sample, ln_g, w_ada, b_ada, w_in, qn_a, kn_a, lam_q1, lam_k1, lam_q2, lam_k2, subln_a, qn_b, kn_b, rpb_b, qn_c, kn_c, w_br, w_out):
    depth = w_in.shape[0]
    n_prompt = c_prompt.shape[0]
    mod_all = _modulation(jnp.concatenate([c_prompt, c_sample], axis=0), w_ada, b_ada)
    order = _stored_column_order()
    w_in_bf = w_in.astype(BF16)[:, :, order]
    w_br_bf = w_br.astype(BF16)
    w_out_bf = w_out.astype(BF16)

    def run(x, mod):
        nb, seq, _ = x.shape
        tabs = _rope_tables(seq)
        for l in range(depth):
            gains = _proj_gains(qn_a[l], kn_a[l], qn_b[l], kn_b[l], qn_c[l], kn_c[l], order)
            lam_params = tuple(p[l].reshape(1, A_QK_DIM) for p in (lam_q1, lam_k1, lam_q2, lam_k2))
            na_bias = _na_bias_tables(rpb_b[l], seq // GRID_W)
            x = _encoder_layer(x, mod[l].reshape(nb, 1, 3 * D_MODEL), l, tabs,
                               ln_g[l].reshape(1, D_MODEL), w_in_bf[l], gains, lam_params,
                               subln_a[l].reshape(1, LANES), na_bias, w_br_bf[l], w_out_bf[l])
        return x

    y_prompt = run(x_prompt, mod_all[:, :n_prompt])
    y_sample = run(x_sample, mod_all[:, n_prompt:])
    return (y_prompt, y_sample)
```

```python
import functools
import math

import jax
import jax.numpy as jnp
from jax import lax
from jax.experimental import pallas as pl
from jax.experimental.pallas import tpu as pltpu

F32 = jnp.float32
BF16 = jnp.bfloat16

D_MODEL = 1024
GRID_W = 64
BRANCH_WIDTH = 512
ROPE_THETA = 10000.0
EPS = 1e-6
NEG = -1e30
A_QK_DIM = 64
A_HEADS = 4
B_HEAD_DIM = 64
B_HEADS = 8
NA_KH = 8
NA_KW = 16
C_DILATIONS = (1, 4, 16)
C_HALF_WINDOW = 64
C_HEAD_DIM = 128
C_HEADS_PER_GROUP = 4
IN_WIDTH = 12288
LANES = 128

PROJ_TILE = 512
TILE_NAMES = ("AQ", "AK", "AV", "BQ", "BK", "BV", "CQ0", "CQ1", "CQ2", "CK0", "CK1", "CK2",
              "CV0", "CV1", "CV2", "Z0", "Z1", "Z2", "G0a", "G0b", "G1a", "G1b", "G2a", "G2b")
PROJ_GROUPS = (("AQ", "Z0", "BQ", "AV"), ("AK", "Z1", "BK", "BV"),
               ("CQ0", "G0a", "CK0", "CV0"), ("CQ1", "G0b", "CK1", "CV1"),
               ("CQ2", "G1a", "CK2", "CV2"), ("Z2", "G1b", "G2a", "G2b"))
GROUP_TILES = len(PROJ_GROUPS[0])
GROUP_WIDTH = GROUP_TILES * PROJ_TILE
SLOT = {name: i for i, name in enumerate(n for grp in PROJ_GROUPS for n in grp)}


def _col_block(name, sub=0):
    return SLOT[name] * (PROJ_TILE // LANES) + sub

VMEM_LIMIT = 56 * 1024 * 1024

NA_QROWS = 8
NA_KROWS = 16
NA_CHUNK = 512
C_QBLK = 128
C_UNROLL = 8


def _cparams(sem):
    return pltpu.CompilerParams(dimension_semantics=sem, vmem_limit_bytes=VMEM_LIMIT)


def _sigmoid(x):
    return 1.0 / (1.0 + jnp.exp(-x))


def _mod_kernel(c_ref, w_ref, b_ref, o_ref):
    c = c_ref[...]
    o_ref[0] = jnp.dot(c * _sigmoid(c), w_ref[0], preferred_element_type=F32) + b_ref[0]


def _modulation(c_all, w_ada, b_ada):
    depth = w_ada.shape[0]
    nb = c_all.shape[0]
    return pl.pallas_call(
        _mod_kernel,
        grid=(depth, 3),
        in_specs=[
            pl.BlockSpec((nb, D_MODEL), lambda l, j: (0, 0)),
            pl.BlockSpec((1, D_MODEL, D_MODEL), lambda l, j: (l, 0, j)),
            pl.BlockSpec((1, 1, D_MODEL), lambda l, j: (l, 0, j)),
        ],
        out_specs=pl.BlockSpec((1, nb, D_MODEL), lambda l, j: (l, 0, j)),
        out_shape=jax.ShapeDtypeStruct((depth, nb, 3 * D_MODEL), F32),
        compiler_params=_cparams(("arbitrary", "arbitrary")),
        name="adaln_mod",
    )(c_all, w_ada, b_ada.reshape(depth, 1, 3 * D_MODEL))


def _first_map_lanes(shape):
    return (lax.broadcasted_iota(jnp.int32, shape, 1) & 32) == 0


def _first_head_lanes(shape):
    return lax.broadcasted_iota(jnp.int32, shape, 1) < 64


def _rms_halves(xb, first):
    sq = xb * xb
    s_a = jnp.sum(jnp.where(first, sq, 0.0), axis=-1, keepdims=True)
    s_b = jnp.sum(jnp.where(first, 0.0, sq), axis=-1, keepdims=True)
    return lax.rsqrt(jnp.where(first, s_a, s_b) * (1.0 / 64.0) + EPS)


def _rms128(xb):
    return lax.rsqrt(jnp.mean(xb * xb, axis=-1, keepdims=True) + EPS)


def _rope(xb, cos, sin_signed):
    return xb * cos + pltpu.roll(xb, 64, 1) * sin_signed


def _tile_epilogue(name, xb, gain, tabs):
    ca_ref, sa_ref, cc_ref, sc_ref = tabs
    if name == "AQ":
        rms = _rms_halves(xb, _first_map_lanes(xb.shape))
        return _rope(xb * (rms * gain), ca_ref[...], sa_ref[...]) * 0.125
    if name == "AK":
        rms = _rms_halves(xb, _first_map_lanes(xb.shape))
        return _rope(xb * (rms * gain), ca_ref[...], sa_ref[...])
    if name == "BQ":
        return xb * gain * (_rms_halves(xb, _first_head_lanes(xb.shape)) * 0.125)
    if name == "BK":
        return xb * gain * _rms_halves(xb, _first_head_lanes(xb.shape))
    if name[:2] in ("CQ", "CK"):
        return _rope(xb * gain, cc_ref[...], sc_ref[...]) * _rms128(xb)
    if name[0] == "Z":
        return xb * _sigmoid(xb)
    if name[0] == "G":
        return _sigmoid(xb)
    return xb


def _proj_kernel(x_ref, mod_ref, lng_ref, w_ref, gain_ref, ca_ref, sa_ref, cc_ref, sc_ref,
                 o_ref, h_ref, acc_ref):
    grp = pl.program_id(1)

    @pl.when(grp == 0)
    def _():
        x = x_ref[...]
        ms = jnp.mean(x * x, axis=-1, keepdims=True)
        y = x * lax.rsqrt(ms + EPS) * lng_ref[...]
        mod = mod_ref[0]
        shift = mod[:, :D_MODEL]
        scale = mod[:, D_MODEL:2 * D_MODEL]
        h_ref[...] = (y * (1.0 + scale) + shift).astype(BF16)

    tabs = (ca_ref, sa_ref, cc_ref, sc_ref)
    for gi, names in enumerate(PROJ_GROUPS):
        @pl.when(grp == gi)
        def _(names=names):
            gain = gain_ref[0]
            for t, name in enumerate(names):
                acc = acc_ref.at[t % 2]
                acc[...] = jnp.dot(h_ref[...], w_ref[:, t * PROJ_TILE:(t + 1) * PROJ_TILE],
                                   preferred_element_type=F32)
                for cb in range(PROJ_TILE // LANES):
                    sl = slice(cb * LANES, (cb + 1) * LANES)
                    out = slice(t * PROJ_TILE + cb * LANES, t * PROJ_TILE + (cb + 1) * LANES)
                    o_ref[:, out] = _tile_epilogue(name, acc[:, sl], gain[:, out], tabs).astype(BF16)


def _projection(x2, mod3, ln_g, w_in_bf, gains, tabs, seq, tm):
    ntok = x2.shape[0]
    tps = seq // tm
    ca, sa, cc, sc = tabs
    tab_spec = pl.BlockSpec((tm, LANES), lambda i, j: (i % tps, 0))
    return pl.pallas_call(
        _proj_kernel,
        grid=(ntok // tm, len(PROJ_GROUPS)),
        in_specs=[
            pl.BlockSpec((tm, D_MODEL), lambda i, j: (i, 0)),
            pl.BlockSpec((1, 1, 3 * D_MODEL), lambda i, j: (i // tps, 0, 0)),
            pl.BlockSpec((1, D_MODEL), lambda i, j: (0, 0)),
            pl.BlockSpec((D_MODEL, GROUP_WIDTH), lambda i, j: (0, j)),
            pl.BlockSpec((1, 1, GROUP_WIDTH), lambda i, j: (j, 0, 0)),
            tab_spec, tab_spec, tab_spec, tab_spec,
        ],
        out_specs=pl.BlockSpec((tm, GROUP_WIDTH), lambda i, j: (i, j)),
        out_shape=jax.ShapeDtypeStruct((ntok, IN_WIDTH), BF16),
        scratch_shapes=[pltpu.VMEM((tm, D_MODEL), BF16), pltpu.VMEM((2, tm, PROJ_TILE), F32)],
        compiler_params=_cparams(("arbitrary", "arbitrary")),
        name="in_proj",
    )(x2, mod3, ln_g, w_in_bf, gains, ca, sa, cc, sc)


def _diff_attn_kernel(q_ref, k_ref, v_ref, lq1_ref, lk1_ref, lq2_ref, lk2_ref, sub_ref, o_ref,
                      *, lam_init, chunk):
    k = k_ref[...]
    v = v_ref[...]
    lam = (jnp.exp(jnp.sum(lq1_ref[...] * lk1_ref[...], axis=-1, keepdims=True))
           - jnp.exp(jnp.sum(lq2_ref[...] * lk2_ref[...], axis=-1, keepdims=True)) + lam_init)
    dn = (((1,), (1,)), ((), ()))
    for c in range(q_ref.shape[0] // chunk):
        rows = slice(c * chunk, (c + 1) * chunk)
        q = q_ref[rows, :]
        lo = _first_map_lanes(q.shape)
        zero = jnp.zeros_like(q)
        s0 = lax.dot_general(jnp.where(lo, q, zero), k, dn, preferred_element_type=F32)
        s1 = lax.dot_general(jnp.where(lo, zero, q), k, dn, preferred_element_type=F32)
        e0 = jnp.exp(s0 - jnp.max(s0, axis=-1, keepdims=True))
        e1 = jnp.exp(s1 - jnp.max(s1, axis=-1, keepdims=True))
        l0 = jnp.sum(e0, axis=-1, keepdims=True)
        l1 = jnp.sum(e1, axis=-1, keepdims=True)
        w = e0 * (1.0 / l0) - e1 * (lam / l1)
        o = jnp.dot(w.astype(BF16), v, preferred_element_type=F32)
        ms = jnp.mean(o * o, axis=-1, keepdims=True)
        o_ref[rows, :] = (o * lax.rsqrt(ms + EPS) * sub_ref[...] * (1.0 - lam_init)).astype(BF16)


def _diff_attention(proj3, lam_params, subln, lam_init, tq, chunk):
    nb, seq, _ = proj3.shape
    vec = pl.BlockSpec((1, A_QK_DIM), lambda b, h, i: (0, 0))
    return pl.pallas_call(
        functools.partial(_diff_attn_kernel, lam_init=lam_init, chunk=chunk),
        grid=(nb, A_HEADS, seq // tq),
        in_specs=[
            pl.BlockSpec((None, tq, LANES), lambda b, h, i: (b, i, _col_block("AQ") + h)),
            pl.BlockSpec((None, seq, LANES), lambda b, h, i: (b, 0, _col_block("AK") + h)),
            pl.BlockSpec((None, seq, LANES), lambda b, h, i: (b, 0, _col_block("AV") + h)),
            vec, vec, vec, vec,
            pl.BlockSpec((1, LANES), lambda b, h, i: (0, 0)),
        ],
        out_specs=pl.BlockSpec((None, tq, LANES), lambda b, h, i: (b, i, h)),
        out_shape=jax.ShapeDtypeStruct((nb, seq, BRANCH_WIDTH), BF16),
        compiler_params=_cparams(("arbitrary", "arbitrary", "arbitrary")),
        name="diff_attn",
    )(proj3, proj3, proj3, *lam_params, subln)


def _na_bias_tables(rpb):
    rows = NA_QROWS + NA_KROWS
    qc = jnp.arange(GRID_W)
    kc = jnp.arange(GRID_W)
    cs = jnp.clip(qc - NA_KW // 2, 0, GRID_W - NA_KW)
    col_ok = (kc[None, :] >= cs[:, None]) & (kc[None, :] < cs[:, None] + NA_KW)
    col_off = jnp.clip(kc[None, :] - qc[:, None], -(NA_KW - 1), NA_KW - 1) + NA_KW - 1
    cols = jnp.where(col_ok, rpb.astype(F32)[:, :, col_off], NEG)
    kh = min(NA_KH, rows)
    tables = []
    for r0, ws in ((0, 0), (NA_QROWS, NA_QROWS - kh // 2), (rows - NA_QROWS, rows - NA_KROWS)):
        r = r0 + jnp.arange(NA_QROWS)
        kr = ws + jnp.arange(NA_KROWS)
        rs = jnp.clip(r - kh // 2, 0, rows - kh)
        row_ok = (kr[None, :] >= rs[:, None]) & (kr[None, :] < rs[:, None] + kh)
        row_off = jnp.clip(kr[None, :] - r[:, None] + NA_KH - 1, 0, 2 * NA_KH - 2)
        t = jnp.where(row_ok[None, :, :, None, None], cols[:, row_off], NEG)
        tables.append(t.transpose(0, 1, 3, 2, 4).reshape(
            B_HEADS, NA_QROWS * GRID_W, NA_KROWS * GRID_W))
    return jnp.stack(tables)


def _na_kernel(q_ref, k_ref, v_ref, bias_ref, o_ref, *, rows):
    i = pl.program_id(2)
    last = pl.num_programs(2) - 1
    kind = jnp.where(i == 0, 0, jnp.where(i == last, 2, 1))
    ws = jnp.clip(i * NA_QROWS - NA_KH // 2, 0, rows - NA_KROWS)
    start = pl.multiple_of(ws * GRID_W, GRID_W)
    nkeys = NA_KROWS * GRID_W
    kw = k_ref[pl.ds(start, nkeys), :]
    vw = v_ref[pl.ds(start, nkeys), :]
    dn = (((1,), (1,)), ((), ()))
    for c in range(q_ref.shape[0] // NA_CHUNK):
        rows = slice(c * NA_CHUNK, (c + 1) * NA_CHUNK)
        q = q_ref[rows, :]
        lo = lax.broadcasted_iota(jnp.int32, q.shape, 1) < 64
        zero = jnp.zeros_like(q)
        outs = []
        for e in range(2):
            qe = jnp.where(lo, q, zero) if e == 0 else jnp.where(lo, zero, q)
            s = lax.dot_general(qe, kw, dn, preferred_element_type=F32) + bias_ref[kind, e, rows, :]
            ex = jnp.exp(s - jnp.max(s, axis=-1, keepdims=True))
            p = ex * (1.0 / jnp.sum(ex, axis=-1, keepdims=True))
            outs.append(jnp.dot(p.astype(BF16), vw, preferred_element_type=F32))
        o_ref[rows, :] = jnp.where(lo, outs[0], outs[1]).astype(BF16)


def _neighborhood_attention(proj3, bias):
    nb, seq, _ = proj3.shape
    rows = seq // GRID_W
    nq = NA_QROWS * GRID_W
    nk = NA_KROWS * GRID_W
    return pl.pallas_call(
        functools.partial(_na_kernel, rows=rows),
        grid=(B_HEADS // 2, nb, rows // NA_QROWS),
        in_specs=[
            pl.BlockSpec((None, nq, LANES), lambda p, b, i: (b, i, _col_block("BQ") + p)),
            pl.BlockSpec((None, seq, LANES), lambda p, b, i: (b, 0, _col_block("BK") + p)),
            pl.BlockSpec((None, seq, LANES), lambda p, b, i: (b, 0, _col_block("BV") + p)),
            pl.BlockSpec((3, 2, nq, nk), lambda p, b, i: (0, p, 0, 0)),
        ],
        out_specs=pl.BlockSpec((None, nq, LANES), lambda p, b, i: (b, i, p)),
        out_shape=jax.ShapeDtypeStruct((nb, seq, BRANCH_WIDTH), BF16),
        compiler_params=_cparams(("arbitrary", "arbitrary", "arbitrary")),
        name="nbr_attn",
    )(proj3, proj3, proj3, bias)


def _dil_kernel(q0_ref, q1_ref, q2_ref, k0_ref, k1_ref, k2_ref, v0_ref, v1_ref, v2_ref, o_ref,
                stage_ref, qd_ref, kd_ref, vd_ref, og_ref, lg_ref, on_ref, ln_ref, oacc_ref,
                lacc_ref, band_ref, *, seq):
    scale = C_HEAD_DIM ** -0.5
    dn = (((1,), (1,)), ((), ()))
    groups = ((q0_ref, k0_ref, v0_ref), (q1_ref, k1_ref, v1_ref), (q2_ref, k2_ref, v2_ref))

    full_win = C_QBLK + 2 * C_HALF_WINDOW
    qi = lax.broadcasted_iota(jnp.int32, (C_QBLK, full_win), 0)
    kj = lax.broadcasted_iota(jnp.int32, (C_QBLK, full_win), 1)
    for lead in range(3):
        band_ref[lead] = jnp.where(
            jnp.abs(kj - qi - lead * C_HALF_WINDOW) <= C_HALF_WINDOW, 0.0, NEG)

    for g, dil in enumerate(C_DILATIONS):
        sub_len = seq // dil
        win = min(sub_len, C_QBLK + 2 * C_HALF_WINDOW)
        bps = sub_len // C_QBLK
        qs, ks, vs = groups[g]
        if dil > 1:
            for src, dst in ((qs, qd_ref), (ks, kd_ref), (vs, vd_ref)):
                stage_ref[...] = src[...].astype(F32)
                for rho in range(dil):
                    dst[rho * sub_len:(rho + 1) * sub_len, :] = stage_ref[
                        pl.ds(rho, sub_len, stride=dil), :].astype(BF16)
            qs, ks, vs = qd_ref, kd_ref, vd_ref
        o_dst, l_dst = (oacc_ref, lacc_ref) if dil == 1 else (og_ref, lg_ref)

        def attend(n, qs=qs, ks=ks, vs=vs, sub_len=sub_len, win=win, bps=bps, o_dst=o_dst,
                   l_dst=l_dst):
            a = (n % bps) * C_QBLK
            base = (n // bps) * sub_len
            ws = jnp.clip(a - C_HALF_WINDOW, 0, sub_len - win)
            rows = pl.ds(pl.multiple_of(n * C_QBLK, C_QBLK), C_QBLK)
            q = qs[rows, :]
            kw = ks[pl.ds(pl.multiple_of(base + ws, C_HALF_WINDOW), win), :]
            vw = vs[pl.ds(pl.multiple_of(base + ws, C_HALF_WINDOW), win), :]
            band = band_ref[(a - ws) // C_HALF_WINDOW, :, :win]
            s = lax.dot_general(q, kw, dn, preferred_element_type=F32) * scale + band
            m = jnp.max(s, axis=-1, keepdims=True)
            e = jnp.exp(s - m)
            l = jnp.sum(e, axis=-1, keepdims=True)
            acc = jnp.dot(e.astype(BF16), vw, preferred_element_type=F32)
            o_dst[rows, :] = acc * (1.0 / l)
            l_dst[rows, :] = jnp.broadcast_to(m + jnp.log(l), acc.shape)

        def blocks(it, carry, attend=attend):
            for u in range(C_UNROLL):
                attend(it * C_UNROLL + u)
            return carry

        lax.fori_loop(0, seq // (C_QBLK * C_UNROLL), blocks, 0)

        if dil > 1:
            top = C_DILATIONS[-1]
            run = seq // top
            pairs = ((og_ref, on_ref), (lg_ref, ln_ref))
            if dil < top:
                for src, dst in pairs:
                    for rho in range(dil):
                        for c in range(top // dil):
                            r = c * dil + rho
                            dst[r * run:(r + 1) * run, :] = src[
                                pl.ds(rho * sub_len + c, run, stride=top // dil), :]
                pairs = ((on_ref, og_ref), (ln_ref, lg_ref))

            def to_token_order(j, carry, pairs=pairs, top=top, run=run):
                for src, dst in pairs:
                    dst[pl.ds(pl.multiple_of(j * top, top), top), :] = src[
                        pl.ds(j, top, stride=run), :]
                return carry

            lax.fori_loop(0, run, to_token_order, 0, unroll=8)
            o_cur_ref, l_cur_ref = pairs[0][1], pairs[1][1]
            l_old = lacc_ref[...]
            l_cur = l_cur_ref[...]
            l_max = jnp.maximum(l_old, l_cur)
            w_old = jnp.exp(l_old - l_max)
            w_cur = jnp.exp(l_cur - l_max)
            inv = 1.0 / (w_old + w_cur)
            oacc_ref[...] = (w_old * oacc_ref[...] + w_cur * o_cur_ref[...]) * inv
            lacc_ref[...] = l_max + jnp.log(w_old + w_cur)

    o_ref[...] = oacc_ref[...].astype(BF16)


def _dilated_mixture(proj3):
    nb, seq, _ = proj3.shape

    def spec(kind, g):
        cb0 = _col_block(f"{kind}{g}")
        return pl.BlockSpec((None, seq, LANES), lambda b, h, cb0=cb0: (b, 0, cb0 + h))

    in_specs = [spec(kind, g) for kind in ("CQ", "CK", "CV") for g in range(len(C_DILATIONS))]
    return pl.pallas_call(
        functools.partial(_dil_kernel, seq=seq),
        grid=(nb, C_HEADS_PER_GROUP),
        in_specs=in_specs,
        out_specs=pl.BlockSpec((None, seq, LANES), lambda b, h: (b, 0, h)),
        out_shape=jax.ShapeDtypeStruct((nb, seq, BRANCH_WIDTH), BF16),
        scratch_shapes=[
            pltpu.VMEM((seq, LANES), F32),
            pltpu.VMEM((seq, LANES), BF16),
            pltpu.VMEM((seq, LANES), BF16),
            pltpu.VMEM((seq, LANES), BF16),
            pltpu.VMEM((seq, LANES), F32),
            pltpu.VMEM((seq, LANES), F32),
            pltpu.VMEM((seq, LANES), F32),
            pltpu.VMEM((seq, LANES), F32),
            pltpu.VMEM((seq, LANES), F32),
            pltpu.VMEM((seq, LANES), F32),
            pltpu.VMEM((3, C_QBLK, C_QBLK + 2 * C_HALF_WINDOW), F32),
        ],
        compiler_params=_cparams(("arbitrary", "arbitrary")),
        name="dil_attn",
    )(*([proj3] * 9))


def _tail_kernel(ya_ref, yb_ref, yc_ref, z0_ref, z1_ref, z2_ref, g0a_ref, g0b_ref, g1a_ref,
                 g1b_ref, g2a_ref, g2b_ref, x_ref, gate_ref, wbr_ref, wout_ref, o_ref):
    halves = [None, None]
    branches = ((ya_ref, z0_ref, (g0a_ref, g0b_ref)), (yb_ref, z1_ref, (g1a_ref, g1b_ref)),
                (yc_ref, z2_ref, (g2a_ref, g2b_ref)))
    for i, (y_ref, z_ref, g_refs) in enumerate(branches):
        yz = (y_ref[...].astype(F32) * z_ref[...].astype(F32)).astype(BF16)
        for half, g_ref in enumerate(g_refs):
            cols = slice(half * PROJ_TILE, (half + 1) * PROJ_TILE)
            u = g_ref[...].astype(F32) * jnp.dot(yz, wbr_ref[i, :, cols], preferred_element_type=F32)
            halves[half] = u if halves[half] is None else halves[half] + u
    merged = jnp.concatenate(halves, axis=1).astype(BF16)
    out = jnp.dot(merged, wout_ref[...], preferred_element_type=F32)
    o_ref[...] = x_ref[...] + gate_ref[0] * out


def _tail(ya, yb, yc, proj2, x2, mod3, w_br_bf, w_out_bf, seq, tm):
    ntok = x2.shape[0]
    tps = seq // tm
    y_spec = pl.BlockSpec((None, tm, BRANCH_WIDTH), lambda i: (i // tps, i % tps, 0))

    def tile_spec(name):
        return pl.BlockSpec((tm, PROJ_TILE), lambda i, s=SLOT[name]: (i, s))

    z_specs = [tile_spec(f"Z{t}") for t in range(3)]
    g_specs = [tile_spec(f"G{t}{half}") for t in range(3) for half in "ab"]
    return pl.pallas_call(
        _tail_kernel,
        grid=(ntok // tm,),
        in_specs=[y_spec, y_spec, y_spec, *z_specs, *g_specs,
                  pl.BlockSpec((tm, D_MODEL), lambda i: (i, 0)),
                  pl.BlockSpec((1, 1, D_MODEL), lambda i: (i // tps, 0, 2)),
                  pl.BlockSpec((3, BRANCH_WIDTH, D_MODEL), lambda i: (0, 0, 0)),
                  pl.BlockSpec((D_MODEL, D_MODEL), lambda i: (0, 0))],
        out_specs=pl.BlockSpec((tm, D_MODEL), lambda i: (i, 0)),
        out_shape=jax.ShapeDtypeStruct((ntok, D_MODEL), F32),
        compiler_params=_cparams(("arbitrary",)),
        name="tail",
    )(ya, yb, yc, *([proj2] * 9), x2, mod3, w_br_bf, w_out_bf)


def _rope_tables(seq):
    def base(d):
        inv = ROPE_THETA ** (-jnp.arange(0, d, 2, dtype=F32) / d)
        ang = jnp.arange(seq, dtype=F32)[:, None] * inv[None, :]
        return jnp.cos(ang), jnp.sin(ang)

    cos_a, sin_a = base(A_QK_DIM)
    ca = jnp.tile(cos_a, (1, 4))
    sa = jnp.concatenate([-sin_a, -sin_a, sin_a, sin_a], axis=-1)
    cos_c, sin_c = base(C_HEAD_DIM)
    cc = jnp.tile(cos_c, (1, 2))
    sc = jnp.concatenate([-sin_c, sin_c], axis=-1)
    return ca, sa, cc, sc


def _to_stored_columns(w):
    lead = w.shape[:-1]
    tiles = {name: w[..., i * PROJ_TILE:(i + 1) * PROJ_TILE] for i, name in enumerate(TILE_NAMES)}
    for name in ("AQ", "AK"):
        t = tiles[name].reshape(*lead, A_HEADS, 2, 2, A_QK_DIM // 2)
        tiles[name] = jnp.swapaxes(t, -3, -2).reshape(*lead, PROJ_TILE)
    return jnp.concatenate([tiles[n] for grp in PROJ_GROUPS for n in grp], axis=-1)


def _proj_gains(qn_a, kn_a, qn_b, kn_b, qn_c, kn_c):
    per_tile = {"AQ": jnp.tile(qn_a, PROJ_TILE // A_QK_DIM), "AK": jnp.tile(kn_a, PROJ_TILE // A_QK_DIM),
                "BQ": jnp.tile(qn_b, PROJ_TILE // B_HEAD_DIM), "BK": jnp.tile(kn_b, PROJ_TILE // B_HEAD_DIM)}
    for g in range(len(C_DILATIONS)):
        per_tile[f"CQ{g}"] = jnp.tile(qn_c, PROJ_TILE // C_HEAD_DIM)
        per_tile[f"CK{g}"] = jnp.tile(kn_c, PROJ_TILE // C_HEAD_DIM)
    ones = jnp.ones((PROJ_TILE,), F32)
    full = jnp.concatenate([per_tile.get(name, ones).astype(F32) for name in TILE_NAMES])
    return _to_stored_columns(full).reshape(len(PROJ_GROUPS), 1, GROUP_WIDTH)


def _proj_token_tile(seq):
    return min(1024, seq)


def _tail_token_tile(seq):
    return min(512, seq)


def _encoder_layer(x, mod3, layer_idx, tabs, ln_g, w_in_bf, gains, lam_params, subln, na_bias,
                   w_br_bf, w_out_bf):
    nb, seq, _ = x.shape
    x2 = x.reshape(nb * seq, D_MODEL)
    proj2 = _projection(x2, mod3, ln_g, w_in_bf, gains, tabs, seq, _proj_token_tile(seq))
    proj3 = proj2.reshape(nb, seq, IN_WIDTH)
    lam_init = 0.8 - 0.6 * math.exp(-0.3 * layer_idx)
    a_chunks = 4 if seq <= 2048 else 1
    ya = _diff_attention(proj3, lam_params, subln, lam_init, tq=256 * a_chunks, chunk=256)
    yb = _neighborhood_attention(proj3, na_bias)
    yc = _dilated_mixture(proj3)
    y2 = _tail(ya, yb, yc, proj2, x2, mod3, w_br_bf, w_out_bf, seq, _tail_token_tile(seq))
    return y2.reshape(nb, seq, D_MODEL)


def kernel(x_prompt, x_sample, c_prompt, c_sample, ln_g, w_ada, b_ada, w_in, qn_a, kn_a, lam_q1, lam_k1, lam_q2, lam_k2, subln_a, qn_b, kn_b, rpb_b, qn_c, kn_c, w_br, w_out):
    depth = w_in.shape[0]
    n_prompt = c_prompt.shape[0]
    mod_all = _modulation(jnp.concatenate([c_prompt, c_sample], axis=0), w_ada, b_ada)
    w_in_bf = _to_stored_columns(w_in.astype(BF16))
    w_br_bf = w_br.astype(BF16)
    w_out_bf = w_out.astype(BF16)
    gains = [_proj_gains(qn_a[l], kn_a[l], qn_b[l], kn_b[l], qn_c[l], kn_c[l]) for l in range(depth)]
    na_bias = [_na_bias_tables(rpb_b[l]) for l in range(depth)]

    def run(x, mod):
        nb, seq, _ = x.shape
        assert seq % (NA_QROWS * GRID_W) == 0 and seq // GRID_W >= NA_KROWS
        tabs = _rope_tables(seq)
        for l in range(depth):
            lam_params = tuple(p[l].reshape(1, A_QK_DIM) for p in (lam_q1, lam_k1, lam_q2, lam_k2))
            x = _encoder_layer(x, mod[l].reshape(nb, 1, 3 * D_MODEL), l, tabs,
                               ln_g[l].reshape(1, D_MODEL), w_in_bf[l], gains[l], lam_params,
                               subln_a[l].reshape(1, LANES), na_bias[l], w_br_bf[l], w_out_bf[l])
        return x

    y_prompt = run(x_prompt, mod_all[:, :n_prompt])
    y_sample = run(x_sample, mod_all[:, n_prompt:])
    return (y_prompt, y_sample)
```

```python
import functools
import math

import jax
import jax.numpy as jnp
from jax import lax
from jax.experimental import pallas as pl
from jax.experimental.pallas import tpu as pltpu

F32 = jnp.float32
BF16 = jnp.bfloat16

D_MODEL = 1024
GRID_W = 64
BRANCH_WIDTH = 512
ROPE_THETA = 10000.0
EPS = 1e-6
NEG = -1e30
A_QK_DIM = 64
A_HEADS = 4
B_HEAD_DIM = 64
B_HEADS = 8
NA_KH = 8
NA_KW = 16
C_DILATIONS = (1, 4, 16)
C_HALF_WINDOW = 64
C_HEAD_DIM = 128
C_HEADS_PER_GROUP = 4
IN_WIDTH = 12288
LANES = 128

PROJ_TILE = 512
TILE_NAMES = ("AQ", "AK", "AV", "BQ", "BK", "BV", "CQ0", "CQ1", "CQ2", "CK0", "CK1", "CK2",
              "CV0", "CV1", "CV2", "Z0", "Z1", "Z2", "G0a", "G0b", "G1a", "G1b", "G2a", "G2b")
PROJ_GROUPS = (("AQ", "Z0", "BQ", "AV"), ("AK", "Z1", "BK", "BV"),
               ("CQ0", "G0a", "CK0", "CV0"), ("CQ1", "G0b", "CK1", "CV1"),
               ("CQ2", "G1a", "CK2", "CV2"), ("Z2", "G1b", "G2a", "G2b"))
GROUP_TILES = len(PROJ_GROUPS[0])
GROUP_WIDTH = GROUP_TILES * PROJ_TILE
SLOT = {name: i for i, name in enumerate(n for grp in PROJ_GROUPS for n in grp)}


def _col_block(name, sub=0):
    return SLOT[name] * (PROJ_TILE // LANES) + sub

VMEM_LIMIT = 56 * 1024 * 1024

NA_QROWS = 8
NA_KROWS = 16
A_QTILE = 256
C_QBLK = 128
C_UNROLL = 8


def _cparams(sem):
    return pltpu.CompilerParams(dimension_semantics=sem, vmem_limit_bytes=VMEM_LIMIT)


def _sigmoid(x):
    return 1.0 / (1.0 + jnp.exp(-x))


def _mod_kernel(c_ref, w_ref, b_ref, o_ref):
    c = c_ref[...]
    o_ref[0] = jnp.dot(c * _sigmoid(c), w_ref[0], preferred_element_type=F32) + b_ref[0]


def _modulation(c_all, w_ada, b_ada):
    depth = w_ada.shape[0]
    nb = c_all.shape[0]
    return pl.pallas_call(
        _mod_kernel,
        grid=(depth, 3),
        in_specs=[
            pl.BlockSpec((nb, D_MODEL), lambda l, j: (0, 0)),
            pl.BlockSpec((1, D_MODEL, D_MODEL), lambda l, j: (l, 0, j)),
            pl.BlockSpec((1, 1, D_MODEL), lambda l, j: (l, 0, j)),
        ],
        out_specs=pl.BlockSpec((1, nb, D_MODEL), lambda l, j: (l, 0, j)),
        out_shape=jax.ShapeDtypeStruct((depth, nb, 3 * D_MODEL), F32),
        compiler_params=_cparams(("arbitrary", "arbitrary")),
        name="adaln_mod",
    )(c_all, w_ada, b_ada.reshape(depth, 1, 3 * D_MODEL))


def _first_map_lanes(shape):
    return (lax.broadcasted_iota(jnp.int32, shape, 1) & 32) == 0


def _first_head_lanes(shape):
    return lax.broadcasted_iota(jnp.int32, shape, 1) < 64


def _rms_halves(xb, first):
    sq = xb * xb
    s_a = jnp.sum(jnp.where(first, sq, 0.0), axis=-1, keepdims=True)
    s_b = jnp.sum(jnp.where(first, 0.0, sq), axis=-1, keepdims=True)
    return lax.rsqrt(jnp.where(first, s_a, s_b) * (1.0 / 64.0) + EPS)


def _rms128(xb):
    return lax.rsqrt(jnp.mean(xb * xb, axis=-1, keepdims=True) + EPS)


def _rope(xb, cos, sin_signed):
    return xb * cos + pltpu.roll(xb, 64, 1) * sin_signed


def _tile_epilogue(name, xb, gain, tabs):
    ca_ref, sa_ref, cc_ref, sc_ref = tabs
    if name == "AQ":
        rms = _rms_halves(xb, _first_map_lanes(xb.shape))
        return _rope(xb * (rms * gain), ca_ref[...], sa_ref[...]) * 0.125
    if name == "AK":
        rms = _rms_halves(xb, _first_map_lanes(xb.shape))
        return _rope(xb * (rms * gain), ca_ref[...], sa_ref[...])
    if name == "BQ":
        return xb * gain * (_rms_halves(xb, _first_head_lanes(xb.shape)) * 0.125)
    if name == "BK":
        return xb * gain * _rms_halves(xb, _first_head_lanes(xb.shape))
    if name[:2] in ("CQ", "CK"):
        return _rope(xb * gain, cc_ref[...], sc_ref[...]) * _rms128(xb)
    if name[0] == "Z":
        return xb * _sigmoid(xb)
    if name[0] == "G":
        return _sigmoid(xb)
    return xb


def _proj_kernel(x_ref, mod_ref, lng_ref, w_ref, gain_ref, ca_ref, sa_ref, cc_ref, sc_ref,
                 o_ref, h_ref, acc_ref):
    grp = pl.program_id(1)

    @pl.when(grp == 0)
    def _():
        x = x_ref[...]
        ms = jnp.mean(x * x, axis=-1, keepdims=True)
        y = x * lax.rsqrt(ms + EPS) * lng_ref[...]
        mod = mod_ref[0]
        shift = mod[:, :D_MODEL]
        scale = mod[:, D_MODEL:2 * D_MODEL]
        h_ref[...] = (y * (1.0 + scale) + shift).astype(BF16)

    tabs = (ca_ref, sa_ref, cc_ref, sc_ref)
    for gi, names in enumerate(PROJ_GROUPS):
        @pl.when(grp == gi)
        def _(names=names):
            gain = gain_ref[0]
            for t, name in enumerate(names):
                acc = acc_ref.at[t % 2]
                acc[...] = jnp.dot(h_ref[...], w_ref[:, t * PROJ_TILE:(t + 1) * PROJ_TILE],
                                   preferred_element_type=F32)
                for cb in range(PROJ_TILE // LANES):
                    sl = slice(cb * LANES, (cb + 1) * LANES)
                    out = slice(t * PROJ_TILE + cb * LANES, t * PROJ_TILE + (cb + 1) * LANES)
                    o_ref[:, out] = _tile_epilogue(name, acc[:, sl], gain[:, out], tabs).astype(BF16)


def _projection(x2, mod3, ln_g, w_in_bf, gains, tabs, seq, tm):
    ntok = x2.shape[0]
    tps = seq // tm
    ca, sa, cc, sc = tabs
    tab_spec = pl.BlockSpec((tm, LANES), lambda i, j: (i % tps, 0))
    return pl.pallas_call(
        _proj_kernel,
        grid=(ntok // tm, len(PROJ_GROUPS)),
        in_specs=[
            pl.BlockSpec((tm, D_MODEL), lambda i, j: (i, 0)),
            pl.BlockSpec((1, 1, 3 * D_MODEL), lambda i, j: (i // tps, 0, 0)),
            pl.BlockSpec((1, D_MODEL), lambda i, j: (0, 0)),
            pl.BlockSpec((D_MODEL, GROUP_WIDTH), lambda i, j: (0, j)),
            pl.BlockSpec((1, 1, GROUP_WIDTH), lambda i, j: (j, 0, 0)),
            tab_spec, tab_spec, tab_spec, tab_spec,
        ],
        out_specs=pl.BlockSpec((tm, GROUP_WIDTH), lambda i, j: (i, j)),
        out_shape=jax.ShapeDtypeStruct((ntok, IN_WIDTH), BF16),
        scratch_shapes=[pltpu.VMEM((tm, D_MODEL), BF16), pltpu.VMEM((2, tm, PROJ_TILE), F32)],
        compiler_params=_cparams(("arbitrary", "arbitrary")),
        name="in_proj",
    )(x2, mod3, ln_g, w_in_bf, gains, ca, sa, cc, sc)


def _diff_attn_kernel(q_ref, k_ref, v_ref, lq1_ref, lk1_ref, lq2_ref, lk2_ref, sub_ref, o_ref,
                      s_even_ref, s_odd_ref, *, lam_init):
    t = pl.program_id(0)
    dn = (((1,), (1,)), ((), ()))

    def score(s_ref):
        q = q_ref[...]
        k = k_ref[...]
        lo = _first_map_lanes(q.shape)
        zero = jnp.zeros_like(q)
        s_ref[0] = lax.dot_general(jnp.where(lo, q, zero), k, dn, preferred_element_type=F32)
        s_ref[1] = lax.dot_general(jnp.where(lo, zero, q), k, dn, preferred_element_type=F32)

    def finish(s_ref):
        lam = (jnp.exp(jnp.sum(lq1_ref[...] * lk1_ref[...], axis=-1, keepdims=True))
               - jnp.exp(jnp.sum(lq2_ref[...] * lk2_ref[...], axis=-1, keepdims=True)) + lam_init)
        s0 = s_ref[0]
        s1 = s_ref[1]
        e0 = jnp.exp(s0 - jnp.max(s0, axis=-1, keepdims=True))
        e1 = jnp.exp(s1 - jnp.max(s1, axis=-1, keepdims=True))
        l0 = jnp.sum(e0, axis=-1, keepdims=True)
        l1 = jnp.sum(e1, axis=-1, keepdims=True)
        w = e0 * (1.0 / l0) - e1 * (lam / l1)
        o = jnp.dot(w.astype(BF16), v_ref[...], preferred_element_type=F32)
        ms = jnp.mean(o * o, axis=-1, keepdims=True)
        o_ref[...] = (o * lax.rsqrt(ms + EPS) * sub_ref[...] * (1.0 - lam_init)).astype(BF16)

    @pl.when(t == 0)
    def _():
        s_odd_ref[...] = jnp.zeros_like(s_odd_ref)

    @pl.when(t % 2 == 0)
    def _():
        score(s_even_ref)
        finish(s_odd_ref)

    @pl.when(t % 2 == 1)
    def _():
        score(s_odd_ref)
        finish(s_even_ref)


def _diff_attention(proj3, lam_params, subln, lam_init, tq):
    nb, seq, _ = proj3.shape
    nq = seq // tq
    tiles = nb * A_HEADS * nq

    def split(t):
        return t // (A_HEADS * nq), (t // nq) % A_HEADS, t % nq

    def cur(t):
        return split(jnp.minimum(t, tiles - 1))

    def prev(t):
        return split(jnp.maximum(t - 1, 0))

    vec = pl.BlockSpec((1, A_QK_DIM), lambda t: (0, 0))
    return pl.pallas_call(
        functools.partial(_diff_attn_kernel, lam_init=lam_init),
        grid=(tiles + 1,),
        in_specs=[
            pl.BlockSpec((None, tq, LANES), lambda t: (cur(t)[0], cur(t)[2], _col_block("AQ") + cur(t)[1])),
            pl.BlockSpec((None, seq, LANES), lambda t: (cur(t)[0], 0, _col_block("AK") + cur(t)[1])),
            pl.BlockSpec((None, seq, LANES), lambda t: (prev(t)[0], 0, _col_block("AV") + prev(t)[1])),
            vec, vec, vec, vec,
            pl.BlockSpec((1, LANES), lambda t: (0, 0)),
        ],
        out_specs=pl.BlockSpec((None, tq, LANES), lambda t: (prev(t)[0], prev(t)[2], prev(t)[1])),
        out_shape=jax.ShapeDtypeStruct((nb, seq, BRANCH_WIDTH), BF16),
        scratch_shapes=[pltpu.VMEM((2, tq, seq), F32), pltpu.VMEM((2, tq, seq), F32)],
        compiler_params=_cparams(("arbitrary",)),
        name="diff_attn",
    )(proj3, proj3, proj3, *lam_params, subln)


def _na_bias_tables(rpb):
    rows = NA_QROWS + NA_KROWS
    qc = jnp.arange(GRID_W)
    kc = jnp.arange(GRID_W)
    cs = jnp.clip(qc - NA_KW // 2, 0, GRID_W - NA_KW)
    col_ok = (kc[None, :] >= cs[:, None]) & (kc[None, :] < cs[:, None] + NA_KW)
    col_off = jnp.clip(kc[None, :] - qc[:, None], -(NA_KW - 1), NA_KW - 1) + NA_KW - 1
    cols = jnp.where(col_ok, rpb.astype(F32)[:, :, col_off], NEG)
    kh = min(NA_KH, rows)
    tables = []
    for r0, ws in ((0, 0), (NA_QROWS, NA_QROWS - kh // 2), (rows - NA_QROWS, rows - NA_KROWS)):
        r = r0 + jnp.arange(NA_QROWS)
        kr = ws + jnp.arange(NA_KROWS)
        rs = jnp.clip(r - kh // 2, 0, rows - kh)
        row_ok = (kr[None, :] >= rs[:, None]) & (kr[None, :] < rs[:, None] + kh)
        row_off = jnp.clip(kr[None, :] - r[:, None] + NA_KH - 1, 0, 2 * NA_KH - 2)
        t = jnp.where(row_ok[None, :, :, None, None], cols[:, row_off], NEG)
        tables.append(t.transpose(0, 1, 3, 2, 4).reshape(
            B_HEADS, NA_QROWS * GRID_W, NA_KROWS * GRID_W))
    return jnp.stack(tables)


def _na_kernel(q_ref, k_ref, v_ref, bias_ref, o_ref, s_even_ref, s_odd_ref, *, rows, nrb):
    t = pl.program_id(0)
    nkeys = NA_KROWS * GRID_W
    dn = (((1,), (1,)), ((), ()))

    def key_start(i):
        ws = jnp.clip(i * NA_QROWS - NA_KH // 2, 0, rows - NA_KROWS)
        return pl.multiple_of(ws * GRID_W, GRID_W)

    def score(s_ref):
        i = jnp.minimum(t, pl.num_programs(0) - 2) % nrb
        kind = jnp.where(i == 0, 0, jnp.where(i == nrb - 1, 2, 1))
        kw = k_ref[pl.ds(key_start(i), nkeys), :]
        q = q_ref[...]
        lo = _first_head_lanes(q.shape)
        zero = jnp.zeros_like(q)
        for e in range(2):
            qe = jnp.where(lo, q, zero) if e == 0 else jnp.where(lo, zero, q)
            s_ref[e] = lax.dot_general(qe, kw, dn, preferred_element_type=F32) + bias_ref[kind, e]

    def finish(s_ref):
        i = jnp.maximum(t - 1, 0) % nrb
        vw = v_ref[pl.ds(key_start(i), nkeys), :]
        outs = []
        for e in range(2):
            s = s_ref[e]
            ex = jnp.exp(s - jnp.max(s, axis=-1, keepdims=True))
            p = ex * (1.0 / jnp.sum(ex, axis=-1, keepdims=True))
            outs.append(jnp.dot(p.astype(BF16), vw, preferred_element_type=F32))
        o_ref[...] = jnp.where(_first_head_lanes(outs[0].shape), outs[0], outs[1]).astype(BF16)

    @pl.when(t == 0)
    def _():
        s_odd_ref[...] = jnp.zeros_like(s_odd_ref)

    @pl.when(t % 2 == 0)
    def _():
        score(s_even_ref)
        finish(s_odd_ref)

    @pl.when(t % 2 == 1)
    def _():
        score(s_odd_ref)
        finish(s_even_ref)


def _neighborhood_attention(proj3, bias):
    nb, seq, _ = proj3.shape
    rows = seq // GRID_W
    nrb = rows // NA_QROWS
    nq = NA_QROWS * GRID_W
    nk = NA_KROWS * GRID_W
    tiles = (B_HEADS // 2) * nb * nrb

    def split(t):
        return t // (nb * nrb), (t // nrb) % nb, t % nrb

    def cur(t):
        return split(jnp.minimum(t, tiles - 1))

    def prev(t):
        return split(jnp.maximum(t - 1, 0))

    return pl.pallas_call(
        functools.partial(_na_kernel, rows=rows, nrb=nrb),
        grid=(tiles + 1,),
        in_specs=[
            pl.BlockSpec((None, nq, LANES), lambda t: (cur(t)[1], cur(t)[2], _col_block("BQ") + cur(t)[0])),
            pl.BlockSpec((None, seq, LANES), lambda t: (cur(t)[1], 0, _col_block("BK") + cur(t)[0])),
            pl.BlockSpec((None, seq, LANES), lambda t: (prev(t)[1], 0, _col_block("BV") + prev(t)[0])),
            pl.BlockSpec((3, 2, nq, nk), lambda t: (0, cur(t)[0], 0, 0)),
        ],
        out_specs=pl.BlockSpec((None, nq, LANES), lambda t: (prev(t)[1], prev(t)[2], prev(t)[0])),
        out_shape=jax.ShapeDtypeStruct((nb, seq, BRANCH_WIDTH), BF16),
        scratch_shapes=[pltpu.VMEM((2, nq, nk), F32), pltpu.VMEM((2, nq, nk), F32)],
        compiler_params=_cparams(("arbitrary",)),
        name="nbr_attn",
    )(proj3, proj3, proj3, bias)


def _dil_kernel(q0_ref, q1_ref, q2_ref, k0_ref, k1_ref, k2_ref, v0_ref, v1_ref, v2_ref, o_ref,
                stage_ref, qd_ref, kd_ref, vd_ref, og_ref, lg_ref, on_ref, ln_ref, oacc_ref,
                lacc_ref, band_ref, *, seq):
    scale = C_HEAD_DIM ** -0.5
    dn = (((1,), (1,)), ((), ()))
    groups = ((q0_ref, k0_ref, v0_ref), (q1_ref, k1_ref, v1_ref), (q2_ref, k2_ref, v2_ref))

    full_win = C_QBLK + 2 * C_HALF_WINDOW
    qi = lax.broadcasted_iota(jnp.int32, (C_QBLK, full_win), 0)
    kj = lax.broadcasted_iota(jnp.int32, (C_QBLK, full_win), 1)
    for lead in range(3):
        band_ref[lead] = jnp.where(
            jnp.abs(kj - qi - lead * C_HALF_WINDOW) <= C_HALF_WINDOW, 0.0, NEG)

    for g, dil in enumerate(C_DILATIONS):
        sub_len = seq // dil
        win = min(sub_len, C_QBLK + 2 * C_HALF_WINDOW)
        bps = sub_len // C_QBLK
        qs, ks, vs = groups[g]
        if dil > 1:
            for src, dst in ((qs, qd_ref), (ks, kd_ref), (vs, vd_ref)):
                stage_ref[...] = src[...].astype(F32)
                for rho in range(dil):
                    dst[rho * sub_len:(rho + 1) * sub_len, :] = stage_ref[
                        pl.ds(rho, sub_len, stride=dil), :].astype(BF16)
            qs, ks, vs = qd_ref, kd_ref, vd_ref
        o_dst, l_dst = (oacc_ref, lacc_ref) if dil == 1 else (og_ref, lg_ref)

        def attend(n, qs=qs, ks=ks, vs=vs, sub_len=sub_len, win=win, bps=bps, o_dst=o_dst,
                   l_dst=l_dst):
            a = (n % bps) * C_QBLK
            base = (n // bps) * sub_len
            ws = jnp.clip(a - C_HALF_WINDOW, 0, sub_len - win)
            rows = pl.ds(pl.multiple_of(n * C_QBLK, C_QBLK), C_QBLK)
            q = qs[rows, :]
            kw = ks[pl.ds(pl.multiple_of(base + ws, C_HALF_WINDOW), win), :]
            vw = vs[pl.ds(pl.multiple_of(base + ws, C_HALF_WINDOW), win), :]
            band = band_ref[(a - ws) // C_HALF_WINDOW, :, :win]
            s = lax.dot_general(q, kw, dn, preferred_element_type=F32) * scale + band
            m = jnp.max(s, axis=-1, keepdims=True)
            e = jnp.exp(s - m)
            l = jnp.sum(e, axis=-1, keepdims=True)
            acc = jnp.dot(e.astype(BF16), vw, preferred_element_type=F32)
            o_dst[rows, :] = acc * (1.0 / l)
            l_dst[rows, :] = jnp.broadcast_to(m + jnp.log(l), acc.shape)

        def blocks(it, carry, attend=attend):
            for u in range(C_UNROLL):
                attend(it * C_UNROLL + u)
            return carry

        lax.fori_loop(0, seq // (C_QBLK * C_UNROLL), blocks, 0)

        if dil > 1:
            top = C_DILATIONS[-1]
            run = seq // top
            pairs = ((og_ref, on_ref), (lg_ref, ln_ref))
            if dil < top:
                for src, dst in pairs:
                    for rho in range(dil):
                        for c in range(top // dil):
                            r = c * dil + rho
                            dst[r * run:(r + 1) * run, :] = src[
                                pl.ds(rho * sub_len + c, run, stride=top // dil), :]
                pairs = ((on_ref, og_ref), (ln_ref, lg_ref))

            def to_token_order(j, carry, pairs=pairs, top=top, run=run):
                for src, dst in pairs:
                    dst[pl.ds(pl.multiple_of(j * top, top), top), :] = src[
                        pl.ds(j, top, stride=run), :]
                return carry

            lax.fori_loop(0, run, to_token_order, 0, unroll=8)
            o_cur_ref, l_cur_ref = pairs[0][1], pairs[1][1]
            l_old = lacc_ref[...]
            l_cur = l_cur_ref[...]
            l_max = jnp.maximum(l_old, l_cur)
            w_old = jnp.exp(l_old - l_max)
            w_cur = jnp.exp(l_cur - l_max)
            inv = 1.0 / (w_old + w_cur)
            oacc_ref[...] = (w_old * oacc_ref[...] + w_cur * o_cur_ref[...]) * inv
            lacc_ref[...] = l_max + jnp.log(w_old + w_cur)

    o_ref[...] = oacc_ref[...].astype(BF16)


def _dilated_mixture(proj3):
    nb, seq, _ = proj3.shape

    def spec(kind, g):
        cb0 = _col_block(f"{kind}{g}")
        return pl.BlockSpec((None, seq, LANES), lambda b, h, cb0=cb0: (b, 0, cb0 + h))

    in_specs = [spec(kind, g) for kind in ("CQ", "CK", "CV") for g in range(len(C_DILATIONS))]
    return pl.pallas_call(
        functools.partial(_dil_kernel, seq=seq),
        grid=(nb, C_HEADS_PER_GROUP),
        in_specs=in_specs,
        out_specs=pl.BlockSpec((None, seq, LANES), lambda b, h: (b, 0, h)),
        out_shape=jax.ShapeDtypeStruct((nb, seq, BRANCH_WIDTH), BF16),
        scratch_shapes=[
            pltpu.VMEM((seq, LANES), F32),
            pltpu.VMEM((seq, LANES), BF16),
            pltpu.VMEM((seq, LANES), BF16),
            pltpu.VMEM((seq, LANES), BF16),
            pltpu.VMEM((seq, LANES), F32),
            pltpu.VMEM((seq, LANES), F32),
            pltpu.VMEM((seq, LANES), F32),
            pltpu.VMEM((seq, LANES), F32),
            pltpu.VMEM((seq, LANES), F32),
            pltpu.VMEM((seq, LANES), F32),
            pltpu.VMEM((3, C_QBLK, C_QBLK + 2 * C_HALF_WINDOW), F32),
        ],
        compiler_params=_cparams(("arbitrary", "arbitrary")),
        name="dil_attn",
    )(*([proj3] * 9))


def _tail_kernel(ya_ref, yb_ref, yc_ref, z0_ref, z1_ref, z2_ref, g0a_ref, g0b_ref, g1a_ref,
                 g1b_ref, g2a_ref, g2b_ref, x_ref, gate_ref, wbr_ref, wout_ref, o_ref):
    halves = [None, None]
    branches = ((ya_ref, z0_ref, (g0a_ref, g0b_ref)), (yb_ref, z1_ref, (g1a_ref, g1b_ref)),
                (yc_ref, z2_ref, (g2a_ref, g2b_ref)))
    for i, (y_ref, z_ref, g_refs) in enumerate(branches):
        yz = (y_ref[...].astype(F32) * z_ref[...].astype(F32)).astype(BF16)
        for half, g_ref in enumerate(g_refs):
            cols = slice(half * PROJ_TILE, (half + 1) * PROJ_TILE)
            u = g_ref[...].astype(F32) * jnp.dot(yz, wbr_ref[i, :, cols], preferred_element_type=F32)
            halves[half] = u if halves[half] is None else halves[half] + u
    merged = jnp.concatenate(halves, axis=1).astype(BF16)
    out = jnp.dot(merged, wout_ref[...], preferred_element_type=F32)
    o_ref[...] = x_ref[...] + gate_ref[0] * out


def _tail(ya, yb, yc, proj2, x2, mod3, w_br_bf, w_out_bf, seq, tm):
    ntok = x2.shape[0]
    tps = seq // tm
    y_spec = pl.BlockSpec((None, tm, BRANCH_WIDTH), lambda i: (i // tps, i % tps, 0))

    def tile_spec(name):
        return pl.BlockSpec((tm, PROJ_TILE), lambda i, s=SLOT[name]: (i, s))

    z_specs = [tile_spec(f"Z{t}") for t in range(3)]
    g_specs = [tile_spec(f"G{t}{half}") for t in range(3) for half in "ab"]
    return pl.pallas_call(
        _tail_kernel,
        grid=(ntok // tm,),
        in_specs=[y_spec, y_spec, y_spec, *z_specs, *g_specs,
                  pl.BlockSpec((tm, D_MODEL), lambda i: (i, 0)),
                  pl.BlockSpec((1, 1, D_MODEL), lambda i: (i // tps, 0, 2)),
                  pl.BlockSpec((3, BRANCH_WIDTH, D_MODEL), lambda i: (0, 0, 0)),
                  pl.BlockSpec((D_MODEL, D_MODEL), lambda i: (0, 0))],
        out_specs=pl.BlockSpec((tm, D_MODEL), lambda i: (i, 0)),
        out_shape=jax.ShapeDtypeStruct((ntok, D_MODEL), F32),
        compiler_params=_cparams(("arbitrary",)),
        name="tail",
    )(ya, yb, yc, *([proj2] * 9), x2, mod3, w_br_bf, w_out_bf)


def _rope_tables(seq):
    def base(d):
        inv = ROPE_THETA ** (-jnp.arange(0, d, 2, dtype=F32) / d)
        ang = jnp.arange(seq, dtype=F32)[:, None] * inv[None, :]
        return jnp.cos(ang), jnp.sin(ang)

    cos_a, sin_a = base(A_QK_DIM)
    ca = jnp.tile(cos_a, (1, 4))
    sa = jnp.concatenate([-sin_a, -sin_a, sin_a, sin_a], axis=-1)
    cos_c, sin_c = base(C_HEAD_DIM)
    cc = jnp.tile(cos_c, (1, 2))
    sc = jnp.concatenate([-sin_c, sin_c], axis=-1)
    return ca, sa, cc, sc


def _to_stored_columns(w):
    lead = w.shape[:-1]
    tiles = {name: w[..., i * PROJ_TILE:(i + 1) * PROJ_TILE] for i, name in enumerate(TILE_NAMES)}
    for name in ("AQ", "AK"):
        t = tiles[name].reshape(*lead, A_HEADS, 2, 2, A_QK_DIM // 2)
        tiles[name] = jnp.swapaxes(t, -3, -2).reshape(*lead, PROJ_TILE)
    return jnp.concatenate([tiles[n] for grp in PROJ_GROUPS for n in grp], axis=-1)


def _proj_gains(qn_a, kn_a, qn_b, kn_b, qn_c, kn_c):
    per_tile = {"AQ": jnp.tile(qn_a, PROJ_TILE // A_QK_DIM), "AK": jnp.tile(kn_a, PROJ_TILE // A_QK_DIM),
                "BQ": jnp.tile(qn_b, PROJ_TILE // B_HEAD_DIM), "BK": jnp.tile(kn_b, PROJ_TILE // B_HEAD_DIM)}
    for g in range(len(C_DILATIONS)):
        per_tile[f"CQ{g}"] = jnp.tile(qn_c, PROJ_TILE // C_HEAD_DIM)
        per_tile[f"CK{g}"] = jnp.tile(kn_c, PROJ_TILE // C_HEAD_DIM)
    ones = jnp.ones((PROJ_TILE,), F32)
    full = jnp.concatenate([per_tile.get(name, ones).astype(F32) for name in TILE_NAMES])
    return _to_stored_columns(full).reshape(len(PROJ_GROUPS), 1, GROUP_WIDTH)


def _proj_token_tile(seq):
    return min(1024, seq)


def _tail_token_tile(seq):
    return min(512, seq)


def _encoder_layer(x, mod3, layer_idx, tabs, ln_g, w_in_bf, gains, lam_params, subln, na_bias,
                   w_br_bf, w_out_bf):
    nb, seq, _ = x.shape
    x2 = x.reshape(nb * seq, D_MODEL)
    proj2 = _projection(x2, mod3, ln_g, w_in_bf, gains, tabs, seq, _proj_token_tile(seq))
    proj3 = proj2.reshape(nb, seq, IN_WIDTH)
    lam_init = 0.8 - 0.6 * math.exp(-0.3 * layer_idx)
    ya = _diff_attention(proj3, lam_params, subln, lam_init, tq=A_QTILE)
    yb = _neighborhood_attention(proj3, na_bias)
    yc = _dilated_mixture(proj3)
    y2 = _tail(ya, yb, yc, proj2, x2, mod3, w_br_bf, w_out_bf, seq, _tail_token_tile(seq))
    return y2.reshape(nb, seq, D_MODEL)


def kernel(x_prompt, x_sample, c_prompt, c_sample, ln_g, w_ada, b_ada, w_in, qn_a, kn_a, lam_q1, lam_k1, lam_q2, lam_k2, subln_a, qn_b, kn_b, rpb_b, qn_c, kn_c, w_br, w_out):
    depth = w_in.shape[0]
    n_prompt = c_prompt.shape[0]
    mod_all = _modulation(jnp.concatenate([c_prompt, c_sample], axis=0), w_ada, b_ada)
    w_in_bf = _to_stored_columns(w_in.astype(BF16))
    w_br_bf = w_br.astype(BF16)
    w_out_bf = w_out.astype(BF16)
    gains = [_proj_gains(qn_a[l], kn_a[l], qn_b[l], kn_b[l], qn_c[l], kn_c[l]) for l in range(depth)]
    na_bias = [_na_bias_tables(rpb_b[l]) for l in range(depth)]

    def run(x, mod):
        nb, seq, _ = x.shape
        assert seq % (NA_QROWS * GRID_W) == 0 and seq // GRID_W >= NA_KROWS
        tabs = _rope_tables(seq)
        for l in range(depth):
            lam_params = tuple(p[l].reshape(1, A_QK_DIM) for p in (lam_q1, lam_k1, lam_q2, lam_k2))
            x = _encoder_layer(x, mod[l].reshape(nb, 1, 3 * D_MODEL), l, tabs,
                               ln_g[l].reshape(1, D_MODEL), w_in_bf[l], gains[l], lam_params,
                               subln_a[l].reshape(1, LANES), na_bias[l], w_br_bf[l], w_out_bf[l])
        return x

    y_prompt = run(x_prompt, mod_all[:, :n_prompt])
    y_sample = run(x_sample, mod_all[:, n_prompt:])
    return (y_prompt, y_sample)
```

```python
import functools
import math

import jax
import jax.numpy as jnp
from jax import lax
from jax.experimental import pallas as pl
from jax.experimental.pallas import tpu as pltpu

F32 = jnp.float32
BF16 = jnp.bfloat16

D_MODEL = 1024
GRID_W = 64
BRANCH_WIDTH = 512
ROPE_THETA = 10000.0
EPS = 1e-6
NEG = -1e30
A_QK_DIM = 64
A_HEADS = 4
B_HEAD_DIM = 64
B_HEADS = 8
NA_KH = 8
NA_KW = 16
C_DILATIONS = (1, 4, 16)
C_HALF_WINDOW = 64
C_HEAD_DIM = 128
C_HEADS_PER_GROUP = 4
IN_WIDTH = 12288
LANES = 128

PROJ_TILE = 512
TILE_NAMES = ("AQ", "AK", "AV", "BQ", "BK", "BV", "CQ0", "CQ1", "CQ2", "CK0", "CK1", "CK2",
              "CV0", "CV1", "CV2", "Z0", "Z1", "Z2", "G0a", "G0b", "G1a", "G1b", "G2a", "G2b")
PROJ_GROUPS = (("AQ", "Z0", "BQ", "AV"), ("AK", "Z1", "BK", "BV"),
               ("CQ0", "G0a", "CK0", "CV0"), ("CQ1", "G0b", "CK1", "CV1"),
               ("CQ2", "G1a", "CK2", "CV2"), ("Z2", "G1b", "G2a", "G2b"))
C_TOP_DIL = C_DILATIONS[-1]
RESIDUE_MAJOR_TILES = tuple(f"C{kind}{g}" for kind in "QKV" for g in (1, 2))
GROUP_TILES = len(PROJ_GROUPS[0])
GROUP_WIDTH = GROUP_TILES * PROJ_TILE
SLOT = {name: i for i, name in enumerate(n for grp in PROJ_GROUPS for n in grp)}


def _col_block(name, sub=0):
    return SLOT[name] * (PROJ_TILE // LANES) + sub

VMEM_LIMIT = 56 * 1024 * 1024

NA_QROWS = 8
NA_KROWS = 16
A_SCORE_BYTES = 16 * 1024 * 1024
C_QBLK = 128
C_UNROLL = 8


def _cparams(sem):
    return pltpu.CompilerParams(dimension_semantics=sem, vmem_limit_bytes=VMEM_LIMIT)


def _sigmoid(x):
    return 1.0 / (1.0 + jnp.exp(-x))


def _mod_kernel(c_ref, w_ref, b_ref, o_ref):
    c = c_ref[...]
    o_ref[0] = jnp.dot(c * _sigmoid(c), w_ref[0], preferred_element_type=F32) + b_ref[0]


def _modulation(c_all, w_ada, b_ada):
    depth = w_ada.shape[0]
    nb = c_all.shape[0]
    return pl.pallas_call(
        _mod_kernel,
        grid=(depth, 3),
        in_specs=[
            pl.BlockSpec((nb, D_MODEL), lambda l, j: (0, 0)),
            pl.BlockSpec((1, D_MODEL, D_MODEL), lambda l, j: (l, 0, j)),
            pl.BlockSpec((1, 1, D_MODEL), lambda l, j: (l, 0, j)),
        ],
        out_specs=pl.BlockSpec((1, nb, D_MODEL), lambda l, j: (l, 0, j)),
        out_shape=jax.ShapeDtypeStruct((depth, nb, 3 * D_MODEL), F32),
        compiler_params=_cparams(("arbitrary", "arbitrary")),
        name="adaln_mod",
    )(c_all, w_ada, b_ada.reshape(depth, 1, 3 * D_MODEL))


def _first_map_lanes(shape):
    return (lax.broadcasted_iota(jnp.int32, shape, 1) & 32) == 0


def _first_head_lanes(shape):
    return lax.broadcasted_iota(jnp.int32, shape, 1) < 64


def _rms_halves(xb, first):
    sq = xb * xb
    s_a = jnp.sum(jnp.where(first, sq, 0.0), axis=-1, keepdims=True)
    s_b = jnp.sum(jnp.where(first, 0.0, sq), axis=-1, keepdims=True)
    return lax.rsqrt(jnp.where(first, s_a, s_b) * (1.0 / 64.0) + EPS)


def _rms128(xb):
    return lax.rsqrt(jnp.mean(xb * xb, axis=-1, keepdims=True) + EPS)


def _rope(xb, cos, sin_signed):
    return xb * cos + pltpu.roll(xb, 64, 1) * sin_signed


def _tile_epilogue(name, xb, gain, tabs):
    ca_ref, sa_ref, cc_ref, sc_ref, ccr_ref, scr_ref = tabs
    if name in RESIDUE_MAJOR_TILES:
        cc_ref, sc_ref = ccr_ref, scr_ref
    if name == "AQ":
        rms = _rms_halves(xb, _first_map_lanes(xb.shape))
        return _rope(xb * (rms * gain), ca_ref[...], sa_ref[...]) * 0.125
    if name == "AK":
        rms = _rms_halves(xb, _first_map_lanes(xb.shape))
        return _rope(xb * (rms * gain), ca_ref[...], sa_ref[...])
    if name == "BQ":
        return xb * gain * (_rms_halves(xb, _first_head_lanes(xb.shape)) * 0.125)
    if name == "BK":
        return xb * gain * _rms_halves(xb, _first_head_lanes(xb.shape))
    if name[:2] in ("CQ", "CK"):
        return _rope(xb * gain, cc_ref[...], sc_ref[...]) * _rms128(xb)
    if name[0] == "Z":
        return xb * _sigmoid(xb)
    if name[0] == "G":
        return _sigmoid(xb)
    return xb


def _proj_kernel(x_ref, mod_ref, lng_ref, w_ref, gain_ref, ca_ref, sa_ref, cc_ref, sc_ref,
                 ccr_ref, scr_ref, o_ref, h_ref, hr_ref, hstage_ref, acc_ref):
    grp = pl.program_id(1)
    tm = h_ref.shape[0]
    run = tm // C_TOP_DIL

    @pl.when(grp == 0)
    def _():
        x = x_ref[...]
        ms = jnp.mean(x * x, axis=-1, keepdims=True)
        y = x * lax.rsqrt(ms + EPS) * lng_ref[...]
        mod = mod_ref[0]
        shift = mod[:, :D_MODEL]
        scale = mod[:, D_MODEL:2 * D_MODEL]
        h = y * (1.0 + scale) + shift
        h_ref[...] = h.astype(BF16)
        for cb in range(D_MODEL // LANES):
            cols = slice(cb * LANES, (cb + 1) * LANES)
            hstage_ref[cb] = h[:, cols]
            for rho in range(C_TOP_DIL):
                hr_ref[rho * run:(rho + 1) * run, cols] = hstage_ref[
                    cb, pl.ds(rho, run, stride=C_TOP_DIL), :].astype(BF16)

    tabs = (ca_ref, sa_ref, cc_ref, sc_ref, ccr_ref, scr_ref)
    for gi, names in enumerate(PROJ_GROUPS):
        @pl.when(grp == gi)
        def _(names=names):
            gain = gain_ref[0]
            for t, name in enumerate(names):
                acc = acc_ref.at[t % 2]
                lhs_ref = hr_ref if name in RESIDUE_MAJOR_TILES else h_ref
                acc[...] = jnp.dot(lhs_ref[...], w_ref[:, t * PROJ_TILE:(t + 1) * PROJ_TILE],
                                   preferred_element_type=F32)
                for cb in range(PROJ_TILE // LANES):
                    sl = slice(cb * LANES, (cb + 1) * LANES)
                    out = slice(t * PROJ_TILE + cb * LANES, t * PROJ_TILE + (cb + 1) * LANES)
                    o_ref[:, out] = _tile_epilogue(name, acc[:, sl], gain[:, out], tabs).astype(BF16)


def _projection(x2, mod3, ln_g, w_in_bf, gains, tabs, seq, tm):
    ntok = x2.shape[0]
    tps = seq // tm
    tab_spec = pl.BlockSpec((tm, LANES), lambda i, j: (i % tps, 0))
    return pl.pallas_call(
        _proj_kernel,
        grid=(ntok // tm, len(PROJ_GROUPS)),
        in_specs=[
            pl.BlockSpec((tm, D_MODEL), lambda i, j: (i, 0)),
            pl.BlockSpec((1, 1, 3 * D_MODEL), lambda i, j: (i // tps, 0, 0)),
            pl.BlockSpec((1, D_MODEL), lambda i, j: (0, 0)),
            pl.BlockSpec((D_MODEL, GROUP_WIDTH), lambda i, j: (0, j)),
            pl.BlockSpec((1, 1, GROUP_WIDTH), lambda i, j: (j, 0, 0)),
            *([tab_spec] * len(tabs)),
        ],
        out_specs=pl.BlockSpec((tm, GROUP_WIDTH), lambda i, j: (i, j)),
        out_shape=jax.ShapeDtypeStruct((ntok, IN_WIDTH), BF16),
        scratch_shapes=[pltpu.VMEM((tm, D_MODEL), BF16), pltpu.VMEM((tm, D_MODEL), BF16),
                        pltpu.VMEM((D_MODEL // LANES, tm, LANES), F32),
                        pltpu.VMEM((2, tm, PROJ_TILE), F32)],
        compiler_params=_cparams(("arbitrary", "arbitrary")),
        name="in_proj",
    )(x2, mod3, ln_g, w_in_bf, gains, *tabs)


def _diff_attn_kernel(q_ref, k_ref, v_ref, lq1_ref, lk1_ref, lq2_ref, lk2_ref, sub_ref, o_ref,
                      s_even_ref, s_odd_ref, *, lam_init):
    t = pl.program_id(0)
    dn = (((1,), (1,)), ((), ()))

    def score(s_ref):
        q = q_ref[...]
        k = k_ref[...]
        lo = _first_map_lanes(q.shape)
        zero = jnp.zeros_like(q)
        s_ref[0] = lax.dot_general(jnp.where(lo, q, zero), k, dn, preferred_element_type=F32)
        s_ref[1] = lax.dot_general(jnp.where(lo, zero, q), k, dn, preferred_element_type=F32)

    def finish(s_ref):
        lam = (jnp.exp(jnp.sum(lq1_ref[...] * lk1_ref[...], axis=-1, keepdims=True))
               - jnp.exp(jnp.sum(lq2_ref[...] * lk2_ref[...], axis=-1, keepdims=True)) + lam_init)
        s0 = s_ref[0]
        s1 = s_ref[1]
        e0 = jnp.exp(s0 - jnp.max(s0, axis=-1, keepdims=True))
        e1 = jnp.exp(s1 - jnp.max(s1, axis=-1, keepdims=True))
        l0 = jnp.sum(e0, axis=-1, keepdims=True)
        l1 = jnp.sum(e1, axis=-1, keepdims=True)
        w = e0 - e1 * (lam * l0 / l1)
        o = jnp.dot(w.astype(BF16), v_ref[...], preferred_element_type=F32) * (1.0 / l0)
        ms = jnp.mean(o * o, axis=-1, keepdims=True)
        o_ref[...] = (o * lax.rsqrt(ms + EPS) * sub_ref[...] * (1.0 - lam_init)).astype(BF16)

    @pl.when(t == 0)
    def _():
        s_odd_ref[...] = jnp.zeros_like(s_odd_ref)

    @pl.when(t % 2 == 0)
    def _():
        score(s_even_ref)
        finish(s_odd_ref)

    @pl.when(t % 2 == 1)
    def _():
        score(s_odd_ref)
        finish(s_even_ref)


def _diff_attention(proj3, lam_params, subln, lam_init, tq):
    nb, seq, _ = proj3.shape
    nq = seq // tq
    tiles = nb * A_HEADS * nq

    def split(t):
        return t // (A_HEADS * nq), (t // nq) % A_HEADS, t % nq

    def cur(t):
        return split(jnp.minimum(t, tiles - 1))

    def prev(t):
        return split(jnp.maximum(t - 1, 0))

    vec = pl.BlockSpec((1, A_QK_DIM), lambda t: (0, 0))
    return pl.pallas_call(
        functools.partial(_diff_attn_kernel, lam_init=lam_init),
        grid=(tiles + 1,),
        in_specs=[
            pl.BlockSpec((None, tq, LANES), lambda t: (cur(t)[0], cur(t)[2], _col_block("AQ") + cur(t)[1])),
            pl.BlockSpec((None, seq, LANES), lambda t: (cur(t)[0], 0, _col_block("AK") + cur(t)[1])),
            pl.BlockSpec((None, seq, LANES), lambda t: (prev(t)[0], 0, _col_block("AV") + prev(t)[1])),
            vec, vec, vec, vec,
            pl.BlockSpec((1, LANES), lambda t: (0, 0)),
        ],
        out_specs=pl.BlockSpec((None, tq, LANES), lambda t: (prev(t)[0], prev(t)[2], prev(t)[1])),
        out_shape=jax.ShapeDtypeStruct((nb, seq, BRANCH_WIDTH), BF16),
        scratch_shapes=[pltpu.VMEM((2, tq, seq), F32), pltpu.VMEM((2, tq, seq), F32)],
        compiler_params=_cparams(("arbitrary",)),
        name="diff_attn",
    )(proj3, proj3, proj3, *lam_params, subln)


def _na_bias_tables(rpb):
    rows = NA_QROWS + NA_KROWS
    qc = jnp.arange(GRID_W)
    kc = jnp.arange(GRID_W)
    cs = jnp.clip(qc - NA_KW // 2, 0, GRID_W - NA_KW)
    col_ok = (kc[None, :] >= cs[:, None]) & (kc[None, :] < cs[:, None] + NA_KW)
    col_off = jnp.clip(kc[None, :] - qc[:, None], -(NA_KW - 1), NA_KW - 1) + NA_KW - 1
    cols = jnp.where(col_ok, rpb.astype(F32)[:, :, col_off], NEG)
    kh = min(NA_KH, rows)
    tables = []
    for r0, ws in ((0, 0), (NA_QROWS, NA_QROWS - kh // 2), (rows - NA_QROWS, rows - NA_KROWS)):
        r = r0 + jnp.arange(NA_QROWS)
        kr = ws + jnp.arange(NA_KROWS)
        rs = jnp.clip(r - kh // 2, 0, rows - kh)
        row_ok = (kr[None, :] >= rs[:, None]) & (kr[None, :] < rs[:, None] + kh)
        row_off = jnp.clip(kr[None, :] - r[:, None] + NA_KH - 1, 0, 2 * NA_KH - 2)
        t = jnp.where(row_ok[None, :, :, None, None], cols[:, row_off], NEG)
        tables.append(t.transpose(0, 1, 3, 2, 4).reshape(
            B_HEADS, NA_QROWS * GRID_W, NA_KROWS * GRID_W))
    return jnp.stack(tables)


def _na_kernel(q_ref, k_ref, v_ref, bias_ref, o_ref, s_even_ref, s_odd_ref, *, rows, nrb):
    t = pl.program_id(0)
    nkeys = NA_KROWS * GRID_W
    dn = (((1,), (1,)), ((), ()))

    def key_start(i):
        ws = jnp.clip(i * NA_QROWS - NA_KH // 2, 0, rows - NA_KROWS)
        return pl.multiple_of(ws * GRID_W, GRID_W)

    def score(s_ref):
        i = jnp.minimum(t, pl.num_programs(0) - 2) % nrb
        kind = jnp.where(i == 0, 0, jnp.where(i == nrb - 1, 2, 1))
        kw = k_ref[pl.ds(key_start(i), nkeys), :]
        q = q_ref[...]
        lo = _first_head_lanes(q.shape)
        zero = jnp.zeros_like(q)
        for e in range(2):
            qe = jnp.where(lo, q, zero) if e == 0 else jnp.where(lo, zero, q)
            s_ref[e] = lax.dot_general(qe, kw, dn, preferred_element_type=F32) + bias_ref[kind, e]

    def finish(s_ref):
        i = jnp.maximum(t - 1, 0) % nrb
        vw = v_ref[pl.ds(key_start(i), nkeys), :]
        outs = []
        for e in range(2):
            s = s_ref[e]
            ex = jnp.exp(s - jnp.max(s, axis=-1, keepdims=True))
            p = ex * (1.0 / jnp.sum(ex, axis=-1, keepdims=True))
            outs.append(jnp.dot(p.astype(BF16), vw, preferred_element_type=F32))
        o_ref[...] = jnp.where(_first_head_lanes(outs[0].shape), outs[0], outs[1]).astype(BF16)

    @pl.when(t == 0)
    def _():
        s_odd_ref[...] = jnp.zeros_like(s_odd_ref)

    @pl.when(t % 2 == 0)
    def _():
        score(s_even_ref)
        finish(s_odd_ref)

    @pl.when(t % 2 == 1)
    def _():
        score(s_odd_ref)
        finish(s_even_ref)


def _neighborhood_attention(proj3, bias):
    nb, seq, _ = proj3.shape
    rows = seq // GRID_W
    nrb = rows // NA_QROWS
    nq = NA_QROWS * GRID_W
    nk = NA_KROWS * GRID_W
    tiles = (B_HEADS // 2) * nb * nrb

    def split(t):
        return t // (nb * nrb), (t // nrb) % nb, t % nrb

    def cur(t):
        return split(jnp.minimum(t, tiles - 1))

    def prev(t):
        return split(jnp.maximum(t - 1, 0))

    return pl.pallas_call(
        functools.partial(_na_kernel, rows=rows, nrb=nrb),
        grid=(tiles + 1,),
        in_specs=[
            pl.BlockSpec((None, nq, LANES), lambda t: (cur(t)[1], cur(t)[2], _col_block("BQ") + cur(t)[0])),
            pl.BlockSpec((None, seq, LANES), lambda t: (cur(t)[1], 0, _col_block("BK") + cur(t)[0])),
            pl.BlockSpec((None, seq, LANES), lambda t: (prev(t)[1], 0, _col_block("BV") + prev(t)[0])),
            pl.BlockSpec((3, 2, nq, nk), lambda t: (0, cur(t)[0], 0, 0)),
        ],
        out_specs=pl.BlockSpec((None, nq, LANES), lambda t: (prev(t)[1], prev(t)[2], prev(t)[0])),
        out_shape=jax.ShapeDtypeStruct((nb, seq, BRANCH_WIDTH), BF16),
        scratch_shapes=[pltpu.VMEM((2, nq, nk), F32), pltpu.VMEM((2, nq, nk), F32)],
        compiler_params=_cparams(("arbitrary",)),
        name="nbr_attn",
    )(proj3, proj3, proj3, bias)


def _softmax_av(s, v):
    m = jnp.max(s, axis=-1, keepdims=True)
    e = jnp.exp(s - m)
    l = jnp.sum(e, axis=-1, keepdims=True)
    acc = jnp.dot(e.astype(BF16), v, preferred_element_type=F32)
    return acc * (1.0 / l), jnp.broadcast_to(m + jnp.log(l), acc.shape)


def _dil_kernel(q0_ref, q1_ref, q2_ref, k0_ref, k1_ref, k2_ref, v0_ref, v1_ref, v2_ref, o_ref,
                o0_ref, l0_ref, o0r_ref, l0r_ref, o1_ref, l1_ref, o2_ref, l2_ref,
                band_ref, band4_ref, band16_ref, *, seq, tile):
    scale = C_HEAD_DIM ** -0.5
    dn = (((1,), (1,)), ((), ()))
    run = tile // C_TOP_DIL
    n_tiles = seq // tile
    sub16 = seq // C_TOP_DIL
    hw = C_HALF_WINDOW

    full_win = C_QBLK + 2 * hw
    qi = lax.broadcasted_iota(jnp.int32, (C_QBLK, full_win), 0)
    kj = lax.broadcasted_iota(jnp.int32, (C_QBLK, full_win), 1)
    for lead in range(3):
        band_ref[lead] = jnp.where(jnp.abs(kj - qi - lead * hw) <= hw, 0.0, NEG)
    pack = band16_ref.shape[0] // sub16
    qi = lax.broadcasted_iota(jnp.int32, band16_ref.shape, 0)
    kj = lax.broadcasted_iota(jnp.int32, band16_ref.shape, 1)
    same_class = (qi // sub16) == (kj // sub16)
    band16_ref[...] = jnp.where(same_class & (jnp.abs(kj - qi) <= hw), 0.0, NEG)
    qr = lax.broadcasted_iota(jnp.int32, (C_QBLK, 2 * C_QBLK), 0)
    kc = lax.broadcasted_iota(jnp.int32, (C_QBLK, 2 * C_QBLK), 1)
    q_pos = 4 * (qr % 32) + qr // 32
    k_pos = 4 * (kc % 64) + kc // 64
    for lead in range(3):
        band4_ref[lead] = jnp.where(jnp.abs(q_pos + 64 * lead - k_pos) <= hw, 0.0, NEG)

    def attend0(n):
        a = n * C_QBLK
        ws = jnp.clip(a - hw, 0, seq - full_win)
        rows = pl.ds(pl.multiple_of(a, C_QBLK), C_QBLK)
        kw = k0_ref[pl.ds(pl.multiple_of(ws, hw), full_win), :]
        vw = v0_ref[pl.ds(pl.multiple_of(ws, hw), full_win), :]
        s = lax.dot_general(q0_ref[rows, :], kw, dn, preferred_element_type=F32) * scale
        o0_ref[rows, :], l0_ref[rows, :] = _softmax_av(s + band_ref[(a - ws) // hw], vw)

    def blocks0(it, carry):
        for u in range(C_UNROLL):
            attend0(it * C_UNROLL + u)
        return carry

    lax.fori_loop(0, seq // (C_QBLK * C_UNROLL), blocks0, 0)

    def row_of(r16, j):
        return (j // run) * tile + r16 * run + j % run

    def attend1(n):
        rho4 = n // (sub16 // 32)
        j0 = (n % (sub16 // 32)) * 32
        jw = jnp.clip(j0 - 16, 0, sub16 - 64)
        q_rows = [pl.ds(pl.multiple_of(row_of(4 * c + rho4, j0), 32), 32) for c in range(4)]
        k_rows = [pl.ds(pl.multiple_of(row_of(4 * c + rho4, jw + 16 * p), 16), 16)
                  for c in range(4) for p in range(4)]
        q = jnp.concatenate([q1_ref[r, :] for r in q_rows], axis=0)
        kw = jnp.concatenate([k1_ref[r, :] for r in k_rows], axis=0)
        vw = jnp.concatenate([v1_ref[r, :] for r in k_rows], axis=0)
        s = lax.dot_general(q, kw, dn, preferred_element_type=F32) * scale
        o, lse = _softmax_av(s + band4_ref[(j0 - jw) // 16], vw)
        for c, r in enumerate(q_rows):
            o1_ref[r, :] = o[c * 32:(c + 1) * 32]
            l1_ref[r, :] = lse[c * 32:(c + 1) * 32]

    def blocks1(it, carry):
        for u in range(C_UNROLL):
            attend1(it * C_UNROLL + u)
        return carry

    lax.fori_loop(0, seq // (C_QBLK * C_UNROLL), blocks1, 0)

    def attend2(n):
        rows = [pl.ds(pl.multiple_of(t * tile + (n * pack + u) * run, run), run)
                for u in range(pack) for t in range(n_tiles)]
        q = jnp.concatenate([q2_ref[r, :] for r in rows], axis=0)
        kw = jnp.concatenate([k2_ref[r, :] for r in rows], axis=0)
        vw = jnp.concatenate([v2_ref[r, :] for r in rows], axis=0)
        s = lax.dot_general(q, kw, dn, preferred_element_type=F32) * scale
        o, lse = _softmax_av(s + band16_ref[...], vw)
        for i, r in enumerate(rows):
            o2_ref[r, :] = o[i * run:(i + 1) * run]
            l2_ref[r, :] = lse[i * run:(i + 1) * run]

    unroll2 = min(C_UNROLL, C_TOP_DIL // pack)

    def blocks2(it, carry):
        for u in range(unroll2):
            attend2(it * unroll2 + u)
        return carry

    lax.fori_loop(0, C_TOP_DIL // pack // unroll2, blocks2, 0)

    for t in range(n_tiles):
        for rho in range(C_TOP_DIL):
            dst = slice(t * tile + rho * run, t * tile + (rho + 1) * run)
            o0r_ref[dst, :] = o0_ref[pl.ds(t * tile + rho, run, stride=C_TOP_DIL), :]
            l0r_ref[dst, :] = l0_ref[pl.ds(t * tile + rho, run, stride=C_TOP_DIL), :]
    l0, l1, l2 = l0r_ref[...], l1_ref[...], l2_ref[...]
    l_max = jnp.maximum(jnp.maximum(l0, l1), l2)
    w0, w1, w2 = jnp.exp(l0 - l_max), jnp.exp(l1 - l_max), jnp.exp(l2 - l_max)
    o0_ref[...] = (w0 * o0r_ref[...] + w1 * o1_ref[...] + w2 * o2_ref[...]) * (1.0 / (w0 + w1 + w2))

    def to_token_order(j, carry):
        src = o0_ref[pl.ds((j // run) * tile + j % run, C_TOP_DIL, stride=run), :]
        o_ref[pl.ds(pl.multiple_of(j * C_TOP_DIL, C_TOP_DIL), C_TOP_DIL), :] = src.astype(BF16)
        return carry

    lax.fori_loop(0, sub16, to_token_order, 0, unroll=8)


def _dilated_mixture(proj3, tile):
    nb, seq, _ = proj3.shape
    assert seq % tile == 0 and tile % (C_TOP_DIL * 32) == 0 and seq >= C_QBLK + 2 * C_HALF_WINDOW
    assert (seq // C_TOP_DIL) % 64 == 0 and seq % (C_QBLK * C_UNROLL) == 0
    side2 = max(seq // C_TOP_DIL, C_QBLK + 2 * C_HALF_WINDOW)

    def spec(kind, g):
        cb0 = _col_block(f"{kind}{g}")
        return pl.BlockSpec((None, seq, LANES), lambda b, h, cb0=cb0: (b, 0, cb0 + h))

    in_specs = [spec(kind, g) for kind in ("CQ", "CK", "CV") for g in range(len(C_DILATIONS))]
    return pl.pallas_call(
        functools.partial(_dil_kernel, seq=seq, tile=tile),
        grid=(nb, C_HEADS_PER_GROUP),
        in_specs=in_specs,
        out_specs=pl.BlockSpec((None, seq, LANES), lambda b, h: (b, 0, h)),
        out_shape=jax.ShapeDtypeStruct((nb, seq, BRANCH_WIDTH), BF16),
        scratch_shapes=(
            [pltpu.VMEM((seq, LANES), F32)] * 8
            + [pltpu.VMEM((3, C_QBLK, C_QBLK + 2 * C_HALF_WINDOW), F32)] * 2
            + [pltpu.VMEM((side2, side2), F32)]),
        compiler_params=_cparams(("arbitrary", "arbitrary")),
        name="dil_attn",
    )(*([proj3] * 9))


def _tail_kernel(ya_ref, yb_ref, yc_ref, z0_ref, z1_ref, z2_ref, g0a_ref, g0b_ref, g1a_ref,
                 g1b_ref, g2a_ref, g2b_ref, x_ref, gate_ref, wbr_ref, wout_ref, o_ref):
    halves = [None, None]
    branches = ((ya_ref, z0_ref, (g0a_ref, g0b_ref)), (yb_ref, z1_ref, (g1a_ref, g1b_ref)),
                (yc_ref, z2_ref, (g2a_ref, g2b_ref)))
    for i, (y_ref, z_ref, g_refs) in enumerate(branches):
        yz = (y_ref[...].astype(F32) * z_ref[...].astype(F32)).astype(BF16)
        for half, g_ref in enumerate(g_refs):
            cols = slice(half * PROJ_TILE, (half + 1) * PROJ_TILE)
            u = g_ref[...].astype(F32) * jnp.dot(yz, wbr_ref[i, :, cols], preferred_element_type=F32)
            halves[half] = u if halves[half] is None else halves[half] + u
    merged = jnp.concatenate(halves, axis=1).astype(BF16)
    out = jnp.dot(merged, wout_ref[...], preferred_element_type=F32)
    o_ref[...] = x_ref[...] + gate_ref[0] * out


def _tail(ya, yb, yc, proj2, x2, mod3, w_br_bf, w_out_bf, seq, tm):
    ntok = x2.shape[0]
    tps = seq // tm
    y_spec = pl.BlockSpec((None, tm, BRANCH_WIDTH), lambda i: (i // tps, i % tps, 0))

    def tile_spec(name):
        return pl.BlockSpec((tm, PROJ_TILE), lambda i, s=SLOT[name]: (i, s))

    z_specs = [tile_spec(f"Z{t}") for t in range(3)]
    g_specs = [tile_spec(f"G{t}{half}") for t in range(3) for half in "ab"]
    return pl.pallas_call(
        _tail_kernel,
        grid=(ntok // tm,),
        in_specs=[y_spec, y_spec, y_spec, *z_specs, *g_specs,
                  pl.BlockSpec((tm, D_MODEL), lambda i: (i, 0)),
                  pl.BlockSpec((1, 1, D_MODEL), lambda i: (i // tps, 0, 2)),
                  pl.BlockSpec((3, BRANCH_WIDTH, D_MODEL), lambda i: (0, 0, 0)),
                  pl.BlockSpec((D_MODEL, D_MODEL), lambda i: (0, 0))],
        out_specs=pl.BlockSpec((tm, D_MODEL), lambda i: (i, 0)),
        out_shape=jax.ShapeDtypeStruct((ntok, D_MODEL), F32),
        compiler_params=_cparams(("arbitrary",)),
        name="tail",
    )(ya, yb, yc, *([proj2] * 9), x2, mod3, w_br_bf, w_out_bf)


def _rope_tables(seq, tile):
    def base(d):
        inv = ROPE_THETA ** (-jnp.arange(0, d, 2, dtype=F32) / d)
        ang = jnp.arange(seq, dtype=F32)[:, None] * inv[None, :]
        return jnp.cos(ang), jnp.sin(ang)

    cos_a, sin_a = base(A_QK_DIM)
    ca = jnp.tile(cos_a, (1, 4))
    sa = jnp.concatenate([-sin_a, -sin_a, sin_a, sin_a], axis=-1)
    cos_c, sin_c = base(C_HEAD_DIM)
    cc = jnp.tile(cos_c, (1, 2))
    sc = jnp.concatenate([-sin_c, sin_c], axis=-1)

    def residue_major(tab):
        t4 = tab.reshape(seq // tile, tile // C_TOP_DIL, C_TOP_DIL, LANES)
        return t4.transpose(0, 2, 1, 3).reshape(seq, LANES)

    return ca, sa, cc, sc, residue_major(cc), residue_major(sc)


def _to_stored_columns(w):
    lead = w.shape[:-1]
    tiles = {name: w[..., i * PROJ_TILE:(i + 1) * PROJ_TILE] for i, name in enumerate(TILE_NAMES)}
    for name in ("AQ", "AK"):
        t = tiles[name].reshape(*lead, A_HEADS, 2, 2, A_QK_DIM // 2)
        tiles[name] = jnp.swapaxes(t, -3, -2).reshape(*lead, PROJ_TILE)
    return jnp.concatenate([tiles[n] for grp in PROJ_GROUPS for n in grp], axis=-1)


def _proj_gains(qn_a, kn_a, qn_b, kn_b, qn_c, kn_c):
    per_tile = {"AQ": jnp.tile(qn_a, PROJ_TILE // A_QK_DIM), "AK": jnp.tile(kn_a, PROJ_TILE // A_QK_DIM),
                "BQ": jnp.tile(qn_b, PROJ_TILE // B_HEAD_DIM), "BK": jnp.tile(kn_b, PROJ_TILE // B_HEAD_DIM)}
    for g in range(len(C_DILATIONS)):
        per_tile[f"CQ{g}"] = jnp.tile(qn_c, PROJ_TILE // C_HEAD_DIM)
        per_tile[f"CK{g}"] = jnp.tile(kn_c, PROJ_TILE // C_HEAD_DIM)
    ones = jnp.ones((PROJ_TILE,), F32)
    full = jnp.concatenate([per_tile.get(name, ones).astype(F32) for name in TILE_NAMES])
    return _to_stored_columns(full).reshape(len(PROJ_GROUPS), 1, GROUP_WIDTH)


def _proj_token_tile(seq):
    return min(1024, seq)


def _tail_token_tile(seq):
    return min(512, seq)


def _encoder_layer(x, mod3, layer_idx, tabs, ln_g, w_in_bf, gains, lam_params, subln, na_bias,
                   w_br_bf, w_out_bf):
    nb, seq, _ = x.shape
    x2 = x.reshape(nb * seq, D_MODEL)
    proj2 = _projection(x2, mod3, ln_g, w_in_bf, gains, tabs, seq, _proj_token_tile(seq))
    proj3 = proj2.reshape(nb, seq, IN_WIDTH)
    lam_init = 0.8 - 0.6 * math.exp(-0.3 * layer_idx)
    tq = A_SCORE_BYTES // (2 * 2 * seq * 4)
    ya = _diff_attention(proj3, lam_params, subln, lam_init, tq=tq)
    yb = _neighborhood_attention(proj3, na_bias)
    yc = _dilated_mixture(proj3, _proj_token_tile(seq))
    y2 = _tail(ya, yb, yc, proj2, x2, mod3, w_br_bf, w_out_bf, seq, _tail_token_tile(seq))
    return y2.reshape(nb, seq, D_MODEL)


def kernel(x_prompt, x_sample, c_prompt, c_sample, ln_g, w_ada, b_ada, w_in, qn_a, kn_a, lam_q1, lam_k1, lam_q2, lam_k2, subln_a, qn_b, kn_b, rpb_b, qn_c, kn_c, w_br, w_out):
    depth = w_in.shape[0]
    n_prompt = c_prompt.shape[0]
    mod_all = _modulation(jnp.concatenate([c_prompt, c_sample], axis=0), w_ada, b_ada)
    w_in_bf = _to_stored_columns(w_in.astype(BF16))
    w_br_bf = w_br.astype(BF16)
    w_out_bf = w_out.astype(BF16)
    gains = [_proj_gains(qn_a[l], kn_a[l], qn_b[l], kn_b[l], qn_c[l], kn_c[l]) for l in range(depth)]
    na_bias = [_na_bias_tables(rpb_b[l]) for l in range(depth)]

    def run(x, mod):
        nb, seq, _ = x.shape
        assert seq % (NA_QROWS * GRID_W) == 0 and seq // GRID_W >= NA_KROWS
        tabs = _rope_tables(seq, _proj_token_tile(seq))
        for l in range(depth):
            lam_params = tuple(p[l].reshape(1, A_QK_DIM) for p in (lam_q1, lam_k1, lam_q2, lam_k2))
            x = _encoder_layer(x, mod[l].reshape(nb, 1, 3 * D_MODEL), l, tabs,
                               ln_g[l].reshape(1, D_MODEL), w_in_bf[l], gains[l], lam_params,
                               subln_a[l].reshape(1, LANES), na_bias[l], w_br_bf[l], w_out_bf[l])
        return x

    y_prompt = run(x_prompt, mod_all[:, :n_prompt])
    y_sample = run(x_sample, mod_all[:, n_prompt:])
    return (y_prompt, y_sample)
```

```python
import functools
import math

import jax
import jax.numpy as jnp
from jax import lax
from jax.experimental import pallas as pl
from jax.experimental.pallas import tpu as pltpu

F32 = jnp.float32
BF16 = jnp.bfloat16

D_MODEL = 1024
GRID_W = 64
BRANCH_WIDTH = 512
ROPE_THETA = 10000.0
EPS = 1e-6
NEG = -1e30
A_QK_DIM = 64
A_HEADS = 4
B_HEAD_DIM = 64
B_HEADS = 8
NA_KH = 8
NA_KW = 16
C_DILATIONS = (1, 4, 16)
C_HALF_WINDOW = 64
C_HEAD_DIM = 128
C_HEADS_PER_GROUP = 4
IN_WIDTH = 12288
LANES = 128

PROJ_TILE = 512
TILE_NAMES = ("AQ", "AK", "AV", "BQ", "BK", "BV", "CQ0", "CQ1", "CQ2", "CK0", "CK1", "CK2",
              "CV0", "CV1", "CV2", "Z0", "Z1", "Z2", "G0a", "G0b", "G1a", "G1b", "G2a", "G2b")
PROJ_GROUPS = (("AQ", "Z0", "BQ", "AV"), ("AK", "Z1", "BK", "BV"),
               ("CQ0", "G0a", "CK0", "CV0"), ("CQ1", "G0b", "CK1", "CV1"),
               ("CQ2", "G1a", "CK2", "CV2"), ("Z2", "G1b", "G2a", "G2b"))
C_TOP_DIL = C_DILATIONS[-1]
RESIDUE_MAJOR_TILES = tuple(f"C{kind}{g}" for kind in "QKV" for g in (1, 2))
GROUP_TILES = len(PROJ_GROUPS[0])
GROUP_WIDTH = GROUP_TILES * PROJ_TILE
SLOT = {name: i for i, name in enumerate(n for grp in PROJ_GROUPS for n in grp)}


def _col_block(name, sub=0):
    return SLOT[name] * (PROJ_TILE // LANES) + sub

VMEM_LIMIT = 56 * 1024 * 1024

NA_QROWS = 8
NA_KROWS = 16
A_SCORE_BYTES = 16 * 1024 * 1024
C_QBLK = 128
C_UNROLL = 8


def _cparams(sem):
    return pltpu.CompilerParams(dimension_semantics=sem, vmem_limit_bytes=VMEM_LIMIT)


def _sigmoid(x):
    return 1.0 / (1.0 + jnp.exp(-x))


def _mod_kernel(c_ref, w_ref, b_ref, o_ref):
    c = c_ref[...]
    o_ref[0] = jnp.dot(c * _sigmoid(c), w_ref[0], preferred_element_type=F32) + b_ref[0]


def _modulation(c_all, w_ada, b_ada):
    depth = w_ada.shape[0]
    nb = c_all.shape[0]
    return pl.pallas_call(
        _mod_kernel,
        grid=(depth, 3),
        in_specs=[
            pl.BlockSpec((nb, D_MODEL), lambda l, j: (0, 0)),
            pl.BlockSpec((1, D_MODEL, D_MODEL), lambda l, j: (l, 0, j)),
            pl.BlockSpec((1, 1, D_MODEL), lambda l, j: (l, 0, j)),
        ],
        out_specs=pl.BlockSpec((1, nb, D_MODEL), lambda l, j: (l, 0, j)),
        out_shape=jax.ShapeDtypeStruct((depth, nb, 3 * D_MODEL), F32),
        compiler_params=_cparams(("arbitrary", "arbitrary")),
        name="adaln_mod",
    )(c_all, w_ada, b_ada.reshape(depth, 1, 3 * D_MODEL))


def _first_map_lanes(shape):
    return (lax.broadcasted_iota(jnp.int32, shape, 1) & 32) == 0


def _first_head_lanes(shape):
    return lax.broadcasted_iota(jnp.int32, shape, 1) < 64


def _rms_halves(xb, first):
    sq = xb * xb
    s_a = jnp.sum(jnp.where(first, sq, 0.0), axis=-1, keepdims=True)
    s_b = jnp.sum(jnp.where(first, 0.0, sq), axis=-1, keepdims=True)
    return lax.rsqrt(jnp.where(first, s_a, s_b) * (1.0 / 64.0) + EPS)


def _rms128(xb):
    return lax.rsqrt(jnp.mean(xb * xb, axis=-1, keepdims=True) + EPS)


def _rope(xb, cos, sin_signed):
    return xb * cos + pltpu.roll(xb, 64, 1) * sin_signed


def _tile_epilogue(name, xb, gain, tabs):
    ca_ref, sa_ref, cc_ref, sc_ref, ccr_ref, scr_ref = tabs
    if name in RESIDUE_MAJOR_TILES:
        cc_ref, sc_ref = ccr_ref, scr_ref
    if name == "AQ":
        rms = _rms_halves(xb, _first_map_lanes(xb.shape))
        return _rope(xb * (rms * gain), ca_ref[...], sa_ref[...]) * 0.125
    if name == "AK":
        rms = _rms_halves(xb, _first_map_lanes(xb.shape))
        return _rope(xb * (rms * gain), ca_ref[...], sa_ref[...])
    if name == "BQ":
        return xb * gain * (_rms_halves(xb, _first_head_lanes(xb.shape)) * 0.125)
    if name == "BK":
        return xb * gain * _rms_halves(xb, _first_head_lanes(xb.shape))
    if name[:2] in ("CQ", "CK"):
        return _rope(xb * gain, cc_ref[...], sc_ref[...]) * _rms128(xb)
    if name[0] == "Z":
        return xb * _sigmoid(xb)
    if name[0] == "G":
        return _sigmoid(xb)
    return xb


def _proj_kernel(x_ref, mod_ref, lng_ref, w_ref, gain_ref, ca_ref, sa_ref, cc_ref, sc_ref,
                 ccr_ref, scr_ref, o_ref, h_ref, hr_ref, hstage_ref, acc_ref):
    grp = pl.program_id(1)
    tm = h_ref.shape[0]
    run = tm // C_TOP_DIL

    @pl.when(grp == 0)
    def _():
        x = x_ref[...]
        ms = jnp.mean(x * x, axis=-1, keepdims=True)
        y = x * lax.rsqrt(ms + EPS) * lng_ref[...]
        mod = mod_ref[0]
        shift = mod[:, :D_MODEL]
        scale = mod[:, D_MODEL:2 * D_MODEL]
        h = y * (1.0 + scale) + shift
        h_ref[...] = h.astype(BF16)
        for cb in range(D_MODEL // LANES):
            cols = slice(cb * LANES, (cb + 1) * LANES)
            hstage_ref[cb] = h[:, cols]
            for rho in range(C_TOP_DIL):
                hr_ref[rho * run:(rho + 1) * run, cols] = hstage_ref[
                    cb, pl.ds(rho, run, stride=C_TOP_DIL), :].astype(BF16)

    tabs = (ca_ref, sa_ref, cc_ref, sc_ref, ccr_ref, scr_ref)
    for gi, names in enumerate(PROJ_GROUPS):
        @pl.when(grp == gi)
        def _(names=names):
            gain = gain_ref[0]
            for t, name in enumerate(names):
                acc = acc_ref.at[t % 2]
                lhs_ref = hr_ref if name in RESIDUE_MAJOR_TILES else h_ref
                acc[...] = jnp.dot(lhs_ref[...], w_ref[:, t * PROJ_TILE:(t + 1) * PROJ_TILE],
                                   preferred_element_type=F32)
                for cb in range(PROJ_TILE // LANES):
                    sl = slice(cb * LANES, (cb + 1) * LANES)
                    out = slice(t * PROJ_TILE + cb * LANES, t * PROJ_TILE + (cb + 1) * LANES)
                    o_ref[:, out] = _tile_epilogue(name, acc[:, sl], gain[:, out], tabs).astype(BF16)


def _projection(x2, mod3, ln_g, w_in_bf, gains, tabs, seq, tm):
    ntok = x2.shape[0]
    tps = seq // tm
    tab_spec = pl.BlockSpec((tm, LANES), lambda i, j: (i % tps, 0))
    return pl.pallas_call(
        _proj_kernel,
        grid=(ntok // tm, len(PROJ_GROUPS)),
        in_specs=[
            pl.BlockSpec((tm, D_MODEL), lambda i, j: (i, 0)),
            pl.BlockSpec((1, 1, 3 * D_MODEL), lambda i, j: (i // tps, 0, 0)),
            pl.BlockSpec((1, D_MODEL), lambda i, j: (0, 0)),
            pl.BlockSpec((D_MODEL, GROUP_WIDTH), lambda i, j: (0, j)),
            pl.BlockSpec((1, 1, GROUP_WIDTH), lambda i, j: (j, 0, 0)),
            *([tab_spec] * len(tabs)),
        ],
        out_specs=pl.BlockSpec((tm, GROUP_WIDTH), lambda i, j: (i, j)),
        out_shape=jax.ShapeDtypeStruct((ntok, IN_WIDTH), BF16),
        scratch_shapes=[pltpu.VMEM((tm, D_MODEL), BF16), pltpu.VMEM((tm, D_MODEL), BF16),
                        pltpu.VMEM((D_MODEL // LANES, tm, LANES), F32),
                        pltpu.VMEM((2, tm, PROJ_TILE), F32)],
        compiler_params=_cparams(("arbitrary", "arbitrary")),
        name="in_proj",
    )(x2, mod3, ln_g, w_in_bf, gains, *tabs)


def _diff_attn_kernel(q_ref, k_ref, v_ref, lq1_ref, lk1_ref, lq2_ref, lk2_ref, sub_ref, o_ref,
                      s_even_ref, s_odd_ref, *, lam_init):
    t = pl.program_id(0)
    dn = (((1,), (1,)), ((), ()))

    def score(s_ref):
        q = q_ref[...]
        k = k_ref[...]
        lo = _first_map_lanes(q.shape)
        zero = jnp.zeros_like(q)
        s_ref[0] = lax.dot_general(jnp.where(lo, q, zero), k, dn, preferred_element_type=F32)
        s_ref[1] = lax.dot_general(jnp.where(lo, zero, q), k, dn, preferred_element_type=F32)

    def finish(s_ref):
        lam = (jnp.exp(jnp.sum(lq1_ref[...] * lk1_ref[...], axis=-1, keepdims=True))
               - jnp.exp(jnp.sum(lq2_ref[...] * lk2_ref[...], axis=-1, keepdims=True)) + lam_init)
        v = v_ref[...]
        v_ones = jnp.concatenate([v, jnp.ones_like(v)], axis=1)
        prods = []
        for m in range(2):
            s = s_ref[m]
            e = jnp.exp(s - jnp.max(s, axis=-1, keepdims=True))
            prods.append(jnp.dot(e.astype(BF16), v_ones, preferred_element_type=F32))
        (a0, a1) = prods
        o = a0[:, :LANES] / a0[:, LANES:] - lam * (a1[:, :LANES] / a1[:, LANES:])
        ms = jnp.mean(o * o, axis=-1, keepdims=True)
        o_ref[...] = (o * lax.rsqrt(ms + EPS) * sub_ref[...] * (1.0 - lam_init)).astype(BF16)

    @pl.when(t == 0)
    def _():
        s_odd_ref[...] = jnp.zeros_like(s_odd_ref)

    @pl.when(t % 2 == 0)
    def _():
        score(s_even_ref)
        finish(s_odd_ref)

    @pl.when(t % 2 == 1)
    def _():
        score(s_odd_ref)
        finish(s_even_ref)


def _diff_attention(proj3, lam_params, subln, lam_init, tq):
    nb, seq, _ = proj3.shape
    nq = seq // tq
    tiles = nb * A_HEADS * nq

    def split(t):
        return t // (A_HEADS * nq), (t // nq) % A_HEADS, t % nq

    def cur(t):
        return split(jnp.minimum(t, tiles - 1))

    def prev(t):
        return split(jnp.maximum(t - 1, 0))

    vec = pl.BlockSpec((1, A_QK_DIM), lambda t: (0, 0))
    return pl.pallas_call(
        functools.partial(_diff_attn_kernel, lam_init=lam_init),
        grid=(tiles + 1,),
        in_specs=[
            pl.BlockSpec((None, tq, LANES), lambda t: (cur(t)[0], cur(t)[2], _col_block("AQ") + cur(t)[1])),
            pl.BlockSpec((None, seq, LANES), lambda t: (cur(t)[0], 0, _col_block("AK") + cur(t)[1])),
            pl.BlockSpec((None, seq, LANES), lambda t: (prev(t)[0], 0, _col_block("AV") + prev(t)[1])),
            vec, vec, vec, vec,
            pl.BlockSpec((1, LANES), lambda t: (0, 0)),
        ],
        out_specs=pl.BlockSpec((None, tq, LANES), lambda t: (prev(t)[0], prev(t)[2], prev(t)[1])),
        out_shape=jax.ShapeDtypeStruct((nb, seq, BRANCH_WIDTH), BF16),
        scratch_shapes=[pltpu.VMEM((2, tq, seq), F32), pltpu.VMEM((2, tq, seq), F32)],
        compiler_params=_cparams(("arbitrary",)),
        name="diff_attn",
    )(proj3, proj3, proj3, *lam_params, subln)


def _na_bias_tables(rpb):
    rows = NA_QROWS + NA_KROWS
    qc = jnp.arange(GRID_W)
    kc = jnp.arange(GRID_W)
    cs = jnp.clip(qc - NA_KW // 2, 0, GRID_W - NA_KW)
    col_ok = (kc[None, :] >= cs[:, None]) & (kc[None, :] < cs[:, None] + NA_KW)
    col_off = jnp.clip(kc[None, :] - qc[:, None], -(NA_KW - 1), NA_KW - 1) + NA_KW - 1
    cols = jnp.where(col_ok, rpb.astype(F32)[:, :, col_off], NEG)
    kh = min(NA_KH, rows)
    tables = []
    for r0, ws in ((0, 0), (NA_QROWS, NA_QROWS - kh // 2), (rows - NA_QROWS, rows - NA_KROWS)):
        r = r0 + jnp.arange(NA_QROWS)
        kr = ws + jnp.arange(NA_KROWS)
        rs = jnp.clip(r - kh // 2, 0, rows - kh)
        row_ok = (kr[None, :] >= rs[:, None]) & (kr[None, :] < rs[:, None] + kh)
        row_off = jnp.clip(kr[None, :] - r[:, None] + NA_KH - 1, 0, 2 * NA_KH - 2)
        t = jnp.where(row_ok[None, :, :, None, None], cols[:, row_off], NEG)
        tables.append(t.transpose(0, 1, 3, 2, 4).reshape(
            B_HEADS, NA_QROWS * GRID_W, NA_KROWS * GRID_W))
    return jnp.stack(tables)


def _na_kernel(q_ref, k_ref, v_ref, bias_ref, o_ref, s_even_ref, s_odd_ref, *, rows, nrb):
    t = pl.program_id(0)
    nkeys = NA_KROWS * GRID_W
    dn = (((1,), (1,)), ((), ()))

    def key_start(i):
        ws = jnp.clip(i * NA_QROWS - NA_KH // 2, 0, rows - NA_KROWS)
        return pl.multiple_of(ws * GRID_W, GRID_W)

    def score(s_ref):
        i = jnp.minimum(t, pl.num_programs(0) - 2) % nrb
        kind = jnp.where(i == 0, 0, jnp.where(i == nrb - 1, 2, 1))
        kw = k_ref[pl.ds(key_start(i), nkeys), :]
        q = q_ref[...]
        lo = _first_head_lanes(q.shape)
        zero = jnp.zeros_like(q)
        for e in range(2):
            qe = jnp.where(lo, q, zero) if e == 0 else jnp.where(lo, zero, q)
            s_ref[e] = lax.dot_general(qe, kw, dn, preferred_element_type=F32) + bias_ref[kind, e]

    def finish(s_ref):
        i = jnp.maximum(t - 1, 0) % nrb
        vw = v_ref[pl.ds(key_start(i), nkeys), :]
        v_ones = jnp.concatenate([vw, jnp.ones_like(vw)], axis=1)
        outs = []
        for e in range(2):
            s = s_ref[e]
            ex = jnp.exp(s - jnp.max(s, axis=-1, keepdims=True))
            a = jnp.dot(ex.astype(BF16), v_ones, preferred_element_type=F32)
            outs.append(a[:, :LANES] / a[:, LANES:])
        o_ref[...] = jnp.where(_first_head_lanes(outs[0].shape), outs[0], outs[1]).astype(BF16)

    @pl.when(t == 0)
    def _():
        s_odd_ref[...] = jnp.zeros_like(s_odd_ref)

    @pl.when(t % 2 == 0)
    def _():
        score(s_even_ref)
        finish(s_odd_ref)

    @pl.when(t % 2 == 1)
    def _():
        score(s_odd_ref)
        finish(s_even_ref)


def _neighborhood_attention(proj3, bias):
    nb, seq, _ = proj3.shape
    rows = seq // GRID_W
    nrb = rows // NA_QROWS
    nq = NA_QROWS * GRID_W
    nk = NA_KROWS * GRID_W
    tiles = (B_HEADS // 2) * nb * nrb

    def split(t):
        return t // (nb * nrb), (t // nrb) % nb, t % nrb

    def cur(t):
        return split(jnp.minimum(t, tiles - 1))

    def prev(t):
        return split(jnp.maximum(t - 1, 0))

    return pl.pallas_call(
        functools.partial(_na_kernel, rows=rows, nrb=nrb),
        grid=(tiles + 1,),
        in_specs=[
            pl.BlockSpec((None, nq, LANES), lambda t: (cur(t)[1], cur(t)[2], _col_block("BQ") + cur(t)[0])),
            pl.BlockSpec((None, seq, LANES), lambda t: (cur(t)[1], 0, _col_block("BK") + cur(t)[0])),
            pl.BlockSpec((None, seq, LANES), lambda t: (prev(t)[1], 0, _col_block("BV") + prev(t)[0])),
            pl.BlockSpec((3, 2, nq, nk), lambda t: (0, cur(t)[0], 0, 0)),
        ],
        out_specs=pl.BlockSpec((None, nq, LANES), lambda t: (prev(t)[1], prev(t)[2], prev(t)[0])),
        out_shape=jax.ShapeDtypeStruct((nb, seq, BRANCH_WIDTH), BF16),
        scratch_shapes=[pltpu.VMEM((2, nq, nk), F32), pltpu.VMEM((2, nq, nk), F32)],
        compiler_params=_cparams(("arbitrary",)),
        name="nbr_attn",
    )(proj3, proj3, proj3, bias)


def _softmax_av(s, v):
    m = jnp.max(s, axis=-1, keepdims=True)
    e = jnp.exp(s - m)
    v_ones = jnp.concatenate([v, jnp.ones_like(v)], axis=1)
    acc = jnp.dot(e.astype(BF16), v_ones, preferred_element_type=F32)
    l = acc[:, LANES:]
    return acc[:, :LANES] / l, m + jnp.log(l)


def _dil_kernel(q0_ref, q1_ref, q2_ref, k0_ref, k1_ref, k2_ref, v0_ref, v1_ref, v2_ref, o_ref,
                o0_ref, l0_ref, o0r_ref, l0r_ref, o1_ref, l1_ref, o2_ref, l2_ref,
                band_ref, band4_ref, band16_ref, *, seq, tile):
    scale = C_HEAD_DIM ** -0.5
    dn = (((1,), (1,)), ((), ()))
    run = tile // C_TOP_DIL
    n_tiles = seq // tile
    sub16 = seq // C_TOP_DIL
    hw = C_HALF_WINDOW

    full_win = C_QBLK + 2 * hw
    qi = lax.broadcasted_iota(jnp.int32, (C_QBLK, full_win), 0)
    kj = lax.broadcasted_iota(jnp.int32, (C_QBLK, full_win), 1)
    for lead in range(3):
        band_ref[lead] = jnp.where(jnp.abs(kj - qi - lead * hw) <= hw, 0.0, NEG)
    pack = band16_ref.shape[0] // sub16
    qi = lax.broadcasted_iota(jnp.int32, band16_ref.shape, 0)
    kj = lax.broadcasted_iota(jnp.int32, band16_ref.shape, 1)
    same_class = (qi // sub16) == (kj // sub16)
    band16_ref[...] = jnp.where(same_class & (jnp.abs(kj - qi) <= hw), 0.0, NEG)
    qr = lax.broadcasted_iota(jnp.int32, (C_QBLK, 2 * C_QBLK), 0)
    kc = lax.broadcasted_iota(jnp.int32, (C_QBLK, 2 * C_QBLK), 1)
    q_pos = 4 * (qr % 32) + qr // 32
    k_pos = 4 * (kc % 64) + kc // 64
    for lead in range(3):
        band4_ref[lead] = jnp.where(jnp.abs(q_pos + 64 * lead - k_pos) <= hw, 0.0, NEG)

    def attend0(n):
        a = n * C_QBLK
        ws = jnp.clip(a - hw, 0, seq - full_win)
        rows = pl.ds(pl.multiple_of(a, C_QBLK), C_QBLK)
        kw = k0_ref[pl.ds(pl.multiple_of(ws, hw), full_win), :]
        vw = v0_ref[pl.ds(pl.multiple_of(ws, hw), full_win), :]
        s = lax.dot_general(q0_ref[rows, :], kw, dn, preferred_element_type=F32) * scale
        o0_ref[rows, :], l0_ref[rows, :] = _softmax_av(s + band_ref[(a - ws) // hw], vw)

    def blocks0(it, carry):
        for u in range(C_UNROLL):
            attend0(it * C_UNROLL + u)
        return carry

    lax.fori_loop(0, seq // (C_QBLK * C_UNROLL), blocks0, 0)

    def row_of(r16, j):
        return (j // run) * tile + r16 * run + j % run

    def attend1(n):
        rho4 = n // (sub16 // 32)
        j0 = (n % (sub16 // 32)) * 32
        jw = jnp.clip(j0 - 16, 0, sub16 - 64)
        q_rows = [pl.ds(pl.multiple_of(row_of(4 * c + rho4, j0), 32), 32) for c in range(4)]
        k_rows = [pl.ds(pl.multiple_of(row_of(4 * c + rho4, jw + 16 * p), 16), 16)
                  for c in range(4) for p in range(4)]
        q = jnp.concatenate([q1_ref[r, :] for r in q_rows], axis=0)
        kw = jnp.concatenate([k1_ref[r, :] for r in k_rows], axis=0)
        vw = jnp.concatenate([v1_ref[r, :] for r in k_rows], axis=0)
        s = lax.dot_general(q, kw, dn, preferred_element_type=F32) * scale
        o, lse = _softmax_av(s + band4_ref[(j0 - jw) // 16], vw)
        for c, r in enumerate(q_rows):
            o1_ref[r, :] = o[c * 32:(c + 1) * 32]
            l1_ref[r, :] = lse[c * 32:(c + 1) * 32]

    def blocks1(it, carry):
        for u in range(C_UNROLL):
            attend1(it * C_UNROLL + u)
        return carry

    lax.fori_loop(0, seq // (C_QBLK * C_UNROLL), blocks1, 0)

    def attend2(n):
        rows = [pl.ds(pl.multiple_of(t * tile + (n * pack + u) * run, run), run)
                for u in range(pack) for t in range(n_tiles)]
        q = jnp.concatenate([q2_ref[r, :] for r in rows], axis=0)
        kw = jnp.concatenate([k2_ref[r, :] for r in rows], axis=0)
        vw = jnp.concatenate([v2_ref[r, :] for r in rows], axis=0)
        s = lax.dot_general(q, kw, dn, preferred_element_type=F32) * scale
        o, lse = _softmax_av(s + band16_ref[...], vw)
        for i, r in enumerate(rows):
            o2_ref[r, :] = o[i * run:(i + 1) * run]
            l2_ref[r, :] = lse[i * run:(i + 1) * run]

    unroll2 = min(C_UNROLL, C_TOP_DIL // pack)

    def blocks2(it, carry):
        for u in range(unroll2):
            attend2(it * unroll2 + u)
        return carry

    lax.fori_loop(0, C_TOP_DIL // pack // unroll2, blocks2, 0)

    for t in range(n_tiles):
        for rho in range(C_TOP_DIL):
            dst = slice(t * tile + rho * run, t * tile + (rho + 1) * run)
            o0r_ref[dst, :] = o0_ref[pl.ds(t * tile + rho, run, stride=C_TOP_DIL), :]
            l0r_ref[dst, :] = l0_ref[pl.ds(t * tile + rho, run, stride=C_TOP_DIL), :]
    l0, l1, l2 = l0r_ref[...], l1_ref[...], l2_ref[...]
    l_max = jnp.maximum(jnp.maximum(l0, l1), l2)
    w0, w1, w2 = jnp.exp(l0 - l_max), jnp.exp(l1 - l_max), jnp.exp(l2 - l_max)
    o0_ref[...] = (w0 * o0r_ref[...] + w1 * o1_ref[...] + w2 * o2_ref[...]) * (1.0 / (w0 + w1 + w2))

    def to_token_order(j, carry):
        src = o0_ref[pl.ds((j // run) * tile + j % run, C_TOP_DIL, stride=run), :]
        o_ref[pl.ds(pl.multiple_of(j * C_TOP_DIL, C_TOP_DIL), C_TOP_DIL), :] = src.astype(BF16)
        return carry

    lax.fori_loop(0, sub16, to_token_order, 0, unroll=8)


def _dilated_mixture(proj3, tile):
    nb, seq, _ = proj3.shape
    assert seq % tile == 0 and tile % (C_TOP_DIL * 32) == 0 and seq >= C_QBLK + 2 * C_HALF_WINDOW
    assert (seq // C_TOP_DIL) % 64 == 0 and seq % (C_QBLK * C_UNROLL) == 0
    side2 = max(seq // C_TOP_DIL, C_QBLK + 2 * C_HALF_WINDOW)

    def spec(kind, g):
        cb0 = _col_block(f"{kind}{g}")
        return pl.BlockSpec((None, seq, LANES), lambda b, h, cb0=cb0: (b, 0, cb0 + h))

    in_specs = [spec(kind, g) for kind in ("CQ", "CK", "CV") for g in range(len(C_DILATIONS))]
    return pl.pallas_call(
        functools.partial(_dil_kernel, seq=seq, tile=tile),
        grid=(nb, C_HEADS_PER_GROUP),
        in_specs=in_specs,
        out_specs=pl.BlockSpec((None, seq, LANES), lambda b, h: (b, 0, h)),
        out_shape=jax.ShapeDtypeStruct((nb, seq, BRANCH_WIDTH), BF16),
        scratch_shapes=(
            [pltpu.VMEM((seq, LANES), F32)] * 8
            + [pltpu.VMEM((3, C_QBLK, C_QBLK + 2 * C_HALF_WINDOW), F32)] * 2
            + [pltpu.VMEM((side2, side2), F32)]),
        compiler_params=_cparams(("arbitrary", "arbitrary")),
        name="dil_attn",
    )(*([proj3] * 9))


def _tail_kernel(ya_ref, yb_ref, yc_ref, z0_ref, z1_ref, z2_ref, g0a_ref, g0b_ref, g1a_ref,
                 g1b_ref, g2a_ref, g2b_ref, x_ref, gate_ref, wbr_ref, wout_ref, o_ref):
    halves = [None, None]
    branches = ((ya_ref, z0_ref, (g0a_ref, g0b_ref)), (yb_ref, z1_ref, (g1a_ref, g1b_ref)),
                (yc_ref, z2_ref, (g2a_ref, g2b_ref)))
    for i, (y_ref, z_ref, g_refs) in enumerate(branches):
        yz = (y_ref[...].astype(F32) * z_ref[...].astype(F32)).astype(BF16)
        for half, g_ref in enumerate(g_refs):
            cols = slice(half * PROJ_TILE, (half + 1) * PROJ_TILE)
            u = g_ref[...].astype(F32) * jnp.dot(yz, wbr_ref[i, :, cols], preferred_element_type=F32)
            halves[half] = u if halves[half] is None else halves[half] + u
    merged = jnp.concatenate(halves, axis=1).astype(BF16)
    out = jnp.dot(merged, wout_ref[...], preferred_element_type=F32)
    o_ref[...] = x_ref[...] + gate_ref[0] * out


def _tail(ya, yb, yc, proj2, x2, mod3, w_br_bf, w_out_bf, seq, tm):
    ntok = x2.shape[0]
    tps = seq // tm
    y_spec = pl.BlockSpec((None, tm, BRANCH_WIDTH), lambda i: (i // tps, i % tps, 0))

    def tile_spec(name):
        return pl.BlockSpec((tm, PROJ_TILE), lambda i, s=SLOT[name]: (i, s))

    z_specs = [tile_spec(f"Z{t}") for t in range(3)]
    g_specs = [tile_spec(f"G{t}{half}") for t in range(3) for half in "ab"]
    return pl.pallas_call(
        _tail_kernel,
        grid=(ntok // tm,),
        in_specs=[y_spec, y_spec, y_spec, *z_specs, *g_specs,
                  pl.BlockSpec((tm, D_MODEL), lambda i: (i, 0)),
                  pl.BlockSpec((1, 1, D_MODEL), lambda i: (i // tps, 0, 2)),
                  pl.BlockSpec((3, BRANCH_WIDTH, D_MODEL), lambda i: (0, 0, 0)),
                  pl.BlockSpec((D_MODEL, D_MODEL), lambda i: (0, 0))],
        out_specs=pl.BlockSpec((tm, D_MODEL), lambda i: (i, 0)),
        out_shape=jax.ShapeDtypeStruct((ntok, D_MODEL), F32),
        compiler_params=_cparams(("arbitrary",)),
        name="tail",
    )(ya, yb, yc, *([proj2] * 9), x2, mod3, w_br_bf, w_out_bf)


def _rope_tables(seq, tile):
    def base(d):
        inv = ROPE_THETA ** (-jnp.arange(0, d, 2, dtype=F32) / d)
        ang = jnp.arange(seq, dtype=F32)[:, None] * inv[None, :]
        return jnp.cos(ang), jnp.sin(ang)

    cos_a, sin_a = base(A_QK_DIM)
    ca = jnp.tile(cos_a, (1, 4))
    sa = jnp.concatenate([-sin_a, -sin_a, sin_a, sin_a], axis=-1)
    cos_c, sin_c = base(C_HEAD_DIM)
    cc = jnp.tile(cos_c, (1, 2))
    sc = jnp.concatenate([-sin_c, sin_c], axis=-1)

    def residue_major(tab):
        t4 = tab.reshape(seq // tile, tile // C_TOP_DIL, C_TOP_DIL, LANES)
        return t4.transpose(0, 2, 1, 3).reshape(seq, LANES)

    return ca, sa, cc, sc, residue_major(cc), residue_major(sc)


def _to_stored_columns(w):
    lead = w.shape[:-1]
    tiles = {name: w[..., i * PROJ_TILE:(i + 1) * PROJ_TILE] for i, name in enumerate(TILE_NAMES)}
    for name in ("AQ", "AK"):
        t = tiles[name].reshape(*lead, A_HEADS, 2, 2, A_QK_DIM // 2)
        tiles[name] = jnp.swapaxes(t, -3, -2).reshape(*lead, PROJ_TILE)
    return jnp.concatenate([tiles[n] for grp in PROJ_GROUPS for n in grp], axis=-1)


def _proj_gains(qn_a, kn_a, qn_b, kn_b, qn_c, kn_c):
    per_tile = {"AQ": jnp.tile(qn_a, PROJ_TILE // A_QK_DIM), "AK": jnp.tile(kn_a, PROJ_TILE // A_QK_DIM),
                "BQ": jnp.tile(qn_b, PROJ_TILE // B_HEAD_DIM), "BK": jnp.tile(kn_b, PROJ_TILE // B_HEAD_DIM)}
    for g in range(len(C_DILATIONS)):
        per_tile[f"CQ{g}"] = jnp.tile(qn_c, PROJ_TILE // C_HEAD_DIM)
        per_tile[f"CK{g}"] = jnp.tile(kn_c, PROJ_TILE // C_HEAD_DIM)
    ones = jnp.ones((PROJ_TILE,), F32)
    full = jnp.concatenate([per_tile.get(name, ones).astype(F32) for name in TILE_NAMES])
    return _to_stored_columns(full).reshape(len(PROJ_GROUPS), 1, GROUP_WIDTH)


def _proj_token_tile(seq):
    return min(1024, seq)


def _tail_token_tile(seq):
    return min(512, seq)


def _encoder_layer(x, mod3, layer_idx, tabs, ln_g, w_in_bf, gains, lam_params, subln, na_bias,
                   w_br_bf, w_out_bf):
    nb, seq, _ = x.shape
    x2 = x.reshape(nb * seq, D_MODEL)
    proj2 = _projection(x2, mod3, ln_g, w_in_bf, gains, tabs, seq, _proj_token_tile(seq))
    proj3 = proj2.reshape(nb, seq, IN_WIDTH)
    lam_init = 0.8 - 0.6 * math.exp(-0.3 * layer_idx)
    tq = A_SCORE_BYTES // (2 * 2 * seq * 4)
    ya = _diff_attention(proj3, lam_params, subln, lam_init, tq=tq)
    yb = _neighborhood_attention(proj3, na_bias)
    yc = _dilated_mixture(proj3, _proj_token_tile(seq))
    y2 = _tail(ya, yb, yc, proj2, x2, mod3, w_br_bf, w_out_bf, seq, _tail_token_tile(seq))
    return y2.reshape(nb, seq, D_MODEL)


def kernel(x_prompt, x_sample, c_prompt, c_sample, ln_g, w_ada, b_ada, w_in, qn_a, kn_a, lam_q1, lam_k1, lam_q2, lam_k2, subln_a, qn_b, kn_b, rpb_b, qn_c, kn_c, w_br, w_out):
    depth = w_in.shape[0]
    n_prompt = c_prompt.shape[0]
    mod_all = _modulation(jnp.concatenate([c_prompt, c_sample], axis=0), w_ada, b_ada)
    w_in_bf = _to_stored_columns(w_in.astype(BF16))
    w_br_bf = w_br.astype(BF16)
    w_out_bf = w_out.astype(BF16)
    gains = [_proj_gains(qn_a[l], kn_a[l], qn_b[l], kn_b[l], qn_c[l], kn_c[l]) for l in range(depth)]
    na_bias = [_na_bias_tables(rpb_b[l]) for l in range(depth)]

    def run(x, mod):
        nb, seq, _ = x.shape
        assert seq % (NA_QROWS * GRID_W) == 0 and seq // GRID_W >= NA_KROWS
        tabs = _rope_tables(seq, _proj_token_tile(seq))
        for l in range(depth):
            lam_params = tuple(p[l].reshape(1, A_QK_DIM) for p in (lam_q1, lam_k1, lam_q2, lam_k2))
            x = _encoder_layer(x, mod[l].reshape(nb, 1, 3 * D_MODEL), l, tabs,
                               ln_g[l].reshape(1, D_MODEL), w_in_bf[l], gains[l], lam_params,
                               subln_a[l].reshape(1, LANES), na_bias[l], w_br_bf[l], w_out_bf[l])
        return x

    y_prompt = run(x_prompt, mod_all[:, :n_prompt])
    y_sample = run(x_sample, mod_all[:, n_prompt:])
    return (y_prompt, y_sample)
```

```python
import functools
import math

import jax
import jax.numpy as jnp
from jax import lax
from jax.experimental import pallas as pl
from jax.experimental.pallas import tpu as pltpu

F32 = jnp.float32
BF16 = jnp.bfloat16

D_MODEL = 1024
GRID_W = 64
BRANCH_WIDTH = 512
ROPE_THETA = 10000.0
EPS = 1e-6
NEG = -1e30
A_QK_DIM = 64
A_HEADS = 4
B_HEAD_DIM = 64
B_HEADS = 8
NA_KH = 8
NA_KW = 16
C_DILATIONS = (1, 4, 16)
C_HALF_WINDOW = 64
C_HEAD_DIM = 128
C_HEADS_PER_GROUP = 4
IN_WIDTH = 12288
LANES = 128

PROJ_TILE = 512
TILE_NAMES = ("AQ", "AK", "AV", "BQ", "BK", "BV", "CQ0", "CQ1", "CQ2", "CK0", "CK1", "CK2",
              "CV0", "CV1", "CV2", "Z0", "Z1", "Z2", "G0a", "G0b", "G1a", "G1b", "G2a", "G2b")
PROJ_GROUPS = (("AQ", "AK", "BQ", "BK", "Z0", "G0a", "AV", "BV"),
               ("CQ0", "CK0", "CQ1", "Z1", "G0b", "G1a", "G1b", "CV0"),
               ("CK1", "CQ2", "CK2", "Z2", "G2a", "G2b", "CV1", "CV2"))
C_TOP_DIL = C_DILATIONS[-1]
RESIDUE_MAJOR_TILES = tuple(f"C{kind}{g}" for kind in "QKV" for g in (1, 2))
GROUP_TILES = len(PROJ_GROUPS[0])
GROUP_WIDTH = GROUP_TILES * PROJ_TILE
SLOT = {name: i for i, name in enumerate(n for grp in PROJ_GROUPS for n in grp)}


def _col_block(name, sub=0):
    return SLOT[name] * (PROJ_TILE // LANES) + sub

VMEM_LIMIT = 56 * 1024 * 1024

NA_QROWS = 8
NA_KROWS = 16
A_SCORE_BYTES = 16 * 1024 * 1024
C_QBLK = 128
C_UNROLL = 8


def _cparams(sem):
    return pltpu.CompilerParams(dimension_semantics=sem, vmem_limit_bytes=VMEM_LIMIT)


def _sigmoid(x):
    return 1.0 / (1.0 + jnp.exp(-x))


def _mod_kernel(c_ref, w_ref, b_ref, o_ref):
    c = c_ref[...]
    o_ref[0] = jnp.dot(c * _sigmoid(c), w_ref[0], preferred_element_type=F32) + b_ref[0]


def _modulation(c_all, w_ada, b_ada):
    depth = w_ada.shape[0]
    nb = c_all.shape[0]
    return pl.pallas_call(
        _mod_kernel,
        grid=(depth, 3),
        in_specs=[
            pl.BlockSpec((nb, D_MODEL), lambda l, j: (0, 0)),
            pl.BlockSpec((1, D_MODEL, D_MODEL), lambda l, j: (l, 0, j)),
            pl.BlockSpec((1, 1, D_MODEL), lambda l, j: (l, 0, j)),
        ],
        out_specs=pl.BlockSpec((1, nb, D_MODEL), lambda l, j: (l, 0, j)),
        out_shape=jax.ShapeDtypeStruct((depth, nb, 3 * D_MODEL), F32),
        compiler_params=_cparams(("arbitrary", "arbitrary")),
        name="adaln_mod",
    )(c_all, w_ada, b_ada.reshape(depth, 1, 3 * D_MODEL))


def _first_map_lanes(shape):
    return (lax.broadcasted_iota(jnp.int32, shape, 1) & 32) == 0


def _first_head_lanes(shape):
    return lax.broadcasted_iota(jnp.int32, shape, 1) < 64


def _rms_halves(xb, first):
    sq = xb * xb
    s_a = jnp.sum(jnp.where(first, sq, 0.0), axis=-1, keepdims=True)
    s_b = jnp.sum(jnp.where(first, 0.0, sq), axis=-1, keepdims=True)
    return lax.rsqrt(jnp.where(first, s_a, s_b) * (1.0 / 64.0) + EPS)


def _rms128(xb):
    return lax.rsqrt(jnp.mean(xb * xb, axis=-1, keepdims=True) + EPS)


def _rope(xb, cos, sin_signed):
    return xb * cos + pltpu.roll(xb, 64, 1) * sin_signed


def _tile_epilogue(name, xb, gain, tabs):
    ca_ref, sa_ref, cc_ref, sc_ref, ccr_ref, scr_ref = tabs
    if name in RESIDUE_MAJOR_TILES:
        cc_ref, sc_ref = ccr_ref, scr_ref
    if name == "AQ":
        rms = _rms_halves(xb, _first_map_lanes(xb.shape))
        return _rope(xb * (rms * gain), ca_ref[...], sa_ref[...]) * 0.125
    if name == "AK":
        rms = _rms_halves(xb, _first_map_lanes(xb.shape))
        return _rope(xb * (rms * gain), ca_ref[...], sa_ref[...])
    if name == "BQ":
        return xb * gain * (_rms_halves(xb, _first_head_lanes(xb.shape)) * 0.125)
    if name == "BK":
        return xb * gain * _rms_halves(xb, _first_head_lanes(xb.shape))
    if name[:2] in ("CQ", "CK"):
        return _rope(xb * gain, cc_ref[...], sc_ref[...]) * _rms128(xb)
    if name[0] == "Z":
        return xb * _sigmoid(xb)
    if name[0] == "G":
        return _sigmoid(xb)
    return xb


def _proj_kernel(x_ref, mod_ref, lng_ref, w_ref, gain_ref, ca_ref, sa_ref, cc_ref, sc_ref,
                 ccr_ref, scr_ref, o_ref, h_ref, hr_ref, hstage_ref, acc_ref):
    grp = pl.program_id(1)
    tm = h_ref.shape[0]
    run = tm // C_TOP_DIL

    @pl.when(grp == 0)
    def _():
        x = x_ref[...]
        ms = jnp.mean(x * x, axis=-1, keepdims=True)
        y = x * lax.rsqrt(ms + EPS) * lng_ref[...]
        mod = mod_ref[0]
        shift = mod[:, :D_MODEL]
        scale = mod[:, D_MODEL:2 * D_MODEL]
        h = y * (1.0 + scale) + shift
        h_ref[...] = h.astype(BF16)
        for cb in range(D_MODEL // LANES):
            cols = slice(cb * LANES, (cb + 1) * LANES)
            hstage_ref[cb] = h[:, cols]
            for rho in range(C_TOP_DIL):
                hr_ref[rho * run:(rho + 1) * run, cols] = hstage_ref[
                    cb, pl.ds(rho, run, stride=C_TOP_DIL), :].astype(BF16)

    tabs = (ca_ref, sa_ref, cc_ref, sc_ref, ccr_ref, scr_ref)
    for gi, names in enumerate(PROJ_GROUPS):
        @pl.when(grp == gi)
        def _(names=names):
            gain = gain_ref[0]
            for t, name in enumerate(names):
                acc = acc_ref.at[t % 2]
                lhs_ref = hr_ref if name in RESIDUE_MAJOR_TILES else h_ref
                acc[...] = jnp.dot(lhs_ref[...], w_ref[:, t * PROJ_TILE:(t + 1) * PROJ_TILE],
                                   preferred_element_type=F32)
                for cb in range(PROJ_TILE // LANES):
                    sl = slice(cb * LANES, (cb + 1) * LANES)
                    out = slice(t * PROJ_TILE + cb * LANES, t * PROJ_TILE + (cb + 1) * LANES)
                    o_ref[:, out] = _tile_epilogue(name, acc[:, sl], gain[:, out], tabs).astype(BF16)


def _projection(x2, mod3, ln_g, w_in_bf, gains, tabs, seq, tm):
    ntok = x2.shape[0]
    tps = seq // tm
    tab_spec = pl.BlockSpec((tm, LANES), lambda i, j: (i % tps, 0))
    return pl.pallas_call(
        _proj_kernel,
        grid=(ntok // tm, len(PROJ_GROUPS)),
        in_specs=[
            pl.BlockSpec((tm, D_MODEL), lambda i, j: (i, 0)),
            pl.BlockSpec((1, 1, 3 * D_MODEL), lambda i, j: (i // tps, 0, 0)),
            pl.BlockSpec((1, D_MODEL), lambda i, j: (0, 0)),
            pl.BlockSpec((D_MODEL, GROUP_WIDTH), lambda i, j: (0, j)),
            pl.BlockSpec((1, 1, GROUP_WIDTH), lambda i, j: (j, 0, 0)),
            *([tab_spec] * len(tabs)),
        ],
        out_specs=pl.BlockSpec((tm, GROUP_WIDTH), lambda i, j: (i, j)),
        out_shape=jax.ShapeDtypeStruct((ntok, IN_WIDTH), BF16),
        scratch_shapes=[pltpu.VMEM((tm, D_MODEL), BF16), pltpu.VMEM((tm, D_MODEL), BF16),
                        pltpu.VMEM((D_MODEL // LANES, tm, LANES), F32),
                        pltpu.VMEM((2, tm, PROJ_TILE), F32)],
        compiler_params=_cparams(("arbitrary", "arbitrary")),
        name="in_proj",
    )(x2, mod3, ln_g, w_in_bf, gains, *tabs)


def _diff_attn_kernel(q_ref, k_ref, v_ref, lq1_ref, lk1_ref, lq2_ref, lk2_ref, sub_ref, o_ref,
                      s_even_ref, s_odd_ref, *, lam_init):
    t = pl.program_id(0)
    dn = (((1,), (1,)), ((), ()))

    def score(s_ref):
        q = q_ref[...]
        k = k_ref[...]
        lo = _first_map_lanes(q.shape)
        zero = jnp.zeros_like(q)
        s_ref[0] = lax.dot_general(jnp.where(lo, q, zero), k, dn, preferred_element_type=F32)
        s_ref[1] = lax.dot_general(jnp.where(lo, zero, q), k, dn, preferred_element_type=F32)

    def finish(s_ref):
        lam = (jnp.exp(jnp.sum(lq1_ref[...] * lk1_ref[...], axis=-1, keepdims=True))
               - jnp.exp(jnp.sum(lq2_ref[...] * lk2_ref[...], axis=-1, keepdims=True)) + lam_init)
        v = v_ref[...]
        v_ones = jnp.concatenate([v, jnp.ones_like(v)], axis=1)
        prods = []
        for m in range(2):
            s = s_ref[m]
            e = jnp.exp(s - jnp.max(s, axis=-1, keepdims=True))
            prods.append(jnp.dot(e.astype(BF16), v_ones, preferred_element_type=F32))
        (a0, a1) = prods
        o = a0[:, :LANES] / a0[:, LANES:] - lam * (a1[:, :LANES] / a1[:, LANES:])
        ms = jnp.mean(o * o, axis=-1, keepdims=True)
        o_ref[...] = (o * lax.rsqrt(ms + EPS) * sub_ref[...] * (1.0 - lam_init)).astype(BF16)

    @pl.when(t == 0)
    def _():
        s_odd_ref[...] = jnp.zeros_like(s_odd_ref)

    @pl.when(t % 2 == 0)
    def _():
        score(s_even_ref)
        finish(s_odd_ref)

    @pl.when(t % 2 == 1)
    def _():
        score(s_odd_ref)
        finish(s_even_ref)


def _diff_attention(proj3, lam_params, subln, lam_init, tq):
    nb, seq, _ = proj3.shape
    nq = seq // tq
    tiles = nb * A_HEADS * nq

    def split(t):
        return t // (A_HEADS * nq), (t // nq) % A_HEADS, t % nq

    def cur(t):
        return split(jnp.minimum(t, tiles - 1))

    def prev(t):
        return split(jnp.maximum(t - 1, 0))

    vec = pl.BlockSpec((1, A_QK_DIM), lambda t: (0, 0))
    return pl.pallas_call(
        functools.partial(_diff_attn_kernel, lam_init=lam_init),
        grid=(tiles + 1,),
        in_specs=[
            pl.BlockSpec((None, tq, LANES), lambda t: (cur(t)[0], cur(t)[2], _col_block("AQ") + cur(t)[1])),
            pl.BlockSpec((None, seq, LANES), lambda t: (cur(t)[0], 0, _col_block("AK") + cur(t)[1])),
            pl.BlockSpec((None, seq, LANES), lambda t: (prev(t)[0], 0, _col_block("AV") + prev(t)[1])),
            vec, vec, vec, vec,
            pl.BlockSpec((1, LANES), lambda t: (0, 0)),
        ],
        out_specs=pl.BlockSpec((None, tq, LANES), lambda t: (prev(t)[0], prev(t)[2], prev(t)[1])),
        out_shape=jax.ShapeDtypeStruct((nb, seq, BRANCH_WIDTH), BF16),
        scratch_shapes=[pltpu.VMEM((2, tq, seq), F32), pltpu.VMEM((2, tq, seq), F32)],
        compiler_params=_cparams(("arbitrary",)),
        name="diff_attn",
    )(proj3, proj3, proj3, *lam_params, subln)


def _na_bias_tables(rpb):
    rows = NA_QROWS + NA_KROWS
    qc = jnp.arange(GRID_W)
    kc = jnp.arange(GRID_W)
    cs = jnp.clip(qc - NA_KW // 2, 0, GRID_W - NA_KW)
    col_ok = (kc[None, :] >= cs[:, None]) & (kc[None, :] < cs[:, None] + NA_KW)
    col_off = jnp.clip(kc[None, :] - qc[:, None], -(NA_KW - 1), NA_KW - 1) + NA_KW - 1
    cols = jnp.where(col_ok, rpb.astype(F32)[:, :, col_off], NEG)
    kh = min(NA_KH, rows)
    tables = []
    for r0, ws in ((0, 0), (NA_QROWS, NA_QROWS - kh // 2), (rows - NA_QROWS, rows - NA_KROWS)):
        r = r0 + jnp.arange(NA_QROWS)
        kr = ws + jnp.arange(NA_KROWS)
        rs = jnp.clip(r - kh // 2, 0, rows - kh)
        row_ok = (kr[None, :] >= rs[:, None]) & (kr[None, :] < rs[:, None] + kh)
        row_off = jnp.clip(kr[None, :] - r[:, None] + NA_KH - 1, 0, 2 * NA_KH - 2)
        t = jnp.where(row_ok[None, :, :, None, None], cols[:, row_off], NEG)
        tables.append(t.transpose(0, 1, 3, 2, 4).reshape(
            B_HEADS, NA_QROWS * GRID_W, NA_KROWS * GRID_W))
    return jnp.stack(tables)


def _na_kernel(q_ref, k_ref, v_ref, bias_ref, o_ref, s_even_ref, s_odd_ref, *, rows, nrb):
    t = pl.program_id(0)
    nkeys = NA_KROWS * GRID_W
    dn = (((1,), (1,)), ((), ()))

    def key_start(i):
        ws = jnp.clip(i * NA_QROWS - NA_KH // 2, 0, rows - NA_KROWS)
        return pl.multiple_of(ws * GRID_W, GRID_W)

    def score(s_ref):
        i = jnp.minimum(t, pl.num_programs(0) - 2) % nrb
        kind = jnp.where(i == 0, 0, jnp.where(i == nrb - 1, 2, 1))
        kw = k_ref[pl.ds(key_start(i), nkeys), :]
        q = q_ref[...]
        lo = _first_head_lanes(q.shape)
        zero = jnp.zeros_like(q)
        for e in range(2):
            qe = jnp.where(lo, q, zero) if e == 0 else jnp.where(lo, zero, q)
            s_ref[e] = lax.dot_general(qe, kw, dn, preferred_element_type=F32) + bias_ref[kind, e]

    def finish(s_ref):
        i = jnp.maximum(t - 1, 0) % nrb
        vw = v_ref[pl.ds(key_start(i), nkeys), :]
        v_ones = jnp.concatenate([vw, jnp.ones_like(vw)], axis=1)
        outs = []
        for e in range(2):
            s = s_ref[e]
            ex = jnp.exp(s - jnp.max(s, axis=-1, keepdims=True))
            a = jnp.dot(ex.astype(BF16), v_ones, preferred_element_type=F32)
            outs.append(a[:, :LANES] / a[:, LANES:])
        o_ref[...] = jnp.where(_first_head_lanes(outs[0].shape), outs[0], outs[1]).astype(BF16)

    @pl.when(t == 0)
    def _():
        s_odd_ref[...] = jnp.zeros_like(s_odd_ref)

    @pl.when(t % 2 == 0)
    def _():
        score(s_even_ref)
        finish(s_odd_ref)

    @pl.when(t % 2 == 1)
    def _():
        score(s_odd_ref)
        finish(s_even_ref)


def _neighborhood_attention(proj3, bias):
    nb, seq, _ = proj3.shape
    rows = seq // GRID_W
    nrb = rows // NA_QROWS
    nq = NA_QROWS * GRID_W
    nk = NA_KROWS * GRID_W
    tiles = (B_HEADS // 2) * nb * nrb

    def split(t):
        return t // (nb * nrb), (t // nrb) % nb, t % nrb

    def cur(t):
        return split(jnp.minimum(t, tiles - 1))

    def prev(t):
        return split(jnp.maximum(t - 1, 0))

    return pl.pallas_call(
        functools.partial(_na_kernel, rows=rows, nrb=nrb),
        grid=(tiles + 1,),
        in_specs=[
            pl.BlockSpec((None, nq, LANES), lambda t: (cur(t)[1], cur(t)[2], _col_block("BQ") + cur(t)[0])),
            pl.BlockSpec((None, seq, LANES), lambda t: (cur(t)[1], 0, _col_block("BK") + cur(t)[0])),
            pl.BlockSpec((None, seq, LANES), lambda t: (prev(t)[1], 0, _col_block("BV") + prev(t)[0])),
            pl.BlockSpec((3, 2, nq, nk), lambda t: (0, cur(t)[0], 0, 0)),
        ],
        out_specs=pl.BlockSpec((None, nq, LANES), lambda t: (prev(t)[1], prev(t)[2], prev(t)[0])),
        out_shape=jax.ShapeDtypeStruct((nb, seq, BRANCH_WIDTH), BF16),
        scratch_shapes=[pltpu.VMEM((2, nq, nk), F32), pltpu.VMEM((2, nq, nk), F32)],
        compiler_params=_cparams(("arbitrary",)),
        name="nbr_attn",
    )(proj3, proj3, proj3, bias)


def _softmax_av(s, v):
    m = jnp.max(s, axis=-1, keepdims=True)
    e = jnp.exp(s - m)
    v_ones = jnp.concatenate([v, jnp.ones_like(v)], axis=1)
    acc = jnp.dot(e.astype(BF16), v_ones, preferred_element_type=F32)
    l = acc[:, LANES:]
    return acc[:, :LANES] / l, m + jnp.log(l)


def _dil_kernel(q0_ref, q1_ref, q2_ref, k0_ref, k1_ref, k2_ref, v0_ref, v1_ref, v2_ref, o_ref,
                o0_ref, l0_ref, o0r_ref, l0r_ref, o1_ref, l1_ref, o2_ref, l2_ref,
                band_ref, band4_ref, band16_ref, *, seq, tile):
    scale = C_HEAD_DIM ** -0.5
    dn = (((1,), (1,)), ((), ()))
    run = tile // C_TOP_DIL
    n_tiles = seq // tile
    sub16 = seq // C_TOP_DIL
    hw = C_HALF_WINDOW

    full_win = C_QBLK + 2 * hw
    qi = lax.broadcasted_iota(jnp.int32, (C_QBLK, full_win), 0)
    kj = lax.broadcasted_iota(jnp.int32, (C_QBLK, full_win), 1)
    for lead in range(3):
        band_ref[lead] = jnp.where(jnp.abs(kj - qi - lead * hw) <= hw, 0.0, NEG)
    pack = band16_ref.shape[0] // sub16
    qi = lax.broadcasted_iota(jnp.int32, band16_ref.shape, 0)
    kj = lax.broadcasted_iota(jnp.int32, band16_ref.shape, 1)
    same_class = (qi // sub16) == (kj // sub16)
    band16_ref[...] = jnp.where(same_class & (jnp.abs(kj - qi) <= hw), 0.0, NEG)
    qr = lax.broadcasted_iota(jnp.int32, (C_QBLK, 2 * C_QBLK), 0)
    kc = lax.broadcasted_iota(jnp.int32, (C_QBLK, 2 * C_QBLK), 1)
    q_pos = 4 * (qr % 32) + qr // 32
    k_pos = 4 * (kc % 64) + kc // 64
    for lead in range(3):
        band4_ref[lead] = jnp.where(jnp.abs(q_pos + 64 * lead - k_pos) <= hw, 0.0, NEG)

    def attend0(n):
        a = n * C_QBLK
        ws = jnp.clip(a - hw, 0, seq - full_win)
        rows = pl.ds(pl.multiple_of(a, C_QBLK), C_QBLK)
        kw = k0_ref[pl.ds(pl.multiple_of(ws, hw), full_win), :]
        vw = v0_ref[pl.ds(pl.multiple_of(ws, hw), full_win), :]
        s = lax.dot_general(q0_ref[rows, :], kw, dn, preferred_element_type=F32) * scale
        o0_ref[rows, :], l0_ref[rows, :] = _softmax_av(s + band_ref[(a - ws) // hw], vw)

    def blocks0(it, carry):
        for u in range(C_UNROLL):
            attend0(it * C_UNROLL + u)
        return carry

    lax.fori_loop(0, seq // (C_QBLK * C_UNROLL), blocks0, 0)

    def row_of(r16, j):
        return (j // run) * tile + r16 * run + j % run

    def attend1(n):
        rho4 = n // (sub16 // 32)
        j0 = (n % (sub16 // 32)) * 32
        jw = jnp.clip(j0 - 16, 0, sub16 - 64)
        q_rows = [pl.ds(pl.multiple_of(row_of(4 * c + rho4, j0), 32), 32) for c in range(4)]
        k_rows = [pl.ds(pl.multiple_of(row_of(4 * c + rho4, jw + 16 * p), 16), 16)
                  for c in range(4) for p in range(4)]
        q = jnp.concatenate([q1_ref[r, :] for r in q_rows], axis=0)
        kw = jnp.concatenate([k1_ref[r, :] for r in k_rows], axis=0)
        vw = jnp.concatenate([v1_ref[r, :] for r in k_rows], axis=0)
        s = lax.dot_general(q, kw, dn, preferred_element_type=F32) * scale
        o, lse = _softmax_av(s + band4_ref[(j0 - jw) // 16], vw)
        for c, r in enumerate(q_rows):
            o1_ref[r, :] = o[c * 32:(c + 1) * 32]
            l1_ref[r, :] = lse[c * 32:(c + 1) * 32]

    def blocks1(it, carry):
        for u in range(C_UNROLL):
            attend1(it * C_UNROLL + u)
        return carry

    lax.fori_loop(0, seq // (C_QBLK * C_UNROLL), blocks1, 0)

    def attend2(n):
        rows = [pl.ds(pl.multiple_of(t * tile + (n * pack + u) * run, run), run)
                for u in range(pack) for t in range(n_tiles)]
        q = jnp.concatenate([q2_ref[r, :] for r in rows], axis=0)
        kw = jnp.concatenate([k2_ref[r, :] for r in rows], axis=0)
        vw = jnp.concatenate([v2_ref[r, :] for r in rows], axis=0)
        s = lax.dot_general(q, kw, dn, preferred_element_type=F32) * scale
        o, lse = _softmax_av(s + band16_ref[...], vw)
        for i, r in enumerate(rows):
            o2_ref[r, :] = o[i * run:(i + 1) * run]
            l2_ref[r, :] = lse[i * run:(i + 1) * run]

    unroll2 = min(C_UNROLL, C_TOP_DIL // pack)

    def blocks2(it, carry):
        for u in range(unroll2):
            attend2(it * unroll2 + u)
        return carry

    lax.fori_loop(0, C_TOP_DIL // pack // unroll2, blocks2, 0)

    for t in range(n_tiles):
        for rho in range(C_TOP_DIL):
            dst = slice(t * tile + rho * run, t * tile + (rho + 1) * run)
            o0r_ref[dst, :] = o0_ref[pl.ds(t * tile + rho, run, stride=C_TOP_DIL), :]
            l0r_ref[dst, :] = l0_ref[pl.ds(t * tile + rho, run, stride=C_TOP_DIL), :]
    l0, l1, l2 = l0r_ref[...], l1_ref[...], l2_ref[...]
    l_max = jnp.maximum(jnp.maximum(l0, l1), l2)
    w0, w1, w2 = jnp.exp(l0 - l_max), jnp.exp(l1 - l_max), jnp.exp(l2 - l_max)
    o0_ref[...] = (w0 * o0r_ref[...] + w1 * o1_ref[...] + w2 * o2_ref[...]) * (1.0 / (w0 + w1 + w2))

    def to_token_order(j, carry):
        src = o0_ref[pl.ds((j // run) * tile + j % run, C_TOP_DIL, stride=run), :]
        o_ref[pl.ds(pl.multiple_of(j * C_TOP_DIL, C_TOP_DIL), C_TOP_DIL), :] = src.astype(BF16)
        return carry

    lax.fori_loop(0, sub16, to_token_order, 0, unroll=8)


def _dilated_mixture(proj3, tile):
    nb, seq, _ = proj3.shape
    assert seq % tile == 0 and tile % (C_TOP_DIL * 32) == 0 and seq >= C_QBLK + 2 * C_HALF_WINDOW
    assert (seq // C_TOP_DIL) % 64 == 0 and seq % (C_QBLK * C_UNROLL) == 0
    side2 = max(seq // C_TOP_DIL, C_QBLK + 2 * C_HALF_WINDOW)

    def spec(kind, g):
        cb0 = _col_block(f"{kind}{g}")
        return pl.BlockSpec((None, seq, LANES), lambda b, h, cb0=cb0: (b, 0, cb0 + h))

    in_specs = [spec(kind, g) for kind in ("CQ", "CK", "CV") for g in range(len(C_DILATIONS))]
    return pl.pallas_call(
        functools.partial(_dil_kernel, seq=seq, tile=tile),
        grid=(nb, C_HEADS_PER_GROUP),
        in_specs=in_specs,
        out_specs=pl.BlockSpec((None, seq, LANES), lambda b, h: (b, 0, h)),
        out_shape=jax.ShapeDtypeStruct((nb, seq, BRANCH_WIDTH), BF16),
        scratch_shapes=(
            [pltpu.VMEM((seq, LANES), F32)] * 8
            + [pltpu.VMEM((3, C_QBLK, C_QBLK + 2 * C_HALF_WINDOW), F32)] * 2
            + [pltpu.VMEM((side2, side2), F32)]),
        compiler_params=_cparams(("arbitrary", "arbitrary")),
        name="dil_attn",
    )(*([proj3] * 9))


def _tail_kernel(ya_ref, yb_ref, yc_ref, z0_ref, z1_ref, z2_ref, g0a_ref, g0b_ref, g1a_ref,
                 g1b_ref, g2a_ref, g2b_ref, x_ref, gate_ref, wbr_ref, wout_ref, o_ref):
    halves = [None, None]
    branches = ((ya_ref, z0_ref, (g0a_ref, g0b_ref)), (yb_ref, z1_ref, (g1a_ref, g1b_ref)),
                (yc_ref, z2_ref, (g2a_ref, g2b_ref)))
    for i, (y_ref, z_ref, g_refs) in enumerate(branches):
        yz = (y_ref[...].astype(F32) * z_ref[...].astype(F32)).astype(BF16)
        for half, g_ref in enumerate(g_refs):
            cols = slice(half * PROJ_TILE, (half + 1) * PROJ_TILE)
            u = g_ref[...].astype(F32) * jnp.dot(yz, wbr_ref[i, :, cols], preferred_element_type=F32)
            halves[half] = u if halves[half] is None else halves[half] + u
    merged = jnp.concatenate(halves, axis=1).astype(BF16)
    out = jnp.dot(merged, wout_ref[...], preferred_element_type=F32)
    o_ref[...] = x_ref[...] + gate_ref[0] * out


def _tail(ya, yb, yc, proj2, x2, mod3, w_br_bf, w_out_bf, seq, tm):
    ntok = x2.shape[0]
    tps = seq // tm
    y_spec = pl.BlockSpec((None, tm, BRANCH_WIDTH), lambda i: (i // tps, i % tps, 0))

    def tile_spec(name):
        return pl.BlockSpec((tm, PROJ_TILE), lambda i, s=SLOT[name]: (i, s))

    z_specs = [tile_spec(f"Z{t}") for t in range(3)]
    g_specs = [tile_spec(f"G{t}{half}") for t in range(3) for half in "ab"]
    return pl.pallas_call(
        _tail_kernel,
        grid=(ntok // tm,),
        in_specs=[y_spec, y_spec, y_spec, *z_specs, *g_specs,
                  pl.BlockSpec((tm, D_MODEL), lambda i: (i, 0)),
                  pl.BlockSpec((1, 1, D_MODEL), lambda i: (i // tps, 0, 2)),
                  pl.BlockSpec((3, BRANCH_WIDTH, D_MODEL), lambda i: (0, 0, 0)),
                  pl.BlockSpec((D_MODEL, D_MODEL), lambda i: (0, 0))],
        out_specs=pl.BlockSpec((tm, D_MODEL), lambda i: (i, 0)),
        out_shape=jax.ShapeDtypeStruct((ntok, D_MODEL), F32),
        compiler_params=_cparams(("arbitrary",)),
        name="tail",
    )(ya, yb, yc, *([proj2] * 9), x2, mod3, w_br_bf, w_out_bf)


def _rope_tables(seq, tile):
    def base(d):
        inv = ROPE_THETA ** (-jnp.arange(0, d, 2, dtype=F32) / d)
        ang = jnp.arange(seq, dtype=F32)[:, None] * inv[None, :]
        return jnp.cos(ang), jnp.sin(ang)

    cos_a, sin_a = base(A_QK_DIM)
    ca = jnp.tile(cos_a, (1, 4))
    sa = jnp.concatenate([-sin_a, -sin_a, sin_a, sin_a], axis=-1)
    cos_c, sin_c = base(C_HEAD_DIM)
    cc = jnp.tile(cos_c, (1, 2))
    sc = jnp.concatenate([-sin_c, sin_c], axis=-1)

    def residue_major(tab):
        t4 = tab.reshape(seq // tile, tile // C_TOP_DIL, C_TOP_DIL, LANES)
        return t4.transpose(0, 2, 1, 3).reshape(seq, LANES)

    return ca, sa, cc, sc, residue_major(cc), residue_major(sc)


def _to_stored_columns(w):
    lead = w.shape[:-1]
    tiles = {name: w[..., i * PROJ_TILE:(i + 1) * PROJ_TILE] for i, name in enumerate(TILE_NAMES)}
    for name in ("AQ", "AK"):
        t = tiles[name].reshape(*lead, A_HEADS, 2, 2, A_QK_DIM // 2)
        tiles[name] = jnp.swapaxes(t, -3, -2).reshape(*lead, PROJ_TILE)
    return jnp.concatenate([tiles[n] for grp in PROJ_GROUPS for n in grp], axis=-1)


def _proj_gains(qn_a, kn_a, qn_b, kn_b, qn_c, kn_c):
    per_tile = {"AQ": jnp.tile(qn_a, PROJ_TILE // A_QK_DIM), "AK": jnp.tile(kn_a, PROJ_TILE // A_QK_DIM),
                "BQ": jnp.tile(qn_b, PROJ_TILE // B_HEAD_DIM), "BK": jnp.tile(kn_b, PROJ_TILE // B_HEAD_DIM)}
    for g in range(len(C_DILATIONS)):
        per_tile[f"CQ{g}"] = jnp.tile(qn_c, PROJ_TILE // C_HEAD_DIM)
        per_tile[f"CK{g}"] = jnp.tile(kn_c, PROJ_TILE // C_HEAD_DIM)
    ones = jnp.ones((PROJ_TILE,), F32)
    full = jnp.concatenate([per_tile.get(name, ones).astype(F32) for name in TILE_NAMES])
    return _to_stored_columns(full).reshape(len(PROJ_GROUPS), 1, GROUP_WIDTH)


def _proj_token_tile(seq):
    return min(512, seq)


def _tail_token_tile(seq):
    return min(512, seq)


def _encoder_layer(x, mod3, layer_idx, tabs, ln_g, w_in_bf, gains, lam_params, subln, na_bias,
                   w_br_bf, w_out_bf):
    nb, seq, _ = x.shape
    x2 = x.reshape(nb * seq, D_MODEL)
    proj2 = _projection(x2, mod3, ln_g, w_in_bf, gains, tabs, seq, _proj_token_tile(seq))
    proj3 = proj2.reshape(nb, seq, IN_WIDTH)
    lam_init = 0.8 - 0.6 * math.exp(-0.3 * layer_idx)
    tq = A_SCORE_BYTES // (2 * 2 * seq * 4)
    ya = _diff_attention(proj3, lam_params, subln, lam_init, tq=tq)
    yb = _neighborhood_attention(proj3, na_bias)
    yc = _dilated_mixture(proj3, _proj_token_tile(seq))
    y2 = _tail(ya, yb, yc, proj2, x2, mod3, w_br_bf, w_out_bf, seq, _tail_token_tile(seq))
    return y2.reshape(nb, seq, D_MODEL)


def kernel(x_prompt, x_sample, c_prompt, c_sample, ln_g, w_ada, b_ada, w_in, qn_a, kn_a, lam_q1, lam_k1, lam_q2, lam_k2, subln_a, qn_b, kn_b, rpb_b, qn_c, kn_c, w_br, w_out):
    depth = w_in.shape[0]
    n_prompt = c_prompt.shape[0]
    mod_all = _modulation(jnp.concatenate([c_prompt, c_sample], axis=0), w_ada, b_ada)
    w_in_bf = _to_stored_columns(w_in.astype(BF16))
    w_br_bf = w_br.astype(BF16)
    w_out_bf = w_out.astype(BF16)
    gains = [_proj_gains(qn_a[l], kn_a[l], qn_b[l], kn_b[l], qn_c[l], kn_c[l]) for l in range(depth)]
    na_bias = [_na_bias_tables(rpb_b[l]) for l in range(depth)]

    def run(x, mod):
        nb, seq, _ = x.shape
        assert seq % (NA_QROWS * GRID_W) == 0 and seq // GRID_W >= NA_KROWS
        tabs = _rope_tables(seq, _proj_token_tile(seq))
        for l in range(depth):
            lam_params = tuple(p[l].reshape(1, A_QK_DIM) for p in (lam_q1, lam_k1, lam_q2, lam_k2))
            x = _encoder_layer(x, mod[l].reshape(nb, 1, 3 * D_MODEL), l, tabs,
                               ln_g[l].reshape(1, D_MODEL), w_in_bf[l], gains[l], lam_params,
                               subln_a[l].reshape(1, LANES), na_bias[l], w_br_bf[l], w_out_bf[l])
        return x

    y_prompt = run(x_prompt, mod_all[:, :n_prompt])
    y_sample = run(x_sample, mod_all[:, n_prompt:])
    return (y_prompt, y_sample)
```

```python
import functools
import math

import jax
import jax.numpy as jnp
from jax import lax
from jax.experimental import pallas as pl
from jax.experimental.pallas import tpu as pltpu

F32 = jnp.float32
BF16 = jnp.bfloat16

D_MODEL = 1024
GRID_W = 64
BRANCH_WIDTH = 512
ROPE_THETA = 10000.0
EPS = 1e-6
NEG = -1e30
A_QK_DIM = 64
A_HEADS = 4
B_HEAD_DIM = 64
B_HEADS = 8
NA_KH = 8
NA_KW = 16
C_DILATIONS = (1, 4, 16)
C_HALF_WINDOW = 64
C_HEAD_DIM = 128
C_HEADS_PER_GROUP = 4
IN_WIDTH = 12288
LANES = 128

PROJ_TILE = 512
TILE_NAMES = ("AQ", "AK", "AV", "BQ", "BK", "BV", "CQ0", "CQ1", "CQ2", "CK0", "CK1", "CK2",
              "CV0", "CV1", "CV2", "Z0", "Z1", "Z2", "G0a", "G0b", "G1a", "G1b", "G2a", "G2b")
PROJ_GROUPS = (("AQ", "AK", "BQ", "BK", "Z0", "G0a", "AV", "BV"),
               ("CQ0", "CK0", "CQ1", "Z1", "G0b", "G1a", "G1b", "CV0"),
               ("CK1", "CQ2", "CK2", "Z2", "G2a", "G2b", "CV1", "CV2"))
C_TOP_DIL = C_DILATIONS[-1]
RESIDUE_MAJOR_TILES = tuple(f"C{kind}{g}" for kind in "QKV" for g in (1, 2))
GROUP_TILES = len(PROJ_GROUPS[0])
GROUP_WIDTH = GROUP_TILES * PROJ_TILE
SLOT = {name: i for i, name in enumerate(n for grp in PROJ_GROUPS for n in grp)}


def _col_block(name, sub=0):
    return SLOT[name] * (PROJ_TILE // LANES) + sub

VMEM_LIMIT = 56 * 1024 * 1024

NA_QROWS = 8
NA_KROWS = 16
A_SCORE_BYTES = 16 * 1024 * 1024
C_QBLK = 128
C_UNROLL = 8


def _cparams(sem):
    return pltpu.CompilerParams(dimension_semantics=sem, vmem_limit_bytes=VMEM_LIMIT)


def _sigmoid(x):
    return 1.0 / (1.0 + jnp.exp(-x))


def _mod_kernel(c_ref, w_ref, b_ref, o_ref):
    c = c_ref[...]
    o_ref[0] = jnp.dot(c * _sigmoid(c), w_ref[0], preferred_element_type=F32) + b_ref[0]


def _modulation(c_all, w_ada, b_ada):
    depth = w_ada.shape[0]
    nb = c_all.shape[0]
    return pl.pallas_call(
        _mod_kernel,
        grid=(depth, 3),
        in_specs=[
            pl.BlockSpec((nb, D_MODEL), lambda l, j: (0, 0)),
            pl.BlockSpec((1, D_MODEL, D_MODEL), lambda l, j: (l, 0, j)),
            pl.BlockSpec((1, 1, D_MODEL), lambda l, j: (l, 0, j)),
        ],
        out_specs=pl.BlockSpec((1, nb, D_MODEL), lambda l, j: (l, 0, j)),
        out_shape=jax.ShapeDtypeStruct((depth, nb, 3 * D_MODEL), F32),
        compiler_params=_cparams(("arbitrary", "arbitrary")),
        name="adaln_mod",
    )(c_all, w_ada, b_ada.reshape(depth, 1, 3 * D_MODEL))


def _first_map_lanes(shape):
    return (lax.broadcasted_iota(jnp.int32, shape, 1) & 32) == 0


def _first_head_lanes(shape):
    return lax.broadcasted_iota(jnp.int32, shape, 1) < 64


def _rms_halves(xb, first):
    sq = xb * xb
    s_a = jnp.sum(jnp.where(first, sq, 0.0), axis=-1, keepdims=True)
    s_b = jnp.sum(jnp.where(first, 0.0, sq), axis=-1, keepdims=True)
    return lax.rsqrt(jnp.where(first, s_a, s_b) * (1.0 / 64.0) + EPS)


def _rms128(xb):
    return lax.rsqrt(jnp.mean(xb * xb, axis=-1, keepdims=True) + EPS)


def _rope(xb, cos, sin_signed):
    return xb * cos + pltpu.roll(xb, 64, 1) * sin_signed


def _tile_epilogue(name, xb, gain, tabs):
    ca_ref, sa_ref, cc_ref, sc_ref, ccr_ref, scr_ref = tabs
    if name in RESIDUE_MAJOR_TILES:
        cc_ref, sc_ref = ccr_ref, scr_ref
    if name == "AQ":
        rms = _rms_halves(xb, _first_map_lanes(xb.shape))
        return _rope(xb * (rms * gain), ca_ref[...], sa_ref[...]) * 0.125
    if name == "AK":
        rms = _rms_halves(xb, _first_map_lanes(xb.shape))
        return _rope(xb * (rms * gain), ca_ref[...], sa_ref[...])
    if name == "BQ":
        return xb * gain * (_rms_halves(xb, _first_head_lanes(xb.shape)) * 0.125)
    if name == "BK":
        return xb * gain * _rms_halves(xb, _first_head_lanes(xb.shape))
    if name[:2] in ("CQ", "CK"):
        return _rope(xb * gain, cc_ref[...], sc_ref[...]) * _rms128(xb)
    if name[0] == "Z":
        return xb * _sigmoid(xb)
    if name[0] == "G":
        return _sigmoid(xb)
    return xb


def _proj_kernel(x_ref, mod_ref, lng_ref, w_ref, gain_ref, ca_ref, sa_ref, cc_ref, sc_ref,
                 ccr_ref, scr_ref, o_ref, h_ref, hr_ref, hstage_ref, acc_ref):
    grp = pl.program_id(1)
    tm = h_ref.shape[0]
    run = tm // C_TOP_DIL

    @pl.when(grp == 0)
    def _():
        x = x_ref[...]
        ms = jnp.mean(x * x, axis=-1, keepdims=True)
        y = x * lax.rsqrt(ms + EPS) * lng_ref[...]
        mod = mod_ref[0]
        shift = mod[:, :D_MODEL]
        scale = mod[:, D_MODEL:2 * D_MODEL]
        h = y * (1.0 + scale) + shift
        h_ref[...] = h.astype(BF16)
        for cb in range(D_MODEL // LANES):
            cols = slice(cb * LANES, (cb + 1) * LANES)
            hstage_ref[cb] = h[:, cols]
            for rho in range(C_TOP_DIL):
                hr_ref[rho * run:(rho + 1) * run, cols] = hstage_ref[
                    cb, pl.ds(rho, run, stride=C_TOP_DIL), :].astype(BF16)

    tabs = (ca_ref, sa_ref, cc_ref, sc_ref, ccr_ref, scr_ref)
    for gi, names in enumerate(PROJ_GROUPS):
        @pl.when(grp == gi)
        def _(names=names):
            gain = gain_ref[0]
            for t, name in enumerate(names):
                acc = acc_ref.at[t % 2]
                lhs_ref = hr_ref if name in RESIDUE_MAJOR_TILES else h_ref
                acc[...] = jnp.dot(lhs_ref[...], w_ref[:, t * PROJ_TILE:(t + 1) * PROJ_TILE],
                                   preferred_element_type=F32)
                for cb in range(PROJ_TILE // LANES):
                    sl = slice(cb * LANES, (cb + 1) * LANES)
                    out = slice(t * PROJ_TILE + cb * LANES, t * PROJ_TILE + (cb + 1) * LANES)
                    o_ref[:, out] = _tile_epilogue(name, acc[:, sl], gain[:, out], tabs).astype(BF16)


def _projection(x2, mod3, ln_g, w_in_bf, gains, tabs, seq, tm):
    ntok = x2.shape[0]
    tps = seq // tm
    tab_spec = pl.BlockSpec((tm, LANES), lambda i, j: (i % tps, 0))
    return pl.pallas_call(
        _proj_kernel,
        grid=(ntok // tm, len(PROJ_GROUPS)),
        in_specs=[
            pl.BlockSpec((tm, D_MODEL), lambda i, j: (i, 0)),
            pl.BlockSpec((1, 1, 3 * D_MODEL), lambda i, j: (i // tps, 0, 0)),
            pl.BlockSpec((1, D_MODEL), lambda i, j: (0, 0)),
            pl.BlockSpec((D_MODEL, GROUP_WIDTH), lambda i, j: (0, j)),
            pl.BlockSpec((1, 1, GROUP_WIDTH), lambda i, j: (j, 0, 0)),
            *([tab_spec] * len(tabs)),
        ],
        out_specs=pl.BlockSpec((tm, GROUP_WIDTH), lambda i, j: (i, j)),
        out_shape=jax.ShapeDtypeStruct((ntok, IN_WIDTH), BF16),
        scratch_shapes=[pltpu.VMEM((tm, D_MODEL), BF16), pltpu.VMEM((tm, D_MODEL), BF16),
                        pltpu.VMEM((D_MODEL // LANES, tm, LANES), F32),
                        pltpu.VMEM((2, tm, PROJ_TILE), F32)],
        compiler_params=_cparams(("arbitrary", "arbitrary")),
        name="in_proj",
    )(x2, mod3, ln_g, w_in_bf, gains, *tabs)


def _diff_attn_kernel(q_ref, k_ref, v_ref, lq1_ref, lk1_ref, lq2_ref, lk2_ref, sub_ref, o_ref,
                      s_even_ref, s_odd_ref, *, lam_init):
    t = pl.program_id(0)
    dn = (((1,), (1,)), ((), ()))

    def score(s_ref):
        q = q_ref[...]
        k = k_ref[...]
        lo = _first_map_lanes(q.shape)
        zero = jnp.zeros_like(q)
        s_ref[0] = lax.dot_general(jnp.where(lo, q, zero), k, dn, preferred_element_type=F32)
        s_ref[1] = lax.dot_general(jnp.where(lo, zero, q), k, dn, preferred_element_type=F32)

    def finish(s_ref):
        lam = (jnp.exp(jnp.sum(lq1_ref[...] * lk1_ref[...], axis=-1, keepdims=True))
               - jnp.exp(jnp.sum(lq2_ref[...] * lk2_ref[...], axis=-1, keepdims=True)) + lam_init)
        v = v_ref[...]
        v_ones = jnp.concatenate([v, jnp.ones_like(v)], axis=1)
        prods = []
        for m in range(2):
            s = s_ref[m]
            e = jnp.exp(s - jnp.max(s, axis=-1, keepdims=True))
            prods.append(jnp.dot(e.astype(BF16), v_ones, preferred_element_type=F32))
        (a0, a1) = prods
        o = a0[:, :LANES] / a0[:, LANES:] - lam * (a1[:, :LANES] / a1[:, LANES:])
        ms = jnp.mean(o * o, axis=-1, keepdims=True)
        o_ref[...] = (o * lax.rsqrt(ms + EPS) * sub_ref[...] * (1.0 - lam_init)).astype(BF16)

    @pl.when(t == 0)
    def _():
        s_odd_ref[...] = jnp.zeros_like(s_odd_ref)

    @pl.when(t % 2 == 0)
    def _():
        score(s_even_ref)
        finish(s_odd_ref)

    @pl.when(t % 2 == 1)
    def _():
        score(s_odd_ref)
        finish(s_even_ref)


def _diff_attention(proj3, lam_params, subln, lam_init, tq):
    nb, seq, _ = proj3.shape
    nq = seq // tq
    tiles = nb * A_HEADS * nq

    def split(t):
        return t // (A_HEADS * nq), (t // nq) % A_HEADS, t % nq

    def cur(t):
        return split(jnp.minimum(t, tiles - 1))

    def prev(t):
        return split(jnp.maximum(t - 1, 0))

    vec = pl.BlockSpec((1, A_QK_DIM), lambda t: (0, 0))
    return pl.pallas_call(
        functools.partial(_diff_attn_kernel, lam_init=lam_init),
        grid=(tiles + 1,),
        in_specs=[
            pl.BlockSpec((None, tq, LANES), lambda t: (cur(t)[0], cur(t)[2], _col_block("AQ") + cur(t)[1])),
            pl.BlockSpec((None, seq, LANES), lambda t: (cur(t)[0], 0, _col_block("AK") + cur(t)[1])),
            pl.BlockSpec((None, seq, LANES), lambda t: (prev(t)[0], 0, _col_block("AV") + prev(t)[1])),
            vec, vec, vec, vec,
            pl.BlockSpec((1, LANES), lambda t: (0, 0)),
        ],
        out_specs=pl.BlockSpec((None, tq, LANES), lambda t: (prev(t)[0], prev(t)[2], prev(t)[1])),
        out_shape=jax.ShapeDtypeStruct((nb, seq, BRANCH_WIDTH), BF16),
        scratch_shapes=[pltpu.VMEM((2, tq, seq), F32), pltpu.VMEM((2, tq, seq), F32)],
        compiler_params=_cparams(("arbitrary",)),
        name="diff_attn",
    )(proj3, proj3, proj3, *lam_params, subln)


def _na_bias_tables(rpb):
    rows = NA_QROWS + NA_KROWS
    qc = jnp.arange(GRID_W)
    kc = jnp.arange(GRID_W)
    cs = jnp.clip(qc - NA_KW // 2, 0, GRID_W - NA_KW)
    col_ok = (kc[None, :] >= cs[:, None]) & (kc[None, :] < cs[:, None] + NA_KW)
    col_off = jnp.clip(kc[None, :] - qc[:, None], -(NA_KW - 1), NA_KW - 1) + NA_KW - 1
    kh = min(NA_KH, rows)
    r0 = jnp.array([0, NA_QROWS, rows - NA_QROWS])
    ws = jnp.array([0, NA_QROWS - kh // 2, rows - NA_KROWS])
    r = r0[:, None] + jnp.arange(NA_QROWS)[None, :]
    kr = ws[:, None] + jnp.arange(NA_KROWS)[None, :]
    rs = jnp.clip(r - kh // 2, 0, rows - kh)
    row_ok = (kr[:, None, :] >= rs[:, :, None]) & (kr[:, None, :] < rs[:, :, None] + kh)
    row_off = jnp.clip(kr[:, None, :] - r[:, :, None] + NA_KH - 1, 0, 2 * NA_KH - 2)
    vals = rpb.astype(F32)[:, row_off[:, :, None, :, None], col_off[None, None, :, None, :]]
    ok = row_ok[:, :, None, :, None] & col_ok[None, None, :, None, :]
    return jnp.where(ok[None], vals, NEG).reshape(
        B_HEADS, 3, NA_QROWS * GRID_W, NA_KROWS * GRID_W)


def _na_kernel(q_ref, k_ref, v_ref, bias_ref, o_ref, s_even_ref, s_odd_ref, *, rows, nrb):
    t = pl.program_id(0)
    nkeys = NA_KROWS * GRID_W
    dn = (((1,), (1,)), ((), ()))

    def key_start(i):
        ws = jnp.clip(i * NA_QROWS - NA_KH // 2, 0, rows - NA_KROWS)
        return pl.multiple_of(ws * GRID_W, GRID_W)

    def score(s_ref):
        i = jnp.minimum(t, pl.num_programs(0) - 2) % nrb
        kind = jnp.where(i == 0, 0, jnp.where(i == nrb - 1, 2, 1))
        kw = k_ref[pl.ds(key_start(i), nkeys), :]
        q = q_ref[...]
        lo = _first_head_lanes(q.shape)
        zero = jnp.zeros_like(q)
        for e in range(2):
            qe = jnp.where(lo, q, zero) if e == 0 else jnp.where(lo, zero, q)
            s_ref[e] = lax.dot_general(qe, kw, dn, preferred_element_type=F32) + bias_ref[e, kind]

    def finish(s_ref):
        i = jnp.maximum(t - 1, 0) % nrb
        vw = v_ref[pl.ds(key_start(i), nkeys), :]
        v_ones = jnp.concatenate([vw, jnp.ones_like(vw)], axis=1)
        outs = []
        for e in range(2):
            s = s_ref[e]
            ex = jnp.exp(s - jnp.max(s, axis=-1, keepdims=True))
            a = jnp.dot(ex.astype(BF16), v_ones, preferred_element_type=F32)
            outs.append(a[:, :LANES] / a[:, LANES:])
        o_ref[...] = jnp.where(_first_head_lanes(outs[0].shape), outs[0], outs[1]).astype(BF16)

    @pl.when(t == 0)
    def _():
        s_odd_ref[...] = jnp.zeros_like(s_odd_ref)

    @pl.when(t % 2 == 0)
    def _():
        score(s_even_ref)
        finish(s_odd_ref)

    @pl.when(t % 2 == 1)
    def _():
        score(s_odd_ref)
        finish(s_even_ref)


def _neighborhood_attention(proj3, bias):
    nb, seq, _ = proj3.shape
    rows = seq // GRID_W
    nrb = rows // NA_QROWS
    nq = NA_QROWS * GRID_W
    nk = NA_KROWS * GRID_W
    tiles = (B_HEADS // 2) * nb * nrb

    def split(t):
        return t // (nb * nrb), (t // nrb) % nb, t % nrb

    def cur(t):
        return split(jnp.minimum(t, tiles - 1))

    def prev(t):
        return split(jnp.maximum(t - 1, 0))

    return pl.pallas_call(
        functools.partial(_na_kernel, rows=rows, nrb=nrb),
        grid=(tiles + 1,),
        in_specs=[
            pl.BlockSpec((None, nq, LANES), lambda t: (cur(t)[1], cur(t)[2], _col_block("BQ") + cur(t)[0])),
            pl.BlockSpec((None, seq, LANES), lambda t: (cur(t)[1], 0, _col_block("BK") + cur(t)[0])),
            pl.BlockSpec((None, seq, LANES), lambda t: (prev(t)[1], 0, _col_block("BV") + prev(t)[0])),
            pl.BlockSpec((2, 3, nq, nk), lambda t: (cur(t)[0], 0, 0, 0)),
        ],
        out_specs=pl.BlockSpec((None, nq, LANES), lambda t: (prev(t)[1], prev(t)[2], prev(t)[0])),
        out_shape=jax.ShapeDtypeStruct((nb, seq, BRANCH_WIDTH), BF16),
        scratch_shapes=[pltpu.VMEM((2, nq, nk), F32), pltpu.VMEM((2, nq, nk), F32)],
        compiler_params=_cparams(("arbitrary",)),
        name="nbr_attn",
    )(proj3, proj3, proj3, bias)


def _softmax_av(s, v):
    m = jnp.max(s, axis=-1, keepdims=True)
    e = jnp.exp(s - m)
    v_ones = jnp.concatenate([v, jnp.ones_like(v)], axis=1)
    acc = jnp.dot(e.astype(BF16), v_ones, preferred_element_type=F32)
    l = acc[:, LANES:]
    return acc[:, :LANES] / l, m + jnp.log(l)


def _dil_kernel(q0_ref, q1_ref, q2_ref, k0_ref, k1_ref, k2_ref, v0_ref, v1_ref, v2_ref, o_ref,
                o0_ref, l0_ref, o0r_ref, l0r_ref, o1_ref, l1_ref, o2_ref, l2_ref,
                band_ref, band4_ref, band16_ref, *, seq, tile):
    scale = C_HEAD_DIM ** -0.5
    dn = (((1,), (1,)), ((), ()))
    run = tile // C_TOP_DIL
    n_tiles = seq // tile
    sub16 = seq // C_TOP_DIL
    hw = C_HALF_WINDOW

    full_win = C_QBLK + 2 * hw
    qi = lax.broadcasted_iota(jnp.int32, (C_QBLK, full_win), 0)
    kj = lax.broadcasted_iota(jnp.int32, (C_QBLK, full_win), 1)
    for lead in range(3):
        band_ref[lead] = jnp.where(jnp.abs(kj - qi - lead * hw) <= hw, 0.0, NEG)
    pack = band16_ref.shape[0] // sub16
    qi = lax.broadcasted_iota(jnp.int32, band16_ref.shape, 0)
    kj = lax.broadcasted_iota(jnp.int32, band16_ref.shape, 1)
    same_class = (qi // sub16) == (kj // sub16)
    band16_ref[...] = jnp.where(same_class & (jnp.abs(kj - qi) <= hw), 0.0, NEG)
    qr = lax.broadcasted_iota(jnp.int32, (C_QBLK, 2 * C_QBLK), 0)
    kc = lax.broadcasted_iota(jnp.int32, (C_QBLK, 2 * C_QBLK), 1)
    q_pos = 4 * (qr % 32) + qr // 32
    k_pos = 4 * (kc % 64) + kc // 64
    for lead in range(3):
        band4_ref[lead] = jnp.where(jnp.abs(q_pos + 64 * lead - k_pos) <= hw, 0.0, NEG)

    def attend0(n):
        a = n * C_QBLK
        ws = jnp.clip(a - hw, 0, seq - full_win)
        rows = pl.ds(pl.multiple_of(a, C_QBLK), C_QBLK)
        kw = k0_ref[pl.ds(pl.multiple_of(ws, hw), full_win), :]
        vw = v0_ref[pl.ds(pl.multiple_of(ws, hw), full_win), :]
        s = lax.dot_general(q0_ref[rows, :], kw, dn, preferred_element_type=F32) * scale
        o0_ref[rows, :], l0_ref[rows, :] = _softmax_av(s + band_ref[(a - ws) // hw], vw)

    def blocks0(it, carry):
        for u in range(C_UNROLL):
            attend0(it * C_UNROLL + u)
        return carry

    lax.fori_loop(0, seq // (C_QBLK * C_UNROLL), blocks0, 0)

    def tile_row(j):
        return (j // run) * tile + j % run

    def attend1(n):
        rho4 = n // (sub16 // 32)
        j0 = (n % (sub16 // 32)) * 32
        jw = jnp.clip(j0 - 16, 0, sub16 - 64)
        class_row = [(4 * c + rho4) * run for c in range(4)]
        q_base = tile_row(j0)
        k_base = [tile_row(jw + 16 * p) for p in range(4)]
        q_rows = [pl.ds(pl.multiple_of(q_base + class_row[c], 32), 32) for c in range(4)]
        k_rows = [pl.ds(pl.multiple_of(k_base[p] + class_row[c], 16), 16)
                  for c in range(4) for p in range(4)]
        q = jnp.concatenate([q1_ref[r, :] for r in q_rows], axis=0)
        kw = jnp.concatenate([k1_ref[r, :] for r in k_rows], axis=0)
        vw = jnp.concatenate([v1_ref[r, :] for r in k_rows], axis=0)
        s = lax.dot_general(q, kw, dn, preferred_element_type=F32) * scale
        o, lse = _softmax_av(s + band4_ref[(j0 - jw) // 16], vw)
        for c, r in enumerate(q_rows):
            o1_ref[r, :] = o[c * 32:(c + 1) * 32]
            l1_ref[r, :] = lse[c * 32:(c + 1) * 32]

    def blocks1(it, carry):
        for u in range(C_UNROLL):
            attend1(it * C_UNROLL + u)
        return carry

    lax.fori_loop(0, seq // (C_QBLK * C_UNROLL), blocks1, 0)

    def attend2(n):
        rows = [pl.ds(pl.multiple_of(t * tile + (n * pack + u) * run, run), run)
                for u in range(pack) for t in range(n_tiles)]
        q = jnp.concatenate([q2_ref[r, :] for r in rows], axis=0)
        kw = jnp.concatenate([k2_ref[r, :] for r in rows], axis=0)
        vw = jnp.concatenate([v2_ref[r, :] for r in rows], axis=0)
        s = lax.dot_general(q, kw, dn, preferred_element_type=F32) * scale
        o, lse = _softmax_av(s + band16_ref[...], vw)
        for i, r in enumerate(rows):
            o2_ref[r, :] = o[i * run:(i + 1) * run]
            l2_ref[r, :] = lse[i * run:(i + 1) * run]

    unroll2 = min(C_UNROLL, C_TOP_DIL // pack)

    def blocks2(it, carry):
        for u in range(unroll2):
            attend2(it * unroll2 + u)
        return carry

    lax.fori_loop(0, C_TOP_DIL // pack // unroll2, blocks2, 0)

    for t in range(n_tiles):
        for rho in range(C_TOP_DIL):
            dst = slice(t * tile + rho * run, t * tile + (rho + 1) * run)
            o0r_ref[dst, :] = o0_ref[pl.ds(t * tile + rho, run, stride=C_TOP_DIL), :]
            l0r_ref[dst, :] = l0_ref[pl.ds(t * tile + rho, run, stride=C_TOP_DIL), :]
    l0, l1, l2 = l0r_ref[...], l1_ref[...], l2_ref[...]
    l_max = jnp.maximum(jnp.maximum(l0, l1), l2)
    w0, w1, w2 = jnp.exp(l0 - l_max), jnp.exp(l1 - l_max), jnp.exp(l2 - l_max)
    o0_ref[...] = (w0 * o0r_ref[...] + w1 * o1_ref[...] + w2 * o2_ref[...]) * (1.0 / (w0 + w1 + w2))

    for t in range(n_tiles):
        def to_token_order(j, carry, t=t):
            src = o0_ref[pl.ds(t * tile + j, C_TOP_DIL, stride=run), :]
            dst = pl.ds(pl.multiple_of(t * tile + j * C_TOP_DIL, C_TOP_DIL), C_TOP_DIL)
            o_ref[dst, :] = src.astype(BF16)
            return carry

        lax.fori_loop(0, run, to_token_order, 0, unroll=8)


def _dilated_mixture(proj3, tile):
    nb, seq, _ = proj3.shape
    assert seq % tile == 0 and tile % (C_TOP_DIL * 32) == 0 and seq >= C_QBLK + 2 * C_HALF_WINDOW
    assert (seq // C_TOP_DIL) % 64 == 0 and seq % (C_QBLK * C_UNROLL) == 0
    side2 = max(seq // C_TOP_DIL, C_QBLK + 2 * C_HALF_WINDOW)

    def spec(kind, g):
        cb0 = _col_block(f"{kind}{g}")
        return pl.BlockSpec((None, seq, LANES), lambda b, h, cb0=cb0: (b, 0, cb0 + h))

    in_specs = [spec(kind, g) for kind in ("CQ", "CK", "CV") for g in range(len(C_DILATIONS))]
    return pl.pallas_call(
        functools.partial(_dil_kernel, seq=seq, tile=tile),
        grid=(nb, C_HEADS_PER_GROUP),
        in_specs=in_specs,
        out_specs=pl.BlockSpec((None, seq, LANES), lambda b, h: (b, 0, h)),
        out_shape=jax.ShapeDtypeStruct((nb, seq, BRANCH_WIDTH), BF16),
        scratch_shapes=(
            [pltpu.VMEM((seq, LANES), F32)] * 8
            + [pltpu.VMEM((3, C_QBLK, C_QBLK + 2 * C_HALF_WINDOW), F32)] * 2
            + [pltpu.VMEM((side2, side2), F32)]),
        compiler_params=_cparams(("arbitrary", "arbitrary")),
        name="dil_attn",
    )(*([proj3] * 9))


def _tail_kernel(ya_ref, yb_ref, yc_ref, z0_ref, z1_ref, z2_ref, g0a_ref, g0b_ref, g1a_ref,
                 g1b_ref, g2a_ref, g2b_ref, x_ref, gate_ref, wbr_ref, wout_ref, o_ref):
    halves = [None, None]
    branches = ((ya_ref, z0_ref, (g0a_ref, g0b_ref)), (yb_ref, z1_ref, (g1a_ref, g1b_ref)),
                (yc_ref, z2_ref, (g2a_ref, g2b_ref)))
    for i, (y_ref, z_ref, g_refs) in enumerate(branches):
        yz = (y_ref[...].astype(F32) * z_ref[...].astype(F32)).astype(BF16)
        for half, g_ref in enumerate(g_refs):
            cols = slice(half * PROJ_TILE, (half + 1) * PROJ_TILE)
            u = g_ref[...].astype(F32) * jnp.dot(yz, wbr_ref[i, :, cols], preferred_element_type=F32)
            halves[half] = u if halves[half] is None else halves[half] + u
    merged = jnp.concatenate(halves, axis=1).astype(BF16)
    out = jnp.dot(merged, wout_ref[...], preferred_element_type=F32)
    o_ref[...] = x_ref[...] + gate_ref[0] * out


def _tail(ya, yb, yc, proj2, x2, mod3, w_br_bf, w_out_bf, seq, tm):
    ntok = x2.shape[0]
    tps = seq // tm
    y_spec = pl.BlockSpec((None, tm, BRANCH_WIDTH), lambda i: (i // tps, i % tps, 0))

    def tile_spec(name):
        return pl.BlockSpec((tm, PROJ_TILE), lambda i, s=SLOT[name]: (i, s))

    z_specs = [tile_spec(f"Z{t}") for t in range(3)]
    g_specs = [tile_spec(f"G{t}{half}") for t in range(3) for half in "ab"]
    return pl.pallas_call(
        _tail_kernel,
        grid=(ntok // tm,),
        in_specs=[y_spec, y_spec, y_spec, *z_specs, *g_specs,
                  pl.BlockSpec((tm, D_MODEL), lambda i: (i, 0)),
                  pl.BlockSpec((1, 1, D_MODEL), lambda i: (i // tps, 0, 2)),
                  pl.BlockSpec((3, BRANCH_WIDTH, D_MODEL), lambda i: (0, 0, 0)),
                  pl.BlockSpec((D_MODEL, D_MODEL), lambda i: (0, 0))],
        out_specs=pl.BlockSpec((tm, D_MODEL), lambda i: (i, 0)),
        out_shape=jax.ShapeDtypeStruct((ntok, D_MODEL), F32),
        compiler_params=_cparams(("arbitrary",)),
        name="tail",
    )(ya, yb, yc, *([proj2] * 9), x2, mod3, w_br_bf, w_out_bf)


def _rope_tables(seq, tile):
    def base(d):
        inv = ROPE_THETA ** (-jnp.arange(0, d, 2, dtype=F32) / d)
        ang = jnp.arange(seq, dtype=F32)[:, None] * inv[None, :]
        return jnp.cos(ang), jnp.sin(ang)

    cos_a, sin_a = base(A_QK_DIM)
    ca = jnp.tile(cos_a, (1, 4))
    sa = jnp.concatenate([-sin_a, -sin_a, sin_a, sin_a], axis=-1)
    cos_c, sin_c = base(C_HEAD_DIM)
    cc = jnp.tile(cos_c, (1, 2))
    sc = jnp.concatenate([-sin_c, sin_c], axis=-1)

    def residue_major(tab):
        t4 = tab.reshape(seq // tile, tile // C_TOP_DIL, C_TOP_DIL, LANES)
        return t4.transpose(0, 2, 1, 3).reshape(seq, LANES)

    return ca, sa, cc, sc, residue_major(cc), residue_major(sc)


def _to_stored_columns(w):
    lead = w.shape[:-1]
    tiles = {name: w[..., i * PROJ_TILE:(i + 1) * PROJ_TILE] for i, name in enumerate(TILE_NAMES)}
    for name in ("AQ", "AK"):
        t = tiles[name].reshape(*lead, A_HEADS, 2, 2, A_QK_DIM // 2)
        tiles[name] = jnp.swapaxes(t, -3, -2).reshape(*lead, PROJ_TILE)
    return jnp.concatenate([tiles[n] for grp in PROJ_GROUPS for n in grp], axis=-1)


def _proj_gains(qn_a, kn_a, qn_b, kn_b, qn_c, kn_c):
    per_tile = {"AQ": jnp.tile(qn_a, PROJ_TILE // A_QK_DIM), "AK": jnp.tile(kn_a, PROJ_TILE // A_QK_DIM),
                "BQ": jnp.tile(qn_b, PROJ_TILE // B_HEAD_DIM), "BK": jnp.tile(kn_b, PROJ_TILE // B_HEAD_DIM)}
    for g in range(len(C_DILATIONS)):
        per_tile[f"CQ{g}"] = jnp.tile(qn_c, PROJ_TILE // C_HEAD_DIM)
        per_tile[f"CK{g}"] = jnp.tile(kn_c, PROJ_TILE // C_HEAD_DIM)
    ones = jnp.ones((PROJ_TILE,), F32)
    full = jnp.concatenate([per_tile.get(name, ones).astype(F32) for name in TILE_NAMES])
    return _to_stored_columns(full).reshape(len(PROJ_GROUPS), 1, GROUP_WIDTH)


def _proj_token_tile(seq):
    return min(512, seq)


def _tail_token_tile(seq):
    return min(512, seq)


def _encoder_layer(x, mod3, layer_idx, tabs, ln_g, w_in_bf, gains, lam_params, subln, na_bias,
                   w_br_bf, w_out_bf):
    nb, seq, _ = x.shape
    x2 = x.reshape(nb * seq, D_MODEL)
    proj2 = _projection(x2, mod3, ln_g, w_in_bf, gains, tabs, seq, _proj_token_tile(seq))
    proj3 = proj2.reshape(nb, seq, IN_WIDTH)
    lam_init = 0.8 - 0.6 * math.exp(-0.3 * layer_idx)
    tq = A_SCORE_BYTES // (2 * 2 * seq * 4)
    ya = _diff_attention(proj3, lam_params, subln, lam_init, tq=tq)
    yb = _neighborhood_attention(proj3, na_bias)
    yc = _dilated_mixture(proj3, _proj_token_tile(seq))
    y2 = _tail(ya, yb, yc, proj2, x2, mod3, w_br_bf, w_out_bf, seq, _tail_token_tile(seq))
    return y2.reshape(nb, seq, D_MODEL)


def kernel(x_prompt, x_sample, c_prompt, c_sample, ln_g, w_ada, b_ada, w_in, qn_a, kn_a, lam_q1, lam_k1, lam_q2, lam_k2, subln_a, qn_b, kn_b, rpb_b, qn_c, kn_c, w_br, w_out):
    depth = w_in.shape[0]
    n_prompt = c_prompt.shape[0]
    mod_all = _modulation(jnp.concatenate([c_prompt, c_sample], axis=0), w_ada, b_ada)
    w_in_bf = _to_stored_columns(w_in.astype(BF16))
    w_br_bf = w_br.astype(BF16)
    w_out_bf = w_out.astype(BF16)
    gains = [_proj_gains(qn_a[l], kn_a[l], qn_b[l], kn_b[l], qn_c[l], kn_c[l]) for l in range(depth)]
    na_bias = [_na_bias_tables(rpb_b[l]) for l in range(depth)]

    def run(x, mod):
        nb, seq, _ = x.shape
        assert seq % (NA_QROWS * GRID_W) == 0 and seq // GRID_W >= NA_KROWS
        tabs = _rope_tables(seq, _proj_token_tile(seq))
        for l in range(depth):
            lam_params = tuple(p[l].reshape(1, A_QK_DIM) for p in (lam_q1, lam_k1, lam_q2, lam_k2))
            x = _encoder_layer(x, mod[l].reshape(nb, 1, 3 * D_MODEL), l, tabs,
                               ln_g[l].reshape(1, D_MODEL), w_in_bf[l], gains[l], lam_params,
                               subln_a[l].reshape(1, LANES), na_bias[l], w_br_bf[l], w_out_bf[l])
        return x

    y_prompt = run(x_prompt, mod_all[:, :n_prompt])
    y_sample = run(x_sample, mod_all[:, n_prompt:])
    return (y_prompt, y_sample)
```

```python
import functools
import math

import jax
import jax.numpy as jnp
from jax import lax
from jax.experimental import pallas as pl
from jax.experimental.pallas import tpu as pltpu

F32 = jnp.float32
BF16 = jnp.bfloat16

D_MODEL = 1024
GRID_W = 64
BRANCH_WIDTH = 512
ROPE_THETA = 10000.0
EPS = 1e-6
NEG = -1e30
A_QK_DIM = 64
A_HEADS = 4
B_HEAD_DIM = 64
B_HEADS = 8
NA_KH = 8
NA_KW = 16
C_DILATIONS = (1, 4, 16)
C_HALF_WINDOW = 64
C_HEAD_DIM = 128
C_HEADS_PER_GROUP = 4
IN_WIDTH = 12288
LANES = 128

PROJ_TILE = 512
TILE_NAMES = ("AQ", "AK", "AV", "BQ", "BK", "BV", "CQ0", "CQ1", "CQ2", "CK0", "CK1", "CK2",
              "CV0", "CV1", "CV2", "Z0", "Z1", "Z2", "G0a", "G0b", "G1a", "G1b", "G2a", "G2b")
PROJ_GROUPS = (("AQ", "AK", "BQ", "BK", "Z0", "G0a", "AV", "BV"),
               ("CQ0", "CK0", "CQ1", "Z1", "G0b", "G1a", "G1b", "CV0"),
               ("CK1", "CQ2", "CK2", "Z2", "G2a", "G2b", "CV1", "CV2"))
C_TOP_DIL = C_DILATIONS[-1]
RESIDUE_MAJOR_TILES = tuple(f"C{kind}{g}" for kind in "QKV" for g in (1, 2))
GROUP_TILES = len(PROJ_GROUPS[0])
GROUP_WIDTH = GROUP_TILES * PROJ_TILE
SLOT = {name: i for i, name in enumerate(n for grp in PROJ_GROUPS for n in grp)}


def _col_block(name, sub=0):
    return SLOT[name] * (PROJ_TILE // LANES) + sub

VMEM_LIMIT = 56 * 1024 * 1024

NA_QROWS = 8
NA_KROWS = 16
A_SCORE_BYTES = 16 * 1024 * 1024
C_QBLK = 128
C_UNROLL = 8


def _cparams(sem):
    return pltpu.CompilerParams(dimension_semantics=sem, vmem_limit_bytes=VMEM_LIMIT)


def _sigmoid(x):
    return 1.0 / (1.0 + jnp.exp(-x))


def _mod_kernel(c_ref, w_ref, b_ref, o_ref):
    c = c_ref[...]
    o_ref[0] = jnp.dot(c * _sigmoid(c), w_ref[0], preferred_element_type=F32) + b_ref[0]


def _modulation(c_all, w_ada, b_ada):
    depth = w_ada.shape[0]
    nb = c_all.shape[0]
    return pl.pallas_call(
        _mod_kernel,
        grid=(depth, 3),
        in_specs=[
            pl.BlockSpec((nb, D_MODEL), lambda l, j: (0, 0)),
            pl.BlockSpec((1, D_MODEL, D_MODEL), lambda l, j: (l, 0, j)),
            pl.BlockSpec((1, 1, D_MODEL), lambda l, j: (l, 0, j)),
        ],
        out_specs=pl.BlockSpec((1, nb, D_MODEL), lambda l, j: (l, 0, j)),
        out_shape=jax.ShapeDtypeStruct((depth, nb, 3 * D_MODEL), F32),
        compiler_params=_cparams(("arbitrary", "arbitrary")),
        name="adaln_mod",
    )(c_all, w_ada, b_ada.reshape(depth, 1, 3 * D_MODEL))


def _first_map_lanes(shape):
    return (lax.broadcasted_iota(jnp.int32, shape, 1) & 32) == 0


def _first_head_lanes(shape):
    return lax.broadcasted_iota(jnp.int32, shape, 1) < 64


def _rms_halves(xb, first):
    sq = xb * xb
    s_a = jnp.sum(jnp.where(first, sq, 0.0), axis=-1, keepdims=True)
    s_b = jnp.sum(jnp.where(first, 0.0, sq), axis=-1, keepdims=True)
    return lax.rsqrt(jnp.where(first, s_a, s_b) * (1.0 / 64.0) + EPS)


def _rms128(xb):
    return lax.rsqrt(jnp.mean(xb * xb, axis=-1, keepdims=True) + EPS)


def _rope(xb, cos, sin_signed):
    return xb * cos + pltpu.roll(xb, 64, 1) * sin_signed


def _tile_epilogue(name, xb, gain, tabs):
    ca_ref, sa_ref, cc_ref, sc_ref, ccr_ref, scr_ref = tabs
    if name in RESIDUE_MAJOR_TILES:
        cc_ref, sc_ref = ccr_ref, scr_ref
    if name == "AQ":
        rms = _rms_halves(xb, _first_map_lanes(xb.shape))
        return _rope(xb * (rms * gain), ca_ref[...], sa_ref[...]) * 0.125
    if name == "AK":
        rms = _rms_halves(xb, _first_map_lanes(xb.shape))
        return _rope(xb * (rms * gain), ca_ref[...], sa_ref[...])
    if name == "BQ":
        return xb * gain * (_rms_halves(xb, _first_head_lanes(xb.shape)) * 0.125)
    if name == "BK":
        return xb * gain * _rms_halves(xb, _first_head_lanes(xb.shape))
    if name[:2] in ("CQ", "CK"):
        return _rope(xb * gain, cc_ref[...], sc_ref[...]) * _rms128(xb)
    if name[0] == "Z":
        return xb * _sigmoid(xb)
    if name[0] == "G":
        return _sigmoid(xb)
    return xb


def _proj_kernel(x_ref, mod_ref, lng_ref, w_ref, gain_ref, ca_ref, sa_ref, cc_ref, sc_ref,
                 ccr_ref, scr_ref, o_ref, h_ref, hr_ref, hstage_ref, acc_ref):
    grp = pl.program_id(1)
    tm = h_ref.shape[0]
    run = tm // C_TOP_DIL

    @pl.when(grp == 0)
    def _():
        x = x_ref[...]
        ms = jnp.mean(x * x, axis=-1, keepdims=True)
        y = x * lax.rsqrt(ms + EPS) * lng_ref[...]
        mod = mod_ref[0]
        shift = mod[:, :D_MODEL]
        scale = mod[:, D_MODEL:2 * D_MODEL]
        h = y * (1.0 + scale) + shift
        h_ref[...] = h.astype(BF16)
        for cb in range(D_MODEL // LANES):
            cols = slice(cb * LANES, (cb + 1) * LANES)
            hstage_ref[cb] = h[:, cols]
            for rho in range(C_TOP_DIL):
                hr_ref[rho * run:(rho + 1) * run, cols] = hstage_ref[
                    cb, pl.ds(rho, run, stride=C_TOP_DIL), :].astype(BF16)

    tabs = (ca_ref, sa_ref, cc_ref, sc_ref, ccr_ref, scr_ref)
    for gi, names in enumerate(PROJ_GROUPS):
        @pl.when(grp == gi)
        def _(names=names):
            gain = gain_ref[0]
            for t, name in enumerate(names):
                acc = acc_ref.at[t % 2]
                lhs_ref = hr_ref if name in RESIDUE_MAJOR_TILES else h_ref
                acc[...] = jnp.dot(lhs_ref[...], w_ref[:, t * PROJ_TILE:(t + 1) * PROJ_TILE],
                                   preferred_element_type=F32)
                for cb in range(PROJ_TILE // LANES):
                    sl = slice(cb * LANES, (cb + 1) * LANES)
                    out = slice(t * PROJ_TILE + cb * LANES, t * PROJ_TILE + (cb + 1) * LANES)
                    o_ref[:, out] = _tile_epilogue(name, acc[:, sl], gain[:, out], tabs).astype(BF16)


def _projection(x2, mod3, ln_g, w_in_bf, gains, tabs, seq, tm):
    ntok = x2.shape[0]
    tps = seq // tm
    tab_spec = pl.BlockSpec((tm, LANES), lambda i, j: (i % tps, 0))
    return pl.pallas_call(
        _proj_kernel,
        grid=(ntok // tm, len(PROJ_GROUPS)),
        in_specs=[
            pl.BlockSpec((tm, D_MODEL), lambda i, j: (i, 0)),
            pl.BlockSpec((1, 1, 3 * D_MODEL), lambda i, j: (i // tps, 0, 0)),
            pl.BlockSpec((1, D_MODEL), lambda i, j: (0, 0)),
            pl.BlockSpec((D_MODEL, GROUP_WIDTH), lambda i, j: (0, j)),
            pl.BlockSpec((1, 1, GROUP_WIDTH), lambda i, j: (j, 0, 0)),
            *([tab_spec] * len(tabs)),
        ],
        out_specs=pl.BlockSpec((tm, GROUP_WIDTH), lambda i, j: (i, j)),
        out_shape=jax.ShapeDtypeStruct((ntok, IN_WIDTH), BF16),
        scratch_shapes=[pltpu.VMEM((tm, D_MODEL), BF16), pltpu.VMEM((tm, D_MODEL), BF16),
                        pltpu.VMEM((D_MODEL // LANES, tm, LANES), F32),
                        pltpu.VMEM((2, tm, PROJ_TILE), F32)],
        compiler_params=_cparams(("arbitrary", "arbitrary")),
        name="in_proj",
    )(x2, mod3, ln_g, w_in_bf, gains, *tabs)


def _diff_attn_kernel(q_ref, k_ref, v_ref, lq1_ref, lk1_ref, lq2_ref, lk2_ref, sub_ref, o_ref,
                      s_even_ref, s_odd_ref, *, lam_init):
    t = pl.program_id(0)
    dn = (((1,), (1,)), ((), ()))

    def score(s_ref):
        q = q_ref[...]
        k = k_ref[...]
        lo = _first_map_lanes(q.shape)
        zero = jnp.zeros_like(q)
        s_ref[0] = lax.dot_general(jnp.where(lo, q, zero), k, dn, preferred_element_type=F32)
        s_ref[1] = lax.dot_general(jnp.where(lo, zero, q), k, dn, preferred_element_type=F32)

    def finish(s_ref):
        lam = (jnp.exp(jnp.sum(lq1_ref[...] * lk1_ref[...], axis=-1, keepdims=True))
               - jnp.exp(jnp.sum(lq2_ref[...] * lk2_ref[...], axis=-1, keepdims=True)) + lam_init)
        v = v_ref[...]
        v_ones = jnp.concatenate([v, jnp.ones_like(v)], axis=1)
        prods = []
        for m in range(2):
            s = s_ref[m]
            e = jnp.exp(s - jnp.max(s, axis=-1, keepdims=True))
            prods.append(jnp.dot(e.astype(BF16), v_ones, preferred_element_type=F32))
        (a0, a1) = prods
        o = a0[:, :LANES] / a0[:, LANES:] - lam * (a1[:, :LANES] / a1[:, LANES:])
        ms = jnp.mean(o * o, axis=-1, keepdims=True)
        o_ref[...] = (o * lax.rsqrt(ms + EPS) * sub_ref[...] * (1.0 - lam_init)).astype(BF16)

    @pl.when(t == 0)
    def _():
        s_odd_ref[...] = jnp.zeros_like(s_odd_ref)

    @pl.when(t % 2 == 0)
    def _():
        score(s_even_ref)
        finish(s_odd_ref)

    @pl.when(t % 2 == 1)
    def _():
        score(s_odd_ref)
        finish(s_even_ref)


def _diff_attention(proj3, lam_params, subln, lam_init, tq):
    nb, seq, _ = proj3.shape
    nq = seq // tq
    tiles = nb * A_HEADS * nq

    def split(t):
        return t // (A_HEADS * nq), (t // nq) % A_HEADS, t % nq

    def cur(t):
        return split(jnp.minimum(t, tiles - 1))

    def prev(t):
        return split(jnp.maximum(t - 1, 0))

    vec = pl.BlockSpec((1, A_QK_DIM), lambda t: (0, 0))
    return pl.pallas_call(
        functools.partial(_diff_attn_kernel, lam_init=lam_init),
        grid=(tiles + 1,),
        in_specs=[
            pl.BlockSpec((None, tq, LANES), lambda t: (cur(t)[0], cur(t)[2], _col_block("AQ") + cur(t)[1])),
            pl.BlockSpec((None, seq, LANES), lambda t: (cur(t)[0], 0, _col_block("AK") + cur(t)[1])),
            pl.BlockSpec((None, seq, LANES), lambda t: (prev(t)[0], 0, _col_block("AV") + prev(t)[1])),
            vec, vec, vec, vec,
            pl.BlockSpec((1, LANES), lambda t: (0, 0)),
        ],
        out_specs=pl.BlockSpec((None, tq, LANES), lambda t: (prev(t)[0], prev(t)[2], prev(t)[1])),
        out_shape=jax.ShapeDtypeStruct((nb, seq, BRANCH_WIDTH), BF16),
        scratch_shapes=[pltpu.VMEM((2, tq, seq), F32), pltpu.VMEM((2, tq, seq), F32)],
        compiler_params=_cparams(("arbitrary",)),
        name="diff_attn",
    )(proj3, proj3, proj3, *lam_params, subln)


def _na_bias_tables(rpb):
    rows = NA_QROWS + NA_KROWS
    qc = jnp.arange(GRID_W)
    kc = jnp.arange(GRID_W)
    cs = jnp.clip(qc - NA_KW // 2, 0, GRID_W - NA_KW)
    col_ok = (kc[None, :] >= cs[:, None]) & (kc[None, :] < cs[:, None] + NA_KW)
    col_off = jnp.clip(kc[None, :] - qc[:, None], -(NA_KW - 1), NA_KW - 1) + NA_KW - 1
    cols = jnp.where(col_ok, rpb.astype(F32)[:, :, col_off], NEG)
    pad = NA_KROWS
    cols = jnp.pad(cols, ((0, 0), (pad, pad), (0, 0), (0, 0)), constant_values=NEG)
    kh = min(NA_KH, rows)
    tables = []
    for r0, ws in ((0, 0), (NA_QROWS, NA_QROWS - kh // 2), (rows - NA_QROWS, rows - NA_KROWS)):
        slabs = []
        for qr in range(NA_QROWS):
            r = r0 + qr
            rs = min(max(r - kh // 2, 0), rows - kh)
            first = ws - r + NA_KH - 1 + pad
            slab = cols[:, first:first + NA_KROWS]
            kr = ws + jnp.arange(NA_KROWS)
            row_ok = (kr >= rs) & (kr < rs + kh)
            slabs.append(jnp.where(row_ok[None, :, None, None], slab, NEG))
        t = jnp.stack(slabs, axis=1)
        tables.append(t.transpose(0, 1, 3, 2, 4).reshape(
            B_HEADS, NA_QROWS * GRID_W, NA_KROWS * GRID_W))
    return jnp.stack(tables)


def _na_kernel(q_ref, k_ref, v_ref, bias_ref, o_ref, s_even_ref, s_odd_ref, *, rows, nrb):
    t = pl.program_id(0)
    nkeys = NA_KROWS * GRID_W
    dn = (((1,), (1,)), ((), ()))

    def key_start(i):
        ws = jnp.clip(i * NA_QROWS - NA_KH // 2, 0, rows - NA_KROWS)
        return pl.multiple_of(ws * GRID_W, GRID_W)

    def score(s_ref):
        i = jnp.minimum(t, pl.num_programs(0) - 2) % nrb
        kind = jnp.where(i == 0, 0, jnp.where(i == nrb - 1, 2, 1))
        kw = k_ref[pl.ds(key_start(i), nkeys), :]
        q = q_ref[...]
        lo = _first_head_lanes(q.shape)
        zero = jnp.zeros_like(q)
        for e in range(2):
            qe = jnp.where(lo, q, zero) if e == 0 else jnp.where(lo, zero, q)
            s_ref[e] = lax.dot_general(qe, kw, dn, preferred_element_type=F32) + bias_ref[kind, e]

    def finish(s_ref):
        i = jnp.maximum(t - 1, 0) % nrb
        vw = v_ref[pl.ds(key_start(i), nkeys), :]
        v_ones = jnp.concatenate([vw, jnp.ones_like(vw)], axis=1)
        outs = []
        for e in range(2):
            s = s_ref[e]
            ex = jnp.exp(s - jnp.max(s, axis=-1, keepdims=True))
            a = jnp.dot(ex.astype(BF16), v_ones, preferred_element_type=F32)
            outs.append(a[:, :LANES] / a[:, LANES:])
        o_ref[...] = jnp.where(_first_head_lanes(outs[0].shape), outs[0], outs[1]).astype(BF16)

    @pl.when(t == 0)
    def _():
        s_odd_ref[...] = jnp.zeros_like(s_odd_ref)

    @pl.when(t % 2 == 0)
    def _():
        score(s_even_ref)
        finish(s_odd_ref)

    @pl.when(t % 2 == 1)
    def _():
        score(s_odd_ref)
        finish(s_even_ref)


def _neighborhood_attention(proj3, bias):
    nb, seq, _ = proj3.shape
    rows = seq // GRID_W
    nrb = rows // NA_QROWS
    nq = NA_QROWS * GRID_W
    nk = NA_KROWS * GRID_W
    tiles = (B_HEADS // 2) * nb * nrb

    def split(t):
        return t // (nb * nrb), (t // nrb) % nb, t % nrb

    def cur(t):
        return split(jnp.minimum(t, tiles - 1))

    def prev(t):
        return split(jnp.maximum(t - 1, 0))

    return pl.pallas_call(
        functools.partial(_na_kernel, rows=rows, nrb=nrb),
        grid=(tiles + 1,),
        in_specs=[
            pl.BlockSpec((None, nq, LANES), lambda t: (cur(t)[1], cur(t)[2], _col_block("BQ") + cur(t)[0])),
            pl.BlockSpec((None, seq, LANES), lambda t: (cur(t)[1], 0, _col_block("BK") + cur(t)[0])),
            pl.BlockSpec((None, seq, LANES), lambda t: (prev(t)[1], 0, _col_block("BV") + prev(t)[0])),
            pl.BlockSpec((3, 2, nq, nk), lambda t: (0, cur(t)[0], 0, 0)),
        ],
        out_specs=pl.BlockSpec((None, nq, LANES), lambda t: (prev(t)[1], prev(t)[2], prev(t)[0])),
        out_shape=jax.ShapeDtypeStruct((nb, seq, BRANCH_WIDTH), BF16),
        scratch_shapes=[pltpu.VMEM((2, nq, nk), F32), pltpu.VMEM((2, nq, nk), F32)],
        compiler_params=_cparams(("arbitrary",)),
        name="nbr_attn",
    )(proj3, proj3, proj3, bias)


def _softmax_av(s, v):
    m = jnp.max(s, axis=-1, keepdims=True)
    e = jnp.exp(s - m)
    v_ones = jnp.concatenate([v, jnp.ones_like(v)], axis=1)
    acc = jnp.dot(e.astype(BF16), v_ones, preferred_element_type=F32)
    l = acc[:, LANES:]
    return acc[:, :LANES] / l, m + jnp.log(l)


def _dil_kernel(q0_ref, q1_ref, q2_ref, k0_ref, k1_ref, k2_ref, v0_ref, v1_ref, v2_ref, o_ref,
                o0_ref, l0_ref, o0r_ref, l0r_ref, o1_ref, l1_ref, o2_ref, l2_ref,
                band_ref, band4_ref, band16_ref, *, seq, tile):
    scale = C_HEAD_DIM ** -0.5
    dn = (((1,), (1,)), ((), ()))
    run = tile // C_TOP_DIL
    n_tiles = seq // tile
    sub16 = seq // C_TOP_DIL
    hw = C_HALF_WINDOW

    full_win = C_QBLK + 2 * hw
    qi = lax.broadcasted_iota(jnp.int32, (C_QBLK, full_win), 0)
    kj = lax.broadcasted_iota(jnp.int32, (C_QBLK, full_win), 1)
    for lead in range(3):
        band_ref[lead] = jnp.where(jnp.abs(kj - qi - lead * hw) <= hw, 0.0, NEG)
    pack = band16_ref.shape[0] // sub16
    qi = lax.broadcasted_iota(jnp.int32, band16_ref.shape, 0)
    kj = lax.broadcasted_iota(jnp.int32, band16_ref.shape, 1)
    same_class = (qi // sub16) == (kj // sub16)
    band16_ref[...] = jnp.where(same_class & (jnp.abs(kj - qi) <= hw), 0.0, NEG)
    qr = lax.broadcasted_iota(jnp.int32, (C_QBLK, 2 * C_QBLK), 0)
    kc = lax.broadcasted_iota(jnp.int32, (C_QBLK, 2 * C_QBLK), 1)
    q_pos = 4 * (qr % 32) + qr // 32
    k_pos = 4 * (kc % 64) + kc // 64
    for lead in range(3):
        band4_ref[lead] = jnp.where(jnp.abs(q_pos + 64 * lead - k_pos) <= hw, 0.0, NEG)

    def attend0(n):
        a = n * C_QBLK
        ws = jnp.clip(a - hw, 0, seq - full_win)
        rows = pl.ds(pl.multiple_of(a, C_QBLK), C_QBLK)
        kw = k0_ref[pl.ds(pl.multiple_of(ws, hw), full_win), :]
        vw = v0_ref[pl.ds(pl.multiple_of(ws, hw), full_win), :]
        s = lax.dot_general(q0_ref[rows, :], kw, dn, preferred_element_type=F32) * scale
        o0_ref[rows, :], l0_ref[rows, :] = _softmax_av(s + band_ref[(a - ws) // hw], vw)

    def blocks0(it, carry):
        for u in range(C_UNROLL):
            attend0(it * C_UNROLL + u)
        return carry

    lax.fori_loop(0, seq // (C_QBLK * C_UNROLL), blocks0, 0)

    def tile_row(j):
        return (j // run) * tile + j % run

    def attend1(n):
        rho4 = n // (sub16 // 32)
        j0 = (n % (sub16 // 32)) * 32
        jw = jnp.clip(j0 - 16, 0, sub16 - 64)
        class_row = [(4 * c + rho4) * run for c in range(4)]
        q_base = tile_row(j0)
        k_base = [tile_row(jw + 16 * p) for p in range(4)]
        q_rows = [pl.ds(pl.multiple_of(q_base + class_row[c], 32), 32) for c in range(4)]
        k_rows = [pl.ds(pl.multiple_of(k_base[p] + class_row[c], 16), 16)
                  for c in range(4) for p in range(4)]
        q = jnp.concatenate([q1_ref[r, :] for r in q_rows], axis=0)
        kw = jnp.concatenate([k1_ref[r, :] for r in k_rows], axis=0)
        vw = jnp.concatenate([v1_ref[r, :] for r in k_rows], axis=0)
        s = lax.dot_general(q, kw, dn, preferred_element_type=F32) * scale
        o, lse = _softmax_av(s + band4_ref[(j0 - jw) // 16], vw)
        for c, r in enumerate(q_rows):
            o1_ref[r, :] = o[c * 32:(c + 1) * 32]
            l1_ref[r, :] = lse[c * 32:(c + 1) * 32]

    def blocks1(it, carry):
        for u in range(C_UNROLL):
            attend1(it * C_UNROLL + u)
        return carry

    lax.fori_loop(0, seq // (C_QBLK * C_UNROLL), blocks1, 0)

    def attend2(n):
        rows = [pl.ds(pl.multiple_of(t * tile + (n * pack + u) * run, run), run)
                for u in range(pack) for t in range(n_tiles)]
        q = jnp.concatenate([q2_ref[r, :] for r in rows], axis=0)
        kw = jnp.concatenate([k2_ref[r, :] for r in rows], axis=0)
        vw = jnp.concatenate([v2_ref[r, :] for r in rows], axis=0)
        s = lax.dot_general(q, kw, dn, preferred_element_type=F32) * scale
        o, lse = _softmax_av(s + band16_ref[...], vw)
        for i, r in enumerate(rows):
            o2_ref[r, :] = o[i * run:(i + 1) * run]
            l2_ref[r, :] = lse[i * run:(i + 1) * run]

    unroll2 = min(C_UNROLL, C_TOP_DIL // pack)

    def blocks2(it, carry):
        for u in range(unroll2):
            attend2(it * unroll2 + u)
        return carry

    lax.fori_loop(0, C_TOP_DIL // pack // unroll2, blocks2, 0)

    for t in range(n_tiles):
        for rho in range(C_TOP_DIL):
            dst = slice(t * tile + rho * run, t * tile + (rho + 1) * run)
            o0r_ref[dst, :] = o0_ref[pl.ds(t * tile + rho, run, stride=C_TOP_DIL), :]
            l0r_ref[dst, :] = l0_ref[pl.ds(t * tile + rho, run, stride=C_TOP_DIL), :]
    l0, l1, l2 = l0r_ref[...], l1_ref[...], l2_ref[...]
    l_max = jnp.maximum(jnp.maximum(l0, l1), l2)
    w0, w1, w2 = jnp.exp(l0 - l_max), jnp.exp(l1 - l_max), jnp.exp(l2 - l_max)
    o0_ref[...] = (w0 * o0r_ref[...] + w1 * o1_ref[...] + w2 * o2_ref[...]) * (1.0 / (w0 + w1 + w2))

    for t in range(n_tiles):
        def to_token_order(j, carry, t=t):
            src = o0_ref[pl.ds(t * tile + j, C_TOP_DIL, stride=run), :]
            dst = pl.ds(pl.multiple_of(t * tile + j * C_TOP_DIL, C_TOP_DIL), C_TOP_DIL)
            o_ref[dst, :] = src.astype(BF16)
            return carry

        lax.fori_loop(0, run, to_token_order, 0, unroll=8)


def _dilated_mixture(proj3, tile):
    nb, seq, _ = proj3.shape
    assert seq % tile == 0 and tile % (C_TOP_DIL * 32) == 0 and seq >= C_QBLK + 2 * C_HALF_WINDOW
    assert (seq // C_TOP_DIL) % 64 == 0 and seq % (C_QBLK * C_UNROLL) == 0
    side2 = max(seq // C_TOP_DIL, C_QBLK + 2 * C_HALF_WINDOW)

    def spec(kind, g):
        cb0 = _col_block(f"{kind}{g}")
        return pl.BlockSpec((None, seq, LANES), lambda b, h, cb0=cb0: (b, 0, cb0 + h))

    in_specs = [spec(kind, g) for kind in ("CQ", "CK", "CV") for g in range(len(C_DILATIONS))]
    return pl.pallas_call(
        functools.partial(_dil_kernel, seq=seq, tile=tile),
        grid=(nb, C_HEADS_PER_GROUP),
        in_specs=in_specs,
        out_specs=pl.BlockSpec((None, seq, LANES), lambda b, h: (b, 0, h)),
        out_shape=jax.ShapeDtypeStruct((nb, seq, BRANCH_WIDTH), BF16),
        scratch_shapes=(
            [pltpu.VMEM((seq, LANES), F32)] * 8
            + [pltpu.VMEM((3, C_QBLK, C_QBLK + 2 * C_HALF_WINDOW), F32)] * 2
            + [pltpu.VMEM((side2, side2), F32)]),
        compiler_params=_cparams(("arbitrary", "arbitrary")),
        name="dil_attn",
    )(*([proj3] * 9))


def _tail_kernel(ya_ref, yb_ref, yc_ref, z0_ref, z1_ref, z2_ref, g0a_ref, g0b_ref, g1a_ref,
                 g1b_ref, g2a_ref, g2b_ref, x_ref, gate_ref, wbr_ref, wout_ref, o_ref):
    halves = [None, None]
    branches = ((ya_ref, z0_ref, (g0a_ref, g0b_ref)), (yb_ref, z1_ref, (g1a_ref, g1b_ref)),
                (yc_ref, z2_ref, (g2a_ref, g2b_ref)))
    for i, (y_ref, z_ref, g_refs) in enumerate(branches):
        yz = (y_ref[...].astype(F32) * z_ref[...].astype(F32)).astype(BF16)
        for half, g_ref in enumerate(g_refs):
            cols = slice(half * PROJ_TILE, (half + 1) * PROJ_TILE)
            u = g_ref[...].astype(F32) * jnp.dot(yz, wbr_ref[i, :, cols], preferred_element_type=F32)
            halves[half] = u if halves[half] is None else halves[half] + u
    merged = jnp.concatenate(halves, axis=1).astype(BF16)
    out = jnp.dot(merged, wout_ref[...], preferred_element_type=F32)
    o_ref[...] = x_ref[...] + gate_ref[0] * out


def _tail(ya, yb, yc, proj2, x2, mod3, w_br_bf, w_out_bf, seq, tm):
    ntok = x2.shape[0]
    tps = seq // tm
    y_spec = pl.BlockSpec((None, tm, BRANCH_WIDTH), lambda i: (i // tps, i % tps, 0))

    def tile_spec(name):
        return pl.BlockSpec((tm, PROJ_TILE), lambda i, s=SLOT[name]: (i, s))

    z_specs = [tile_spec(f"Z{t}") for t in range(3)]
    g_specs = [tile_spec(f"G{t}{half}") for t in range(3) for half in "ab"]
    return pl.pallas_call(
        _tail_kernel,
        grid=(ntok // tm,),
        in_specs=[y_spec, y_spec, y_spec, *z_specs, *g_specs,
                  pl.BlockSpec((tm, D_MODEL), lambda i: (i, 0)),
                  pl.BlockSpec((1, 1, D_MODEL), lambda i: (i // tps, 0, 2)),
                  pl.BlockSpec((3, BRANCH_WIDTH, D_MODEL), lambda i: (0, 0, 0)),
                  pl.BlockSpec((D_MODEL, D_MODEL), lambda i: (0, 0))],
        out_specs=pl.BlockSpec((tm, D_MODEL), lambda i: (i, 0)),
        out_shape=jax.ShapeDtypeStruct((ntok, D_MODEL), F32),
        compiler_params=_cparams(("arbitrary",)),
        name="tail",
    )(ya, yb, yc, *([proj2] * 9), x2, mod3, w_br_bf, w_out_bf)


def _rope_tables(seq, tile):
    def base(d):
        inv = ROPE_THETA ** (-jnp.arange(0, d, 2, dtype=F32) / d)
        ang = jnp.arange(seq, dtype=F32)[:, None] * inv[None, :]
        return jnp.cos(ang), jnp.sin(ang)

    cos_a, sin_a = base(A_QK_DIM)
    ca = jnp.tile(cos_a, (1, 4))
    sa = jnp.concatenate([-sin_a, -sin_a, sin_a, sin_a], axis=-1)
    cos_c, sin_c = base(C_HEAD_DIM)
    cc = jnp.tile(cos_c, (1, 2))
    sc = jnp.concatenate([-sin_c, sin_c], axis=-1)

    def residue_major(tab):
        t4 = tab.reshape(seq // tile, tile // C_TOP_DIL, C_TOP_DIL, LANES)
        return t4.transpose(0, 2, 1, 3).reshape(seq, LANES)

    return ca, sa, cc, sc, residue_major(cc), residue_major(sc)


def _to_stored_columns(w):
    lead = w.shape[:-1]
    tiles = {name: w[..., i * PROJ_TILE:(i + 1) * PROJ_TILE] for i, name in enumerate(TILE_NAMES)}
    for name in ("AQ", "AK"):
        t = tiles[name].reshape(*lead, A_HEADS, 2, 2, A_QK_DIM // 2)
        tiles[name] = jnp.swapaxes(t, -3, -2).reshape(*lead, PROJ_TILE)
    return jnp.concatenate([tiles[n] for grp in PROJ_GROUPS for n in grp], axis=-1)


def _proj_gains(qn_a, kn_a, qn_b, kn_b, qn_c, kn_c):
    per_tile = {"AQ": jnp.tile(qn_a, PROJ_TILE // A_QK_DIM), "AK": jnp.tile(kn_a, PROJ_TILE // A_QK_DIM),
                "BQ": jnp.tile(qn_b, PROJ_TILE // B_HEAD_DIM), "BK": jnp.tile(kn_b, PROJ_TILE // B_HEAD_DIM)}
    for g in range(len(C_DILATIONS)):
        per_tile[f"CQ{g}"] = jnp.tile(qn_c, PROJ_TILE // C_HEAD_DIM)
        per_tile[f"CK{g}"] = jnp.tile(kn_c, PROJ_TILE // C_HEAD_DIM)
    ones = jnp.ones((PROJ_TILE,), F32)
    full = jnp.concatenate([per_tile.get(name, ones).astype(F32) for name in TILE_NAMES])
    return _to_stored_columns(full).reshape(len(PROJ_GROUPS), 1, GROUP_WIDTH)


def _proj_token_tile(seq):
    return min(512, seq)


def _tail_token_tile(seq):
    return min(512, seq)


def _encoder_layer(x, mod3, layer_idx, tabs, ln_g, w_in_bf, gains, lam_params, subln, na_bias,
                   w_br_bf, w_out_bf):
    nb, seq, _ = x.shape
    x2 = x.reshape(nb * seq, D_MODEL)
    proj2 = _projection(x2, mod3, ln_g, w_in_bf, gains, tabs, seq, _proj_token_tile(seq))
    proj3 = proj2.reshape(nb, seq, IN_WIDTH)
    lam_init = 0.8 - 0.6 * math.exp(-0.3 * layer_idx)
    tq = A_SCORE_BYTES // (2 * 2 * seq * 4)
    ya = _diff_attention(proj3, lam_params, subln, lam_init, tq=tq)
    yb = _neighborhood_attention(proj3, na_bias)
    yc = _dilated_mixture(proj3, _proj_token_tile(seq))
    y2 = _tail(ya, yb, yc, proj2, x2, mod3, w_br_bf, w_out_bf, seq, _tail_token_tile(seq))
    return y2.reshape(nb, seq, D_MODEL)


def kernel(x_prompt, x_sample, c_prompt, c_sample, ln_g, w_ada, b_ada, w_in, qn_a, kn_a, lam_q1, lam_k1, lam_q2, lam_k2, subln_a, qn_b, kn_b, rpb_b, qn_c, kn_c, w_br, w_out):
    depth = w_in.shape[0]
    n_prompt = c_prompt.shape[0]
    mod_all = _modulation(jnp.concatenate([c_prompt, c_sample], axis=0), w_ada, b_ada)
    w_in_bf = _to_stored_columns(w_in.astype(BF16))
    w_br_bf = w_br.astype(BF16)
    w_out_bf = w_out.astype(BF16)
    gains = [_proj_gains(qn_a[l], kn_a[l], qn_b[l], kn_b[l], qn_c[l], kn_c[l]) for l in range(depth)]
    na_bias = [_na_bias_tables(rpb_b[l]) for l in range(depth)]

    def run(x, mod):
        nb, seq, _ = x.shape
        assert seq % (NA_QROWS * GRID_W) == 0 and seq // GRID_W >= NA_KROWS
        tabs = _rope_tables(seq, _proj_token_tile(seq))
        for l in range(depth):
            lam_params = tuple(p[l].reshape(1, A_QK_DIM) for p in (lam_q1, lam_k1, lam_q2, lam_k2))
            x = _encoder_layer(x, mod[l].reshape(nb, 1, 3 * D_MODEL), l, tabs,
                               ln_g[l].reshape(1, D_MODEL), w_in_bf[l], gains[l], lam_params,
                               subln_a[l].reshape(1, LANES), na_bias[l], w_br_bf[l], w_out_bf[l])
        return x

    y_prompt = run(x_prompt, mod_all[:, :n_prompt])
    y_sample = run(x_sample, mod_all[:, n_prompt:])
    return (y_prompt, y_sample)
```

```python
import functools
import math

import jax
import jax.numpy as jnp
from jax import lax
from jax.experimental import pallas as pl
from jax.experimental.pallas import tpu as pltpu

F32 = jnp.float32
BF16 = jnp.bfloat16

D_MODEL = 1024
GRID_W = 64
BRANCH_WIDTH = 512
ROPE_THETA = 10000.0
EPS = 1e-6
NEG = -1e30
A_QK_DIM = 64
A_HEADS = 4
B_HEAD_DIM = 64
B_HEADS = 8
NA_KH = 8
NA_KW = 16
C_DILATIONS = (1, 4, 16)
C_HALF_WINDOW = 64
C_HEAD_DIM = 128
C_HEADS_PER_GROUP = 4
IN_WIDTH = 12288
LANES = 128

PROJ_TILE = 512
TILE_NAMES = ("AQ", "AK", "AV", "BQ", "BK", "BV", "CQ0", "CQ1", "CQ2", "CK0", "CK1", "CK2",
              "CV0", "CV1", "CV2", "Z0", "Z1", "Z2", "G0a", "G0b", "G1a", "G1b", "G2a", "G2b")
PROJ_GROUPS = (("AQ", "AK", "BQ", "BK", "Z0", "G0a", "AV", "BV"),
               ("CQ0", "CK0", "CQ1", "Z1", "G0b", "G1a", "G1b", "CV0"),
               ("CK1", "CQ2", "CK2", "Z2", "G2a", "G2b", "CV1", "CV2"))
C_TOP_DIL = C_DILATIONS[-1]
RESIDUE_MAJOR_TILES = tuple(f"C{kind}{g}" for kind in "QKV" for g in (1, 2))
GROUP_TILES = len(PROJ_GROUPS[0])
GROUP_WIDTH = GROUP_TILES * PROJ_TILE
SLOT = {name: i for i, name in enumerate(n for grp in PROJ_GROUPS for n in grp)}


def _col_block(name, sub=0):
    return SLOT[name] * (PROJ_TILE // LANES) + sub

VMEM_LIMIT = 56 * 1024 * 1024

NA_QROWS = 8
NA_KROWS = 16
A_SCORE_BYTES = 16 * 1024 * 1024
C_QBLK = 128
C_UNROLL = 8
C_G1_RUNS = C_TOP_DIL // C_DILATIONS[1]
C_G1_QRUN = C_QBLK // C_G1_RUNS
C_G1_REACH = C_HALF_WINDOW // C_DILATIONS[1]
C_G1_KRUN = C_G1_QRUN + 2 * C_G1_REACH
BF16_ROWS = 16
QK64_SCALE = A_QK_DIM ** -0.5


def _cparams(sem):
    return pltpu.CompilerParams(dimension_semantics=sem, vmem_limit_bytes=VMEM_LIMIT)


def _sigmoid(x):
    return 1.0 / (1.0 + jnp.exp(-x))


def _mod_kernel(c_ref, w_ref, b_ref, o_ref):
    c = c_ref[...]
    o_ref[0] = jnp.dot(c * _sigmoid(c), w_ref[0], preferred_element_type=F32) + b_ref[0]


def _modulation(c_all, w_ada, b_ada):
    depth = w_ada.shape[0]
    nb = c_all.shape[0]
    return pl.pallas_call(
        _mod_kernel,
        grid=(depth, 3),
        in_specs=[
            pl.BlockSpec((nb, D_MODEL), lambda l, j: (0, 0)),
            pl.BlockSpec((1, D_MODEL, D_MODEL), lambda l, j: (l, 0, j)),
            pl.BlockSpec((1, 1, D_MODEL), lambda l, j: (l, 0, j)),
        ],
        out_specs=pl.BlockSpec((1, nb, D_MODEL), lambda l, j: (l, 0, j)),
        out_shape=jax.ShapeDtypeStruct((depth, nb, 3 * D_MODEL), F32),
        compiler_params=_cparams(("arbitrary", "arbitrary")),
        name="adaln_mod",
    )(c_all, w_ada, b_ada.reshape(depth, 1, 3 * D_MODEL))


def _first_map_lanes(shape):
    return (lax.broadcasted_iota(jnp.int32, shape, 1) & (A_QK_DIM // 2)) == 0


def _first_head_lanes(shape):
    return lax.broadcasted_iota(jnp.int32, shape, 1) < 64


def _rms_halves(xb, first):
    sq = xb * xb
    s_a = jnp.sum(jnp.where(first, sq, 0.0), axis=-1, keepdims=True)
    s_b = jnp.sum(jnp.where(first, 0.0, sq), axis=-1, keepdims=True)
    return lax.rsqrt(jnp.where(first, s_a, s_b) * (1.0 / A_QK_DIM) + EPS)


def _rms128(xb):
    return lax.rsqrt(jnp.mean(xb * xb, axis=-1, keepdims=True) + EPS)


def _rope(xb, cos, sin_signed):
    return xb * cos + pltpu.roll(xb, 64, 1) * sin_signed


def _tile_epilogue(name, xb, gain, tabs):
    ca_ref, sa_ref, cc_ref, sc_ref, ccr_ref, scr_ref = tabs
    if name in RESIDUE_MAJOR_TILES:
        cc_ref, sc_ref = ccr_ref, scr_ref
    if name == "AQ":
        rms = _rms_halves(xb, _first_map_lanes(xb.shape))
        return _rope(xb * (rms * gain), ca_ref[...], sa_ref[...]) * QK64_SCALE
    if name == "AK":
        rms = _rms_halves(xb, _first_map_lanes(xb.shape))
        return _rope(xb * (rms * gain), ca_ref[...], sa_ref[...])
    if name == "BQ":
        return xb * gain * (_rms_halves(xb, _first_head_lanes(xb.shape)) * QK64_SCALE)
    if name == "BK":
        return xb * gain * _rms_halves(xb, _first_head_lanes(xb.shape))
    if name[:2] in ("CQ", "CK"):
        return _rope(xb * gain, cc_ref[...], sc_ref[...]) * _rms128(xb)
    if name[0] == "Z":
        return xb * _sigmoid(xb)
    if name[0] == "G":
        return _sigmoid(xb)
    return xb


def _proj_kernel(x_ref, mod_ref, lng_ref, w_ref, gain_ref, ca_ref, sa_ref, cc_ref, sc_ref,
                 ccr_ref, scr_ref, o_ref, h_ref, hr_ref, hstage_ref, acc_ref):
    grp = pl.program_id(1)
    tm = h_ref.shape[0]
    run = tm // C_TOP_DIL

    @pl.when(grp == 0)
    def _():
        x = x_ref[...]
        ms = jnp.mean(x * x, axis=-1, keepdims=True)
        y = x * lax.rsqrt(ms + EPS) * lng_ref[...]
        mod = mod_ref[0]
        shift = mod[:, :D_MODEL]
        scale = mod[:, D_MODEL:2 * D_MODEL]
        h = y * (1.0 + scale) + shift
        h_ref[...] = h.astype(BF16)
        for cb in range(D_MODEL // LANES):
            hstage_ref[cb] = h[:, cb * LANES:(cb + 1) * LANES]

    def regroup_rows():
        for cb in range(D_MODEL // LANES):
            for rho in range(C_TOP_DIL):
                hr_ref[rho * run:(rho + 1) * run, cb * LANES:(cb + 1) * LANES] = hstage_ref[
                    cb, pl.ds(rho, run, stride=C_TOP_DIL), :].astype(BF16)

    assert not any(name in RESIDUE_MAJOR_TILES for name in PROJ_GROUPS[0])
    tabs = (ca_ref, sa_ref, cc_ref, sc_ref, ccr_ref, scr_ref)
    for gi, names in enumerate(PROJ_GROUPS):
        @pl.when(grp == gi)
        def _(names=names, gi=gi):
            gain = gain_ref[0]
            for t, name in enumerate(names):
                acc = acc_ref.at[t % 2]
                lhs_ref = hr_ref if name in RESIDUE_MAJOR_TILES else h_ref
                acc[...] = jnp.dot(lhs_ref[...], w_ref[:, t * PROJ_TILE:(t + 1) * PROJ_TILE],
                                   preferred_element_type=F32)
                for cb in range(PROJ_TILE // LANES):
                    sl = slice(cb * LANES, (cb + 1) * LANES)
                    out = slice(t * PROJ_TILE + cb * LANES, t * PROJ_TILE + (cb + 1) * LANES)
                    o_ref[:, out] = _tile_epilogue(name, acc[:, sl], gain[:, out], tabs).astype(BF16)
                if gi == 0 and t == 0:
                    regroup_rows()


def _projection(x2, mod3, ln_g, w_in_bf, gains, tabs, seq, tm):
    ntok = x2.shape[0]
    tps = seq // tm
    tab_spec = pl.BlockSpec((tm, LANES), lambda i, j: (i % tps, 0))
    return pl.pallas_call(
        _proj_kernel,
        grid=(ntok // tm, len(PROJ_GROUPS)),
        in_specs=[
            pl.BlockSpec((tm, D_MODEL), lambda i, j: (i, 0)),
            pl.BlockSpec((1, 1, 3 * D_MODEL), lambda i, j: (i // tps, 0, 0)),
            pl.BlockSpec((1, D_MODEL), lambda i, j: (0, 0)),
            pl.BlockSpec((D_MODEL, GROUP_WIDTH), lambda i, j: (0, j)),
            pl.BlockSpec((1, 1, GROUP_WIDTH), lambda i, j: (j, 0, 0)),
            *([tab_spec] * len(tabs)),
        ],
        out_specs=pl.BlockSpec((tm, GROUP_WIDTH), lambda i, j: (i, j)),
        out_shape=jax.ShapeDtypeStruct((ntok, IN_WIDTH), BF16),
        scratch_shapes=[pltpu.VMEM((tm, D_MODEL), BF16), pltpu.VMEM((tm, D_MODEL), BF16),
                        pltpu.VMEM((D_MODEL // LANES, tm, LANES), F32),
                        pltpu.VMEM((2, tm, PROJ_TILE), F32)],
        compiler_params=_cparams(("arbitrary", "arbitrary")),
        name="in_proj",
    )(x2, mod3, ln_g, w_in_bf, gains, *tabs)


def _diff_attn_kernel(q_ref, k_ref, v_ref, lam_ref, sub_ref, o_ref, s_even_ref, s_odd_ref,
                      *, lam_init):
    t = pl.program_id(0)
    dn = (((1,), (1,)), ((), ()))

    def score(s_ref):
        q = q_ref[...]
        k = k_ref[...]
        lo = _first_map_lanes(q.shape)
        zero = jnp.zeros_like(q)
        s_ref[0] = lax.dot_general(jnp.where(lo, q, zero), k, dn, preferred_element_type=F32)
        s_ref[1] = lax.dot_general(jnp.where(lo, zero, q), k, dn, preferred_element_type=F32)

    def finish(s_ref):
        lq1, lk1, lq2, lk2 = (lam_ref[i:i + 1, :] for i in range(4))
        lam = (jnp.exp(jnp.sum(lq1 * lk1, axis=-1, keepdims=True))
               - jnp.exp(jnp.sum(lq2 * lk2, axis=-1, keepdims=True)) + lam_init)
        v = v_ref[...]
        v_ones = jnp.concatenate([v, jnp.ones_like(v)], axis=1)
        prods = []
        for m in range(2):
            s = s_ref[m]
            e = jnp.exp(s - jnp.max(s, axis=-1, keepdims=True))
            prods.append(jnp.dot(e.astype(BF16), v_ones, preferred_element_type=F32))
        (a0, a1) = prods
        o = a0[:, :LANES] / a0[:, LANES:] - lam * (a1[:, :LANES] / a1[:, LANES:])
        ms = jnp.mean(o * o, axis=-1, keepdims=True)
        o_ref[...] = (o * lax.rsqrt(ms + EPS) * sub_ref[...] * (1.0 - lam_init)).astype(BF16)

    @pl.when(t == 0)
    def _():
        s_odd_ref[...] = jnp.zeros_like(s_odd_ref)

    @pl.when(t % 2 == 0)
    def _():
        score(s_even_ref)
        finish(s_odd_ref)

    @pl.when(t % 2 == 1)
    def _():
        score(s_odd_ref)
        finish(s_even_ref)


def _diff_attention(proj3, lam_params, subln, lam_init, tq):
    nb, seq, _ = proj3.shape
    nq = seq // tq
    tiles = nb * A_HEADS * nq

    def split(t):
        return t // (A_HEADS * nq), (t // nq) % A_HEADS, t % nq

    def cur(t):
        return split(jnp.minimum(t, tiles - 1))

    def prev(t):
        return split(jnp.maximum(t - 1, 0))

    lam_spec = pl.BlockSpec((4, A_QK_DIM), lambda t: (0, 0))
    return pl.pallas_call(
        functools.partial(_diff_attn_kernel, lam_init=lam_init),
        grid=(tiles + 1,),
        in_specs=[
            pl.BlockSpec((None, tq, LANES), lambda t: (cur(t)[0], cur(t)[2], _col_block("AQ") + cur(t)[1])),
            pl.BlockSpec((None, seq, LANES), lambda t: (cur(t)[0], 0, _col_block("AK") + cur(t)[1])),
            pl.BlockSpec((None, seq, LANES), lambda t: (prev(t)[0], 0, _col_block("AV") + prev(t)[1])),
            lam_spec,
            pl.BlockSpec((1, LANES), lambda t: (0, 0)),
        ],
        out_specs=pl.BlockSpec((None, tq, LANES), lambda t: (prev(t)[0], prev(t)[2], prev(t)[1])),
        out_shape=jax.ShapeDtypeStruct((nb, seq, BRANCH_WIDTH), BF16),
        scratch_shapes=[pltpu.VMEM((2, tq, seq), F32), pltpu.VMEM((2, tq, seq), F32)],
        compiler_params=_cparams(("arbitrary",)),
        name="diff_attn",
    )(proj3, proj3, proj3, lam_params, subln)


def _na_bias_tables(rpb):
    rows = NA_QROWS + NA_KROWS
    qc = jnp.arange(GRID_W)
    kc = jnp.arange(GRID_W)
    cs = jnp.clip(qc - NA_KW // 2, 0, GRID_W - NA_KW)
    col_ok = (kc[None, :] >= cs[:, None]) & (kc[None, :] < cs[:, None] + NA_KW)
    col_off = jnp.clip(kc[None, :] - qc[:, None], -(NA_KW - 1), NA_KW - 1) + NA_KW - 1
    cols = jnp.where(col_ok, rpb.astype(F32)[:, :, col_off], NEG)
    pad = NA_KROWS
    cols = jnp.pad(cols, ((0, 0), (pad, pad), (0, 0), (0, 0)), constant_values=NEG)
    kh = min(NA_KH, rows)
    tables = []
    for r0, ws in ((0, 0), (NA_QROWS, NA_QROWS - kh // 2), (rows - NA_QROWS, rows - NA_KROWS)):
        slabs = []
        for qr in range(NA_QROWS):
            r = r0 + qr
            rs = min(max(r - kh // 2, 0), rows - kh)
            first = ws - r + NA_KH - 1 + pad
            slab = cols[:, first:first + NA_KROWS]
            kr = ws + jnp.arange(NA_KROWS)
            row_ok = (kr >= rs) & (kr < rs + kh)
            slabs.append(jnp.where(row_ok[None, :, None, None], slab, NEG))
        t = jnp.stack(slabs, axis=1)
        tables.append(t.transpose(0, 1, 3, 2, 4).reshape(
            B_HEADS, NA_QROWS * GRID_W, NA_KROWS * GRID_W))
    return jnp.stack(tables)


def _na_kernel(q_ref, k_ref, v_ref, bias_ref, o_ref, s_even_ref, s_odd_ref, *, rows, nrb):
    t = pl.program_id(0)
    nkeys = NA_KROWS * GRID_W
    dn = (((1,), (1,)), ((), ()))

    def key_start(i):
        ws = jnp.clip(i * NA_QROWS - NA_KH // 2, 0, rows - NA_KROWS)
        return pl.multiple_of(ws * GRID_W, GRID_W)

    def score(s_ref):
        i = jnp.minimum(t, pl.num_programs(0) - 2) % nrb
        kind = jnp.where(i == 0, 0, jnp.where(i == nrb - 1, 2, 1))
        kw = k_ref[pl.ds(key_start(i), nkeys), :]
        q = q_ref[...]
        lo = _first_head_lanes(q.shape)
        zero = jnp.zeros_like(q)
        for e in range(2):
            qe = jnp.where(lo, q, zero) if e == 0 else jnp.where(lo, zero, q)
            s_ref[e] = lax.dot_general(qe, kw, dn, preferred_element_type=F32) + bias_ref[kind, e]

    def finish(s_ref):
        i = jnp.maximum(t - 1, 0) % nrb
        vw = v_ref[pl.ds(key_start(i), nkeys), :]
        v_ones = jnp.concatenate([vw, jnp.ones_like(vw)], axis=1)
        outs = []
        for e in range(2):
            s = s_ref[e]
            ex = jnp.exp(s - jnp.max(s, axis=-1, keepdims=True))
            a = jnp.dot(ex.astype(BF16), v_ones, preferred_element_type=F32)
            outs.append(a[:, :LANES] / a[:, LANES:])
        o_ref[...] = jnp.where(_first_head_lanes(outs[0].shape), outs[0], outs[1]).astype(BF16)

    @pl.when(t == 0)
    def _():
        s_odd_ref[...] = jnp.zeros_like(s_odd_ref)

    @pl.when(t % 2 == 0)
    def _():
        score(s_even_ref)
        finish(s_odd_ref)

    @pl.when(t % 2 == 1)
    def _():
        score(s_odd_ref)
        finish(s_even_ref)


def _neighborhood_attention(proj3, bias):
    nb, seq, _ = proj3.shape
    rows = seq // GRID_W
    nrb = rows // NA_QROWS
    nq = NA_QROWS * GRID_W
    nk = NA_KROWS * GRID_W
    tiles = (B_HEADS // 2) * nb * nrb

    def split(t):
        return t // (nb * nrb), (t // nrb) % nb, t % nrb

    def cur(t):
        return split(jnp.minimum(t, tiles - 1))

    def prev(t):
        return split(jnp.maximum(t - 1, 0))

    return pl.pallas_call(
        functools.partial(_na_kernel, rows=rows, nrb=nrb),
        grid=(tiles + 1,),
        in_specs=[
            pl.BlockSpec((None, nq, LANES), lambda t: (cur(t)[1], cur(t)[2], _col_block("BQ") + cur(t)[0])),
            pl.BlockSpec((None, seq, LANES), lambda t: (cur(t)[1], 0, _col_block("BK") + cur(t)[0])),
            pl.BlockSpec((None, seq, LANES), lambda t: (prev(t)[1], 0, _col_block("BV") + prev(t)[0])),
            pl.BlockSpec((3, 2, nq, nk), lambda t: (0, cur(t)[0], 0, 0)),
        ],
        out_specs=pl.BlockSpec((None, nq, LANES), lambda t: (prev(t)[1], prev(t)[2], prev(t)[0])),
        out_shape=jax.ShapeDtypeStruct((nb, seq, BRANCH_WIDTH), BF16),
        scratch_shapes=[pltpu.VMEM((2, nq, nk), F32), pltpu.VMEM((2, nq, nk), F32)],
        compiler_params=_cparams(("arbitrary",)),
        name="nbr_attn",
    )(proj3, proj3, proj3, bias)


def _softmax_av(s, v):
    m = jnp.max(s, axis=-1, keepdims=True)
    e = jnp.exp(s - m)
    v_ones = jnp.concatenate([v, jnp.ones_like(v)], axis=1)
    acc = jnp.dot(e.astype(BF16), v_ones, preferred_element_type=F32)
    l = acc[:, LANES:]
    return acc[:, :LANES] / l, m + jnp.log(l)


def _dil_kernel(q0_ref, q1_ref, q2_ref, k0_ref, k1_ref, k2_ref, v0_ref, v1_ref, v2_ref, o_ref,
                o0_ref, l0_ref, o0r_ref, l0r_ref, o1_ref, l1_ref, o2_ref, l2_ref,
                band_ref, band4_ref, band16_ref, *, seq, tile):
    scale = C_HEAD_DIM ** -0.5
    dn = (((1,), (1,)), ((), ()))
    run = tile // C_TOP_DIL
    n_tiles = seq // tile
    sub16 = seq // C_TOP_DIL
    hw = C_HALF_WINDOW

    full_win = C_QBLK + 2 * hw
    qi = lax.broadcasted_iota(jnp.int32, (C_QBLK, full_win), 0)
    kj = lax.broadcasted_iota(jnp.int32, (C_QBLK, full_win), 1)
    for lead in range(3):
        band_ref[lead] = jnp.where(jnp.abs(kj - qi - lead * hw) <= hw, 0.0, NEG)
    pack = band16_ref.shape[0] // sub16
    qi = lax.broadcasted_iota(jnp.int32, band16_ref.shape, 0)
    kj = lax.broadcasted_iota(jnp.int32, band16_ref.shape, 1)
    same_class = (qi // sub16) == (kj // sub16)
    band16_ref[...] = jnp.where(same_class & (jnp.abs(kj - qi) <= hw), 0.0, NEG)
    g1_keys = C_G1_RUNS * C_G1_KRUN
    qr = lax.broadcasted_iota(jnp.int32, (C_QBLK, g1_keys), 0)
    kc = lax.broadcasted_iota(jnp.int32, (C_QBLK, g1_keys), 1)
    q_pos = C_G1_RUNS * (qr % C_G1_QRUN) + qr // C_G1_QRUN
    k_pos = C_G1_RUNS * (kc % C_G1_KRUN) + kc // C_G1_KRUN
    for lead in range(3):
        shift = C_G1_RUNS * C_G1_REACH * lead
        band4_ref[lead] = jnp.where(jnp.abs(q_pos + shift - k_pos) <= hw, 0.0, NEG)

    def attend0(n):
        a = n * C_QBLK
        ws = jnp.clip(a - hw, 0, seq - full_win)
        rows = pl.ds(pl.multiple_of(a, C_QBLK), C_QBLK)
        kw = k0_ref[pl.ds(pl.multiple_of(ws, hw), full_win), :]
        vw = v0_ref[pl.ds(pl.multiple_of(ws, hw), full_win), :]
        s = lax.dot_general(q0_ref[rows, :], kw, dn, preferred_element_type=F32) * scale
        o0_ref[rows, :], l0_ref[rows, :] = _softmax_av(s + band_ref[(a - ws) // hw], vw)

    def blocks0(it, carry):
        for u in range(C_UNROLL):
            attend0(it * C_UNROLL + u)
        return carry

    lax.fori_loop(0, seq // (C_QBLK * C_UNROLL), blocks0, 0)

    def tile_row(j):
        return (j // run) * tile + j % run

    def attend1(n):
        blocks_per_class = sub16 // C_G1_QRUN
        rho4 = n // blocks_per_class
        j0 = (n % blocks_per_class) * C_G1_QRUN
        jw = jnp.clip(j0 - C_G1_REACH, 0, sub16 - C_G1_KRUN)
        class_row = [(C_DILATIONS[1] * c + rho4) * run for c in range(C_G1_RUNS)]
        q_base = tile_row(j0)
        k_base = [tile_row(jw + BF16_ROWS * p) for p in range(C_G1_KRUN // BF16_ROWS)]
        q_rows = [pl.ds(pl.multiple_of(q_base + class_row[c], C_G1_QRUN), C_G1_QRUN)
                  for c in range(C_G1_RUNS)]
        k_rows = [pl.ds(pl.multiple_of(base + class_row[c], BF16_ROWS), BF16_ROWS)
                  for c in range(C_G1_RUNS) for base in k_base]
        q = jnp.concatenate([q1_ref[r, :] for r in q_rows], axis=0)
        kw = jnp.concatenate([k1_ref[r, :] for r in k_rows], axis=0)
        vw = jnp.concatenate([v1_ref[r, :] for r in k_rows], axis=0)
        s = lax.dot_general(q, kw, dn, preferred_element_type=F32) * scale
        o, lse = _softmax_av(s + band4_ref[(j0 - jw) // C_G1_REACH], vw)
        for c, r in enumerate(q_rows):
            o1_ref[r, :] = o[c * C_G1_QRUN:(c + 1) * C_G1_QRUN]
            l1_ref[r, :] = lse[c * C_G1_QRUN:(c + 1) * C_G1_QRUN]

    def blocks1(it, carry):
        for u in range(C_UNROLL):
            attend1(it * C_UNROLL + u)
        return carry

    lax.fori_loop(0, seq // (C_QBLK * C_UNROLL), blocks1, 0)

    def attend2(n):
        rows = [pl.ds(pl.multiple_of(t * tile + (n * pack + u) * run, run), run)
                for u in range(pack) for t in range(n_tiles)]
        q = jnp.concatenate([q2_ref[r, :] for r in rows], axis=0)
        kw = jnp.concatenate([k2_ref[r, :] for r in rows], axis=0)
        vw = jnp.concatenate([v2_ref[r, :] for r in rows], axis=0)
        s = lax.dot_general(q, kw, dn, preferred_element_type=F32) * scale
        o, lse = _softmax_av(s + band16_ref[...], vw)
        for i, r in enumerate(rows):
            o2_ref[r, :] = o[i * run:(i + 1) * run]
            l2_ref[r, :] = lse[i * run:(i + 1) * run]

    unroll2 = min(C_UNROLL, C_TOP_DIL // pack)

    def blocks2(it, carry):
        for u in range(unroll2):
            attend2(it * unroll2 + u)
        return carry

    lax.fori_loop(0, C_TOP_DIL // pack // unroll2, blocks2, 0)

    for t in range(n_tiles):
        for rho in range(C_TOP_DIL):
            dst = slice(t * tile + rho * run, t * tile + (rho + 1) * run)
            o0r_ref[dst, :] = o0_ref[pl.ds(t * tile + rho, run, stride=C_TOP_DIL), :]
            l0r_ref[dst, :] = l0_ref[pl.ds(t * tile + rho, run, stride=C_TOP_DIL), :]
    l0, l1, l2 = l0r_ref[...], l1_ref[...], l2_ref[...]
    l_max = jnp.maximum(jnp.maximum(l0, l1), l2)
    w0, w1, w2 = jnp.exp(l0 - l_max), jnp.exp(l1 - l_max), jnp.exp(l2 - l_max)
    o0_ref[...] = (w0 * o0r_ref[...] + w1 * o1_ref[...] + w2 * o2_ref[...]) * (1.0 / (w0 + w1 + w2))

    for t in range(n_tiles):
        def to_token_order(j, carry, t=t):
            src = o0_ref[pl.ds(t * tile + j, C_TOP_DIL, stride=run), :]
            dst = pl.ds(pl.multiple_of(t * tile + j * C_TOP_DIL, C_TOP_DIL), C_TOP_DIL)
            o_ref[dst, :] = src.astype(BF16)
            return carry

        lax.fori_loop(0, run, to_token_order, 0, unroll=8)


def _dilated_mixture(proj3, tile):
    nb, seq, _ = proj3.shape
    assert seq % tile == 0 and tile % (C_TOP_DIL * C_G1_QRUN) == 0
    assert seq >= C_QBLK + 2 * C_HALF_WINDOW and seq % (C_QBLK * C_UNROLL) == 0
    assert (seq // C_TOP_DIL) % C_G1_KRUN == 0
    side2 = max(seq // C_TOP_DIL, C_QBLK + 2 * C_HALF_WINDOW)

    def spec(kind, g):
        cb0 = _col_block(f"{kind}{g}")
        return pl.BlockSpec((None, seq, LANES), lambda b, h, cb0=cb0: (b, 0, cb0 + h))

    in_specs = [spec(kind, g) for kind in ("CQ", "CK", "CV") for g in range(len(C_DILATIONS))]
    return pl.pallas_call(
        functools.partial(_dil_kernel, seq=seq, tile=tile),
        grid=(nb, C_HEADS_PER_GROUP),
        in_specs=in_specs,
        out_specs=pl.BlockSpec((None, seq, LANES), lambda b, h: (b, 0, h)),
        out_shape=jax.ShapeDtypeStruct((nb, seq, BRANCH_WIDTH), BF16),
        scratch_shapes=(
            [pltpu.VMEM((seq, LANES), F32)] * 8
            + [pltpu.VMEM((3, C_QBLK, C_QBLK + 2 * C_HALF_WINDOW), F32)] * 2
            + [pltpu.VMEM((side2, side2), F32)]),
        compiler_params=_cparams(("arbitrary", "arbitrary")),
        name="dil_attn",
    )(*([proj3] * 9))


def _tail_kernel(ya_ref, yb_ref, yc_ref, z0_ref, z1_ref, z2_ref, g0a_ref, g0b_ref, g1a_ref,
                 g1b_ref, g2a_ref, g2b_ref, x_ref, gate_ref, wbr_ref, wout_ref, o_ref):
    halves = [None, None]
    branches = ((ya_ref, z0_ref, (g0a_ref, g0b_ref)), (yb_ref, z1_ref, (g1a_ref, g1b_ref)),
                (yc_ref, z2_ref, (g2a_ref, g2b_ref)))
    for i, (y_ref, z_ref, g_refs) in enumerate(branches):
        yz = (y_ref[...].astype(F32) * z_ref[...].astype(F32)).astype(BF16)
        for half, g_ref in enumerate(g_refs):
            cols = slice(half * PROJ_TILE, (half + 1) * PROJ_TILE)
            u = g_ref[...].astype(F32) * jnp.dot(yz, wbr_ref[i, :, cols], preferred_element_type=F32)
            halves[half] = u if halves[half] is None else halves[half] + u
    merged = jnp.concatenate(halves, axis=1).astype(BF16)
    out = jnp.dot(merged, wout_ref[...], preferred_element_type=F32)
    o_ref[...] = x_ref[...] + gate_ref[0] * out


def _tail(ya, yb, yc, proj2, x2, mod3, w_br_bf, w_out_bf, seq, tm):
    ntok = x2.shape[0]
    tps = seq // tm
    y_spec = pl.BlockSpec((None, tm, BRANCH_WIDTH), lambda i: (i // tps, i % tps, 0))

    def tile_spec(name):
        return pl.BlockSpec((tm, PROJ_TILE), lambda i, s=SLOT[name]: (i, s))

    z_specs = [tile_spec(f"Z{t}") for t in range(3)]
    g_specs = [tile_spec(f"G{t}{half}") for t in range(3) for half in "ab"]
    return pl.pallas_call(
        _tail_kernel,
        grid=(ntok // tm,),
        in_specs=[y_spec, y_spec, y_spec, *z_specs, *g_specs,
                  pl.BlockSpec((tm, D_MODEL), lambda i: (i, 0)),
                  pl.BlockSpec((1, 1, D_MODEL), lambda i: (i // tps, 0, 2)),
                  pl.BlockSpec((3, BRANCH_WIDTH, D_MODEL), lambda i: (0, 0, 0)),
                  pl.BlockSpec((D_MODEL, D_MODEL), lambda i: (0, 0))],
        out_specs=pl.BlockSpec((tm, D_MODEL), lambda i: (i, 0)),
        out_shape=jax.ShapeDtypeStruct((ntok, D_MODEL), F32),
        compiler_params=_cparams(("arbitrary",)),
        name="tail",
    )(ya, yb, yc, *([proj2] * 9), x2, mod3, w_br_bf, w_out_bf)


def _rope_tables(seq, tile):
    def base(d):
        inv = ROPE_THETA ** (-jnp.arange(0, d, 2, dtype=F32) / d)
        ang = jnp.arange(seq, dtype=F32)[:, None] * inv[None, :]
        return jnp.cos(ang), jnp.sin(ang)

    cos_a, sin_a = base(A_QK_DIM)
    ca = jnp.tile(cos_a, (1, 4))
    sa = jnp.concatenate([-sin_a, -sin_a, sin_a, sin_a], axis=-1)
    cos_c, sin_c = base(C_HEAD_DIM)
    cc = jnp.tile(cos_c, (1, 2))
    sc = jnp.concatenate([-sin_c, sin_c], axis=-1)

    def residue_major(tab):
        t4 = tab.reshape(seq // tile, tile // C_TOP_DIL, C_TOP_DIL, LANES)
        return t4.transpose(0, 2, 1, 3).reshape(seq, LANES)

    return ca, sa, cc, sc, residue_major(cc), residue_major(sc)


def _to_stored_columns(w):
    lead = w.shape[:-1]
    tiles = {name: w[..., i * PROJ_TILE:(i + 1) * PROJ_TILE] for i, name in enumerate(TILE_NAMES)}
    for name in ("AQ", "AK"):
        t = tiles[name].reshape(*lead, A_HEADS, 2, 2, A_QK_DIM // 2)
        tiles[name] = jnp.swapaxes(t, -3, -2).reshape(*lead, PROJ_TILE)
    return jnp.concatenate([tiles[n] for grp in PROJ_GROUPS for n in grp], axis=-1)


def _proj_gains(qn_a, kn_a, qn_b, kn_b, qn_c, kn_c):
    per_tile = {"AQ": jnp.tile(qn_a, PROJ_TILE // A_QK_DIM), "AK": jnp.tile(kn_a, PROJ_TILE // A_QK_DIM),
                "BQ": jnp.tile(qn_b, PROJ_TILE // B_HEAD_DIM), "BK": jnp.tile(kn_b, PROJ_TILE // B_HEAD_DIM)}
    for g in range(len(C_DILATIONS)):
        per_tile[f"CQ{g}"] = jnp.tile(qn_c, PROJ_TILE // C_HEAD_DIM)
        per_tile[f"CK{g}"] = jnp.tile(kn_c, PROJ_TILE // C_HEAD_DIM)
    ones = jnp.ones((PROJ_TILE,), F32)
    full = jnp.concatenate([per_tile.get(name, ones).astype(F32) for name in TILE_NAMES])
    return _to_stored_columns(full).reshape(len(PROJ_GROUPS), 1, GROUP_WIDTH)


def _proj_token_tile(seq):
    return min(512, seq)


def _tail_token_tile(seq):
    return min(512, seq)


def _encoder_layer(x, mod3, layer_idx, tabs, ln_g, w_in_bf, gains, lam_params, subln, na_bias,
                   w_br_bf, w_out_bf):
    nb, seq, _ = x.shape
    x2 = x.reshape(nb * seq, D_MODEL)
    proj2 = _projection(x2, mod3, ln_g, w_in_bf, gains, tabs, seq, _proj_token_tile(seq))
    proj3 = proj2.reshape(nb, seq, IN_WIDTH)
    lam_init = 0.8 - 0.6 * math.exp(-0.3 * layer_idx)
    tq = A_SCORE_BYTES // (2 * 2 * seq * 4)
    ya = _diff_attention(proj3, lam_params, subln, lam_init, tq=tq)
    yb = _neighborhood_attention(proj3, na_bias)
    yc = _dilated_mixture(proj3, _proj_token_tile(seq))
    y2 = _tail(ya, yb, yc, proj2, x2, mod3, w_br_bf, w_out_bf, seq, _tail_token_tile(seq))
    return y2.reshape(nb, seq, D_MODEL)


def kernel(x_prompt, x_sample, c_prompt, c_sample, ln_g, w_ada, b_ada, w_in, qn_a, kn_a, lam_q1, lam_k1, lam_q2, lam_k2, subln_a, qn_b, kn_b, rpb_b, qn_c, kn_c, w_br, w_out):
    depth = w_in.shape[0]
    n_prompt = c_prompt.shape[0]
    mod_all = _modulation(jnp.concatenate([c_prompt, c_sample], axis=0), w_ada, b_ada)
    w_in_bf = _to_stored_columns(w_in.astype(BF16))
    w_br_bf = w_br.astype(BF16)
    w_out_bf = w_out.astype(BF16)
    gains = [_proj_gains(qn_a[l], kn_a[l], qn_b[l], kn_b[l], qn_c[l], kn_c[l]) for l in range(depth)]
    na_bias = [_na_bias_tables(rpb_b[l]) for l in range(depth)]

    def run(x, mod):
        nb, seq, _ = x.shape
        assert seq % (NA_QROWS * GRID_W) == 0 and seq // GRID_W >= NA_KROWS
        tabs = _rope_tables(seq, _proj_token_tile(seq))
        for l in range(depth):
            lam_params = jnp.stack([p[l] for p in (lam_q1, lam_k1, lam_q2, lam_k2)])
            x = _encoder_layer(x, mod[l].reshape(nb, 1, 3 * D_MODEL), l, tabs,
                               ln_g[l].reshape(1, D_MODEL), w_in_bf[l], gains[l], lam_params,
                               subln_a[l].reshape(1, LANES), na_bias[l], w_br_bf[l], w_out_bf[l])
        return x

    y_prompt = run(x_prompt, mod_all[:, :n_prompt])
    y_sample = run(x_sample, mod_all[:, n_prompt:])
    return (y_prompt, y_sample)
```

```python
import functools
import math

import jax
import jax.numpy as jnp
from jax import lax
from jax.experimental import pallas as pl
from jax.experimental.pallas import tpu as pltpu

F32 = jnp.float32
BF16 = jnp.bfloat16

D_MODEL = 1024
GRID_W = 64
BRANCH_WIDTH = 512
ROPE_THETA = 10000.0
EPS = 1e-6
NEG = -1e30
A_QK_DIM = 64
A_HEADS = 4
B_HEAD_DIM = 64
B_HEADS = 8
NA_KH = 8
NA_KW = 16
C_DILATIONS = (1, 4, 16)
C_HALF_WINDOW = 64
C_HEAD_DIM = 128
C_HEADS_PER_GROUP = 4
IN_WIDTH = 12288
LANES = 128

PROJ_TILE = 512
TILE_NAMES = ("AQ", "AK", "AV", "BQ", "BK", "BV", "CQ0", "CQ1", "CQ2", "CK0", "CK1", "CK2",
              "CV0", "CV1", "CV2", "Z0", "Z1", "Z2", "G0a", "G0b", "G1a", "G1b", "G2a", "G2b")
PROJ_GROUPS = (("AQ", "AK", "BQ", "BK", "Z0", "G0a", "AV", "BV"),
               ("CQ0", "CK0", "CQ1", "Z1", "G0b", "G1a", "G1b", "CV0"),
               ("CK1", "CQ2", "CK2", "Z2", "G2a", "G2b", "CV1", "CV2"))
C_TOP_DIL = C_DILATIONS[-1]
RESIDUE_MAJOR_TILES = tuple(f"C{kind}{g}" for kind in "QKV" for g in (1, 2))
GROUP_TILES = len(PROJ_GROUPS[0])
GROUP_WIDTH = GROUP_TILES * PROJ_TILE
SLOT = {name: i for i, name in enumerate(n for grp in PROJ_GROUPS for n in grp)}


def _col_block(name, sub=0):
    return SLOT[name] * (PROJ_TILE // LANES) + sub

VMEM_LIMIT = 56 * 1024 * 1024

NA_QROWS = 8
NA_KROWS = 16
A_SCORE_BYTES = 16 * 1024 * 1024
C_QBLK = 128
C_UNROLL = 32
C_G1_RUNS = C_TOP_DIL // C_DILATIONS[1]
C_G1_QRUN = C_QBLK // C_G1_RUNS
C_G1_REACH = C_HALF_WINDOW // C_DILATIONS[1]
C_G1_KRUN = C_G1_QRUN + 2 * C_G1_REACH
BF16_ROWS = 16
QK64_SCALE = A_QK_DIM ** -0.5


def _cparams(sem):
    return pltpu.CompilerParams(dimension_semantics=sem, vmem_limit_bytes=VMEM_LIMIT)


def _sigmoid(x):
    return 1.0 / (1.0 + jnp.exp(-x))


def _mod_kernel(c_ref, w_ref, b_ref, o_ref):
    c = c_ref[...]
    o_ref[0] = jnp.dot(c * _sigmoid(c), w_ref[0], preferred_element_type=F32) + b_ref[0]


def _modulation(c_all, w_ada, b_ada):
    depth = w_ada.shape[0]
    nb = c_all.shape[0]
    return pl.pallas_call(
        _mod_kernel,
        grid=(depth, 3),
        in_specs=[
            pl.BlockSpec((nb, D_MODEL), lambda l, j: (0, 0)),
            pl.BlockSpec((1, D_MODEL, D_MODEL), lambda l, j: (l, 0, j)),
            pl.BlockSpec((1, 1, D_MODEL), lambda l, j: (l, 0, j)),
        ],
        out_specs=pl.BlockSpec((1, nb, D_MODEL), lambda l, j: (l, 0, j)),
        out_shape=jax.ShapeDtypeStruct((depth, nb, 3 * D_MODEL), F32),
        compiler_params=_cparams(("arbitrary", "arbitrary")),
        name="adaln_mod",
    )(c_all, w_ada, b_ada.reshape(depth, 1, 3 * D_MODEL))


def _first_map_lanes(shape):
    return (lax.broadcasted_iota(jnp.int32, shape, 1) & (A_QK_DIM // 2)) == 0


def _first_head_lanes(shape):
    return lax.broadcasted_iota(jnp.int32, shape, 1) < 64


def _rms_halves(xb, first):
    sq = xb * xb
    s_a = jnp.sum(jnp.where(first, sq, 0.0), axis=-1, keepdims=True)
    s_b = jnp.sum(jnp.where(first, 0.0, sq), axis=-1, keepdims=True)
    return lax.rsqrt(jnp.where(first, s_a, s_b) * (1.0 / A_QK_DIM) + EPS)


def _rms128(xb):
    return lax.rsqrt(jnp.mean(xb * xb, axis=-1, keepdims=True) + EPS)


def _rope(xb, cos, sin_signed):
    return xb * cos + pltpu.roll(xb, 64, 1) * sin_signed


def _tile_epilogue(name, xb, gain, tabs):
    ca_ref, sa_ref, cc_ref, sc_ref, ccr_ref, scr_ref = tabs
    if name in RESIDUE_MAJOR_TILES:
        cc_ref, sc_ref = ccr_ref, scr_ref
    if name == "AQ":
        rms = _rms_halves(xb, _first_map_lanes(xb.shape))
        return _rope(xb * (rms * gain), ca_ref[...], sa_ref[...]) * QK64_SCALE
    if name == "AK":
        rms = _rms_halves(xb, _first_map_lanes(xb.shape))
        return _rope(xb * (rms * gain), ca_ref[...], sa_ref[...])
    if name == "BQ":
        return xb * gain * (_rms_halves(xb, _first_head_lanes(xb.shape)) * QK64_SCALE)
    if name == "BK":
        return xb * gain * _rms_halves(xb, _first_head_lanes(xb.shape))
    if name[:2] in ("CQ", "CK"):
        return _rope(xb * gain, cc_ref[...], sc_ref[...]) * _rms128(xb)
    if name[0] == "Z":
        return xb * _sigmoid(xb)
    if name[0] == "G":
        return _sigmoid(xb)
    return xb


def _proj_kernel(x_ref, mod_ref, lng_ref, w_ref, gain_ref, ca_ref, sa_ref, cc_ref, sc_ref,
                 ccr_ref, scr_ref, o_ref, h_ref, hr_ref, hstage_ref, acc_ref):
    grp = pl.program_id(1)
    tm = h_ref.shape[0]
    run = tm // C_TOP_DIL

    @pl.when(grp == 0)
    def _():
        x = x_ref[...]
        ms = jnp.mean(x * x, axis=-1, keepdims=True)
        y = x * lax.rsqrt(ms + EPS) * lng_ref[...]
        mod = mod_ref[0]
        shift = mod[:, :D_MODEL]
        scale = mod[:, D_MODEL:2 * D_MODEL]
        h = y * (1.0 + scale) + shift
        h_ref[...] = h.astype(BF16)
        for cb in range(D_MODEL // LANES):
            hstage_ref[cb] = h[:, cb * LANES:(cb + 1) * LANES]

    def regroup_rows():
        for cb in range(D_MODEL // LANES):
            for rho in range(C_TOP_DIL):
                hr_ref[rho * run:(rho + 1) * run, cb * LANES:(cb + 1) * LANES] = hstage_ref[
                    cb, pl.ds(rho, run, stride=C_TOP_DIL), :].astype(BF16)

    assert not any(name in RESIDUE_MAJOR_TILES for name in PROJ_GROUPS[0])
    tabs = (ca_ref, sa_ref, cc_ref, sc_ref, ccr_ref, scr_ref)
    for gi, names in enumerate(PROJ_GROUPS):
        @pl.when(grp == gi)
        def _(names=names, gi=gi):
            gain = gain_ref[0]
            for t, name in enumerate(names):
                acc = acc_ref.at[t % 2]
                lhs_ref = hr_ref if name in RESIDUE_MAJOR_TILES else h_ref
                acc[...] = jnp.dot(lhs_ref[...], w_ref[:, t * PROJ_TILE:(t + 1) * PROJ_TILE],
                                   preferred_element_type=F32)
                for cb in range(PROJ_TILE // LANES):
                    sl = slice(cb * LANES, (cb + 1) * LANES)
                    out = slice(t * PROJ_TILE + cb * LANES, t * PROJ_TILE + (cb + 1) * LANES)
                    o_ref[:, out] = _tile_epilogue(name, acc[:, sl], gain[:, out], tabs).astype(BF16)
                if gi == 0 and t == 0:
                    regroup_rows()


def _projection(x2, mod3, ln_g, w_in_bf, gains, tabs, seq, tm):
    ntok = x2.shape[0]
    tps = seq // tm
    tab_spec = pl.BlockSpec((tm, LANES), lambda i, j: (i % tps, 0))
    return pl.pallas_call(
        _proj_kernel,
        grid=(ntok // tm, len(PROJ_GROUPS)),
        in_specs=[
            pl.BlockSpec((tm, D_MODEL), lambda i, j: (i, 0)),
            pl.BlockSpec((1, 1, 3 * D_MODEL), lambda i, j: (i // tps, 0, 0)),
            pl.BlockSpec((1, D_MODEL), lambda i, j: (0, 0)),
            pl.BlockSpec((D_MODEL, GROUP_WIDTH), lambda i, j: (0, j)),
            pl.BlockSpec((1, 1, GROUP_WIDTH), lambda i, j: (j, 0, 0)),
            *([tab_spec] * len(tabs)),
        ],
        out_specs=pl.BlockSpec((tm, GROUP_WIDTH), lambda i, j: (i, j)),
        out_shape=jax.ShapeDtypeStruct((ntok, IN_WIDTH), BF16),
        scratch_shapes=[pltpu.VMEM((tm, D_MODEL), BF16), pltpu.VMEM((tm, D_MODEL), BF16),
                        pltpu.VMEM((D_MODEL // LANES, tm, LANES), F32),
                        pltpu.VMEM((2, tm, PROJ_TILE), F32)],
        compiler_params=_cparams(("arbitrary", "arbitrary")),
        name="in_proj",
    )(x2, mod3, ln_g, w_in_bf, gains, *tabs)


def _diff_attn_kernel(q_ref, k_ref, v_ref, lam_ref, sub_ref, o_ref, s_even_ref, s_odd_ref,
                      *, lam_init):
    t = pl.program_id(0)
    dn = (((1,), (1,)), ((), ()))

    def score(s_ref):
        q = q_ref[...]
        k = k_ref[...]
        lo = _first_map_lanes(q.shape)
        zero = jnp.zeros_like(q)
        s_ref[0] = lax.dot_general(jnp.where(lo, q, zero), k, dn, preferred_element_type=F32)
        s_ref[1] = lax.dot_general(jnp.where(lo, zero, q), k, dn, preferred_element_type=F32)

    def finish(s_ref):
        lq1, lk1, lq2, lk2 = (lam_ref[i:i + 1, :] for i in range(4))
        lam = (jnp.exp(jnp.sum(lq1 * lk1, axis=-1, keepdims=True))
               - jnp.exp(jnp.sum(lq2 * lk2, axis=-1, keepdims=True)) + lam_init)
        v = v_ref[...]
        v_ones = jnp.concatenate([v, jnp.ones_like(v)], axis=1)
        prods = []
        for m in range(2):
            s = s_ref[m]
            e = jnp.exp(s - jnp.max(s, axis=-1, keepdims=True))
            prods.append(jnp.dot(e.astype(BF16), v_ones, preferred_element_type=F32))
        (a0, a1) = prods
        o = a0[:, :LANES] / a0[:, LANES:] - lam * (a1[:, :LANES] / a1[:, LANES:])
        ms = jnp.mean(o * o, axis=-1, keepdims=True)
        o_ref[...] = (o * lax.rsqrt(ms + EPS) * sub_ref[...] * (1.0 - lam_init)).astype(BF16)

    @pl.when(t == 0)
    def _():
        s_odd_ref[...] = jnp.zeros_like(s_odd_ref)

    @pl.when(t % 2 == 0)
    def _():
        score(s_even_ref)
        finish(s_odd_ref)

    @pl.when(t % 2 == 1)
    def _():
        score(s_odd_ref)
        finish(s_even_ref)


def _diff_attention(proj3, lam_params, subln, lam_init, tq):
    nb, seq, _ = proj3.shape
    nq = seq // tq
    tiles = nb * A_HEADS * nq

    def split(t):
        return t // (A_HEADS * nq), (t // nq) % A_HEADS, t % nq

    def cur(t):
        return split(jnp.minimum(t, tiles - 1))

    def prev(t):
        return split(jnp.maximum(t - 1, 0))

    lam_spec = pl.BlockSpec((4, A_QK_DIM), lambda t: (0, 0))
    return pl.pallas_call(
        functools.partial(_diff_attn_kernel, lam_init=lam_init),
        grid=(tiles + 1,),
        in_specs=[
            pl.BlockSpec((None, tq, LANES), lambda t: (cur(t)[0], cur(t)[2], _col_block("AQ") + cur(t)[1])),
            pl.BlockSpec((None, seq, LANES), lambda t: (cur(t)[0], 0, _col_block("AK") + cur(t)[1])),
            pl.BlockSpec((None, seq, LANES), lambda t: (prev(t)[0], 0, _col_block("AV") + prev(t)[1])),
            lam_spec,
            pl.BlockSpec((1, LANES), lambda t: (0, 0)),
        ],
        out_specs=pl.BlockSpec((None, tq, LANES), lambda t: (prev(t)[0], prev(t)[2], prev(t)[1])),
        out_shape=jax.ShapeDtypeStruct((nb, seq, BRANCH_WIDTH), BF16),
        scratch_shapes=[pltpu.VMEM((2, tq, seq), F32), pltpu.VMEM((2, tq, seq), F32)],
        compiler_params=_cparams(("arbitrary",)),
        name="diff_attn",
    )(proj3, proj3, proj3, lam_params, subln)


def _na_bias_tables(rpb):
    rows = NA_QROWS + NA_KROWS
    qc = jnp.arange(GRID_W)
    kc = jnp.arange(GRID_W)
    cs = jnp.clip(qc - NA_KW // 2, 0, GRID_W - NA_KW)
    col_ok = (kc[None, :] >= cs[:, None]) & (kc[None, :] < cs[:, None] + NA_KW)
    col_off = jnp.clip(kc[None, :] - qc[:, None], -(NA_KW - 1), NA_KW - 1) + NA_KW - 1
    cols = jnp.where(col_ok, rpb.astype(F32)[:, :, col_off], NEG)
    pad = NA_KROWS
    cols = jnp.pad(cols, ((0, 0), (pad, pad), (0, 0), (0, 0)), constant_values=NEG)
    kh = min(NA_KH, rows)
    tables = []
    for r0, ws in ((0, 0), (NA_QROWS, NA_QROWS - kh // 2), (rows - NA_QROWS, rows - NA_KROWS)):
        slabs = []
        for qr in range(NA_QROWS):
            r = r0 + qr
            rs = min(max(r - kh // 2, 0), rows - kh)
            first = ws - r + NA_KH - 1 + pad
            slab = cols[:, first:first + NA_KROWS]
            kr = ws + jnp.arange(NA_KROWS)
            row_ok = (kr >= rs) & (kr < rs + kh)
            slabs.append(jnp.where(row_ok[None, :, None, None], slab, NEG))
        t = jnp.stack(slabs, axis=1)
        tables.append(t.transpose(0, 1, 3, 2, 4).reshape(
            B_HEADS, NA_QROWS * GRID_W, NA_KROWS * GRID_W))
    return jnp.stack(tables)


def _na_kernel(q_ref, k_ref, v_ref, bias_ref, o_ref, s_even_ref, s_odd_ref, *, rows, nrb):
    t = pl.program_id(0)
    nkeys = NA_KROWS * GRID_W
    dn = (((1,), (1,)), ((), ()))

    def key_start(i):
        ws = jnp.clip(i * NA_QROWS - NA_KH // 2, 0, rows - NA_KROWS)
        return pl.multiple_of(ws * GRID_W, GRID_W)

    def score(s_ref):
        i = jnp.minimum(t, pl.num_programs(0) - 2) % nrb
        kind = jnp.where(i == 0, 0, jnp.where(i == nrb - 1, 2, 1))
        kw = k_ref[pl.ds(key_start(i), nkeys), :]
        q = q_ref[...]
        lo = _first_head_lanes(q.shape)
        zero = jnp.zeros_like(q)
        for e in range(2):
            qe = jnp.where(lo, q, zero) if e == 0 else jnp.where(lo, zero, q)
            s_ref[e] = lax.dot_general(qe, kw, dn, preferred_element_type=F32) + bias_ref[kind, e]

    def finish(s_ref):
        i = jnp.maximum(t - 1, 0) % nrb
        vw = v_ref[pl.ds(key_start(i), nkeys), :]
        v_ones = jnp.concatenate([vw, jnp.ones_like(vw)], axis=1)
        outs = []
        for e in range(2):
            s = s_ref[e]
            ex = jnp.exp(s - jnp.max(s, axis=-1, keepdims=True))
            a = jnp.dot(ex.astype(BF16), v_ones, preferred_element_type=F32)
            outs.append(a[:, :LANES] / a[:, LANES:])
        o_ref[...] = jnp.where(_first_head_lanes(outs[0].shape), outs[0], outs[1]).astype(BF16)

    @pl.when(t == 0)
    def _():
        s_odd_ref[...] = jnp.zeros_like(s_odd_ref)

    @pl.when(t % 2 == 0)
    def _():
        score(s_even_ref)
        finish(s_odd_ref)

    @pl.when(t % 2 == 1)
    def _():
        score(s_odd_ref)
        finish(s_even_ref)


def _neighborhood_attention(proj3, bias):
    nb, seq, _ = proj3.shape
    rows = seq // GRID_W
    nrb = rows // NA_QROWS
    nq = NA_QROWS * GRID_W
    nk = NA_KROWS * GRID_W
    tiles = (B_HEADS // 2) * nb * nrb

    def split(t):
        return t // (nb * nrb), (t // nrb) % nb, t % nrb

    def cur(t):
        return split(jnp.minimum(t, tiles - 1))

    def prev(t):
        return split(jnp.maximum(t - 1, 0))

    return pl.pallas_call(
        functools.partial(_na_kernel, rows=rows, nrb=nrb),
        grid=(tiles + 1,),
        in_specs=[
            pl.BlockSpec((None, nq, LANES), lambda t: (cur(t)[1], cur(t)[2], _col_block("BQ") + cur(t)[0])),
            pl.BlockSpec((None, seq, LANES), lambda t: (cur(t)[1], 0, _col_block("BK") + cur(t)[0])),
            pl.BlockSpec((None, seq, LANES), lambda t: (prev(t)[1], 0, _col_block("BV") + prev(t)[0])),
            pl.BlockSpec((3, 2, nq, nk), lambda t: (0, cur(t)[0], 0, 0)),
        ],
        out_specs=pl.BlockSpec((None, nq, LANES), lambda t: (prev(t)[1], prev(t)[2], prev(t)[0])),
        out_shape=jax.ShapeDtypeStruct((nb, seq, BRANCH_WIDTH), BF16),
        scratch_shapes=[pltpu.VMEM((2, nq, nk), F32), pltpu.VMEM((2, nq, nk), F32)],
        compiler_params=_cparams(("arbitrary",)),
        name="nbr_attn",
    )(proj3, proj3, proj3, bias)


def _softmax_av(s, v):
    m = jnp.max(s, axis=-1, keepdims=True)
    e = jnp.exp(s - m)
    v_ones = jnp.concatenate([v, jnp.ones_like(v)], axis=1)
    acc = jnp.dot(e.astype(BF16), v_ones, preferred_element_type=F32)
    l = acc[:, LANES:]
    return acc[:, :LANES] / l, m + jnp.log(l)


def _dil_kernel(q0_ref, q1_ref, q2_ref, k0_ref, k1_ref, k2_ref, v0_ref, v1_ref, v2_ref, o_ref,
                o0_ref, l0_ref, o0r_ref, l0r_ref, o1_ref, l1_ref, o2_ref, l2_ref,
                band_ref, band4_ref, band16_ref, *, seq, tile):
    scale = C_HEAD_DIM ** -0.5
    dn = (((1,), (1,)), ((), ()))
    run = tile // C_TOP_DIL
    n_tiles = seq // tile
    sub16 = seq // C_TOP_DIL
    hw = C_HALF_WINDOW

    full_win = C_QBLK + 2 * hw
    pack = band16_ref.shape[0] // sub16

    @pl.when((pl.program_id(0) == 0) & (pl.program_id(1) == 0))
    def _():
        qi = lax.broadcasted_iota(jnp.int32, (C_QBLK, full_win), 0)
        kj = lax.broadcasted_iota(jnp.int32, (C_QBLK, full_win), 1)
        for lead in range(3):
            band_ref[lead] = jnp.where(jnp.abs(kj - qi - lead * hw) <= hw, 0.0, NEG)
        qi = lax.broadcasted_iota(jnp.int32, band16_ref.shape, 0)
        kj = lax.broadcasted_iota(jnp.int32, band16_ref.shape, 1)
        same_class = (qi // sub16) == (kj // sub16)
        band16_ref[...] = jnp.where(same_class & (jnp.abs(kj - qi) <= hw), 0.0, NEG)
        g1_keys = C_G1_RUNS * C_G1_KRUN
        qr = lax.broadcasted_iota(jnp.int32, (C_QBLK, g1_keys), 0)
        kc = lax.broadcasted_iota(jnp.int32, (C_QBLK, g1_keys), 1)
        q_pos = C_G1_RUNS * (qr % C_G1_QRUN) + qr // C_G1_QRUN
        k_pos = C_G1_RUNS * (kc % C_G1_KRUN) + kc // C_G1_KRUN
        for lead in range(3):
            shift = C_G1_RUNS * C_G1_REACH * lead
            band4_ref[lead] = jnp.where(jnp.abs(q_pos + shift - k_pos) <= hw, 0.0, NEG)

    def attend0(n):
        a = n * C_QBLK
        ws = jnp.clip(a - hw, 0, seq - full_win)
        rows = pl.ds(pl.multiple_of(a, C_QBLK), C_QBLK)
        kw = k0_ref[pl.ds(pl.multiple_of(ws, hw), full_win), :]
        vw = v0_ref[pl.ds(pl.multiple_of(ws, hw), full_win), :]
        s = lax.dot_general(q0_ref[rows, :], kw, dn, preferred_element_type=F32) * scale
        o0_ref[rows, :], l0_ref[rows, :] = _softmax_av(s + band_ref[(a - ws) // hw], vw)

    n_blocks = seq // C_QBLK
    unroll = min(C_UNROLL, n_blocks)

    def blocks0(it, carry):
        for u in range(unroll):
            attend0(it * unroll + u)
        return carry

    lax.fori_loop(0, n_blocks // unroll, blocks0, 0)

    def tile_row(j):
        return (j // run) * tile + j % run

    def attend1(n):
        blocks_per_class = sub16 // C_G1_QRUN
        rho4 = n // blocks_per_class
        j0 = (n % blocks_per_class) * C_G1_QRUN
        jw = jnp.clip(j0 - C_G1_REACH, 0, sub16 - C_G1_KRUN)
        class_row = [(C_DILATIONS[1] * c + rho4) * run for c in range(C_G1_RUNS)]
        q_base = tile_row(j0)
        k_base = [tile_row(jw + BF16_ROWS * p) for p in range(C_G1_KRUN // BF16_ROWS)]
        q_rows = [pl.ds(pl.multiple_of(q_base + class_row[c], C_G1_QRUN), C_G1_QRUN)
                  for c in range(C_G1_RUNS)]
        k_rows = [pl.ds(pl.multiple_of(base + class_row[c], BF16_ROWS), BF16_ROWS)
                  for c in range(C_G1_RUNS) for base in k_base]
        q = jnp.concatenate([q1_ref[r, :] for r in q_rows], axis=0)
        kw = jnp.concatenate([k1_ref[r, :] for r in k_rows], axis=0)
        vw = jnp.concatenate([v1_ref[r, :] for r in k_rows], axis=0)
        s = lax.dot_general(q, kw, dn, preferred_element_type=F32) * scale
        o, lse = _softmax_av(s + band4_ref[(j0 - jw) // C_G1_REACH], vw)
        for c, r in enumerate(q_rows):
            o1_ref[r, :] = o[c * C_G1_QRUN:(c + 1) * C_G1_QRUN]
            l1_ref[r, :] = lse[c * C_G1_QRUN:(c + 1) * C_G1_QRUN]

    def blocks1(it, carry):
        for u in range(unroll):
            attend1(it * unroll + u)
        return carry

    lax.fori_loop(0, n_blocks // unroll, blocks1, 0)

    def attend2(n):
        rows = [pl.ds(pl.multiple_of(t * tile + (n * pack + u) * run, run), run)
                for u in range(pack) for t in range(n_tiles)]
        q = jnp.concatenate([q2_ref[r, :] for r in rows], axis=0)
        kw = jnp.concatenate([k2_ref[r, :] for r in rows], axis=0)
        vw = jnp.concatenate([v2_ref[r, :] for r in rows], axis=0)
        s = lax.dot_general(q, kw, dn, preferred_element_type=F32) * scale
        o, lse = _softmax_av(s + band16_ref[...], vw)
        for i, r in enumerate(rows):
            o2_ref[r, :] = o[i * run:(i + 1) * run]
            l2_ref[r, :] = lse[i * run:(i + 1) * run]

    unroll2 = min(C_UNROLL, C_TOP_DIL // pack)

    def blocks2(it, carry):
        for u in range(unroll2):
            attend2(it * unroll2 + u)
        return carry

    lax.fori_loop(0, C_TOP_DIL // pack // unroll2, blocks2, 0)

    for t in range(n_tiles):
        for rho in range(C_TOP_DIL):
            dst = slice(t * tile + rho * run, t * tile + (rho + 1) * run)
            o0r_ref[dst, :] = o0_ref[pl.ds(t * tile + rho, run, stride=C_TOP_DIL), :]
            l0r_ref[dst, :] = l0_ref[pl.ds(t * tile + rho, run, stride=C_TOP_DIL), :]
    l0, l1, l2 = l0r_ref[...], l1_ref[...], l2_ref[...]
    l_max = jnp.maximum(jnp.maximum(l0, l1), l2)
    w0, w1, w2 = jnp.exp(l0 - l_max), jnp.exp(l1 - l_max), jnp.exp(l2 - l_max)
    o0_ref[...] = (w0 * o0r_ref[...] + w1 * o1_ref[...] + w2 * o2_ref[...]) * (1.0 / (w0 + w1 + w2))

    for t in range(n_tiles):
        def to_token_order(j, carry, t=t):
            src = o0_ref[pl.ds(t * tile + j, C_TOP_DIL, stride=run), :]
            dst = pl.ds(pl.multiple_of(t * tile + j * C_TOP_DIL, C_TOP_DIL), C_TOP_DIL)
            o_ref[dst, :] = src.astype(BF16)
            return carry

        lax.fori_loop(0, run, to_token_order, 0, unroll=8)


def _dilated_mixture(proj3, tile):
    nb, seq, _ = proj3.shape
    assert seq % tile == 0 and tile % (C_TOP_DIL * C_G1_QRUN) == 0
    assert seq >= C_QBLK + 2 * C_HALF_WINDOW and (seq // C_QBLK) % min(C_UNROLL, seq // C_QBLK) == 0
    assert (seq // C_TOP_DIL) % C_G1_KRUN == 0
    side2 = max(seq // C_TOP_DIL, C_QBLK + 2 * C_HALF_WINDOW)

    def spec(kind, g):
        cb0 = _col_block(f"{kind}{g}")
        return pl.BlockSpec((None, seq, LANES), lambda b, h, cb0=cb0: (b, 0, cb0 + h))

    in_specs = [spec(kind, g) for kind in ("CQ", "CK", "CV") for g in range(len(C_DILATIONS))]
    return pl.pallas_call(
        functools.partial(_dil_kernel, seq=seq, tile=tile),
        grid=(nb, C_HEADS_PER_GROUP),
        in_specs=in_specs,
        out_specs=pl.BlockSpec((None, seq, LANES), lambda b, h: (b, 0, h)),
        out_shape=jax.ShapeDtypeStruct((nb, seq, BRANCH_WIDTH), BF16),
        scratch_shapes=(
            [pltpu.VMEM((seq, LANES), F32)] * 8
            + [pltpu.VMEM((3, C_QBLK, C_QBLK + 2 * C_HALF_WINDOW), F32)] * 2
            + [pltpu.VMEM((side2, side2), F32)]),
        compiler_params=_cparams(("arbitrary", "arbitrary")),
        name="dil_attn",
    )(*([proj3] * 9))


def _tail_kernel(ya_ref, yb_ref, yc_ref, z0_ref, z1_ref, z2_ref, g0a_ref, g0b_ref, g1a_ref,
                 g1b_ref, g2a_ref, g2b_ref, x_ref, gate_ref, wbr_ref, wout_ref, o_ref):
    halves = [None, None]
    branches = ((ya_ref, z0_ref, (g0a_ref, g0b_ref)), (yb_ref, z1_ref, (g1a_ref, g1b_ref)),
                (yc_ref, z2_ref, (g2a_ref, g2b_ref)))
    for i, (y_ref, z_ref, g_refs) in enumerate(branches):
        yz = (y_ref[...].astype(F32) * z_ref[...].astype(F32)).astype(BF16)
        for half, g_ref in enumerate(g_refs):
            cols = slice(half * PROJ_TILE, (half + 1) * PROJ_TILE)
            u = g_ref[...].astype(F32) * jnp.dot(yz, wbr_ref[i, :, cols], preferred_element_type=F32)
            halves[half] = u if halves[half] is None else halves[half] + u
    merged = jnp.concatenate(halves, axis=1).astype(BF16)
    out = jnp.dot(merged, wout_ref[...], preferred_element_type=F32)
    o_ref[...] = x_ref[...] + gate_ref[0] * out


def _tail(ya, yb, yc, proj2, x2, mod3, w_br_bf, w_out_bf, seq, tm):
    ntok = x2.shape[0]
    tps = seq // tm
    y_spec = pl.BlockSpec((None, tm, BRANCH_WIDTH), lambda i: (i // tps, i % tps, 0))

    def tile_spec(name):
        return pl.BlockSpec((tm, PROJ_TILE), lambda i, s=SLOT[name]: (i, s))

    z_specs = [tile_spec(f"Z{t}") for t in range(3)]
    g_specs = [tile_spec(f"G{t}{half}") for t in range(3) for half in "ab"]
    return pl.pallas_call(
        _tail_kernel,
        grid=(ntok // tm,),
        in_specs=[y_spec, y_spec, y_spec, *z_specs, *g_specs,
                  pl.BlockSpec((tm, D_MODEL), lambda i: (i, 0)),
                  pl.BlockSpec((1, 1, D_MODEL), lambda i: (i // tps, 0, 2)),
                  pl.BlockSpec((3, BRANCH_WIDTH, D_MODEL), lambda i: (0, 0, 0)),
                  pl.BlockSpec((D_MODEL, D_MODEL), lambda i: (0, 0))],
        out_specs=pl.BlockSpec((tm, D_MODEL), lambda i: (i, 0)),
        out_shape=jax.ShapeDtypeStruct((ntok, D_MODEL), F32),
        compiler_params=_cparams(("arbitrary",)),
        name="tail",
    )(ya, yb, yc, *([proj2] * 9), x2, mod3, w_br_bf, w_out_bf)


def _rope_tables(seq, tile):
    def base(d):
        inv = ROPE_THETA ** (-jnp.arange(0, d, 2, dtype=F32) / d)
        ang = jnp.arange(seq, dtype=F32)[:, None] * inv[None, :]
        return jnp.cos(ang), jnp.sin(ang)

    cos_a, sin_a = base(A_QK_DIM)
    ca = jnp.tile(cos_a, (1, 4))
    sa = jnp.concatenate([-sin_a, -sin_a, sin_a, sin_a], axis=-1)
    cos_c, sin_c = base(C_HEAD_DIM)
    cc = jnp.tile(cos_c, (1, 2))
    sc = jnp.concatenate([-sin_c, sin_c], axis=-1)

    def residue_major(tab):
        t4 = tab.reshape(seq // tile, tile // C_TOP_DIL, C_TOP_DIL, LANES)
        return t4.transpose(0, 2, 1, 3).reshape(seq, LANES)

    return ca, sa, cc, sc, residue_major(cc), residue_major(sc)


def _to_stored_columns(w):
    lead = w.shape[:-1]
    tiles = {name: w[..., i * PROJ_TILE:(i + 1) * PROJ_TILE] for i, name in enumerate(TILE_NAMES)}
    for name in ("AQ", "AK"):
        t = tiles[name].reshape(*lead, A_HEADS, 2, 2, A_QK_DIM // 2)
        tiles[name] = jnp.swapaxes(t, -3, -2).reshape(*lead, PROJ_TILE)
    return jnp.concatenate([tiles[n] for grp in PROJ_GROUPS for n in grp], axis=-1)


def _proj_gains(qn_a, kn_a, qn_b, kn_b, qn_c, kn_c):
    per_tile = {"AQ": jnp.tile(qn_a, PROJ_TILE // A_QK_DIM), "AK": jnp.tile(kn_a, PROJ_TILE // A_QK_DIM),
                "BQ": jnp.tile(qn_b, PROJ_TILE // B_HEAD_DIM), "BK": jnp.tile(kn_b, PROJ_TILE // B_HEAD_DIM)}
    for g in range(len(C_DILATIONS)):
        per_tile[f"CQ{g}"] = jnp.tile(qn_c, PROJ_TILE // C_HEAD_DIM)
        per_tile[f"CK{g}"] = jnp.tile(kn_c, PROJ_TILE // C_HEAD_DIM)
    ones = jnp.ones((PROJ_TILE,), F32)
    full = jnp.concatenate([per_tile.get(name, ones).astype(F32) for name in TILE_NAMES])
    return _to_stored_columns(full).reshape(len(PROJ_GROUPS), 1, GROUP_WIDTH)


def _proj_token_tile(seq):
    return min(512, seq)


def _tail_token_tile(seq):
    return min(512, seq)


def _encoder_layer(x, mod3, layer_idx, tabs, ln_g, w_in_bf, gains, lam_params, subln, na_bias,
                   w_br_bf, w_out_bf):
    nb, seq, _ = x.shape
    x2 = x.reshape(nb * seq, D_MODEL)
    proj2 = _projection(x2, mod3, ln_g, w_in_bf, gains, tabs, seq, _proj_token_tile(seq))
    proj3 = proj2.reshape(nb, seq, IN_WIDTH)
    lam_init = 0.8 - 0.6 * math.exp(-0.3 * layer_idx)
    tq = A_SCORE_BYTES // (2 * 2 * seq * 4)
    ya = _diff_attention(proj3, lam_params, subln, lam_init, tq=tq)
    yb = _neighborhood_attention(proj3, na_bias)
    yc = _dilated_mixture(proj3, _proj_token_tile(seq))
    y2 = _tail(ya, yb, yc, proj2, x2, mod3, w_br_bf, w_out_bf, seq, _tail_token_tile(seq))
    return y2.reshape(nb, seq, D_MODEL)


def kernel(x_prompt, x_sample, c_prompt, c_sample, ln_g, w_ada, b_ada, w_in, qn_a, kn_a, lam_q1, lam_k1, lam_q2, lam_k2, subln_a, qn_b, kn_b, rpb_b, qn_c, kn_c, w_br, w_out):
    depth = w_in.shape[0]
    n_prompt = c_prompt.shape[0]
    mod_all = _modulation(jnp.concatenate([c_prompt, c_sample], axis=0), w_ada, b_ada)
    w_in_bf = _to_stored_columns(w_in.astype(BF16))
    w_br_bf = w_br.astype(BF16)
    w_out_bf = w_out.astype(BF16)
    gains = [_proj_gains(qn_a[l], kn_a[l], qn_b[l], kn_b[l], qn_c[l], kn_c[l]) for l in range(depth)]
    na_bias = [_na_bias_tables(rpb_b[l]) for l in range(depth)]

    def run(x, mod):
        nb, seq, _ = x.shape
        assert seq % (NA_QROWS * GRID_W) == 0 and seq // GRID_W >= NA_KROWS
        tabs = _rope_tables(seq, _proj_token_tile(seq))
        for l in range(depth):
            lam_params = jnp.stack([p[l] for p in (lam_q1, lam_k1, lam_q2, lam_k2)])
            x = _encoder_layer(x, mod[l].reshape(nb, 1, 3 * D_MODEL), l, tabs,
                               ln_g[l].reshape(1, D_MODEL), w_in_bf[l], gains[l], lam_params,
                               subln_a[l].reshape(1, LANES), na_bias[l], w_br_bf[l], w_out_bf[l])
        return x

    y_prompt = run(x_prompt, mod_all[:, :n_prompt])
    y_sample = run(x_sample, mod_all[:, n_prompt:])
    return (y_prompt, y_sample)
```

```python
import functools
import math

import jax
import jax.numpy as jnp
from jax import lax
from jax.experimental import pallas as pl
from jax.experimental.pallas import tpu as pltpu

F32 = jnp.float32
BF16 = jnp.bfloat16

D_MODEL = 1024
GRID_W = 64
BRANCH_WIDTH = 512
ROPE_THETA = 10000.0
EPS = 1e-6
NEG = -1e30
A_QK_DIM = 64
A_HEADS = 4
B_HEAD_DIM = 64
B_HEADS = 8
NA_KH = 8
NA_KW = 16
C_DILATIONS = (1, 4, 16)
C_HALF_WINDOW = 64
C_HEAD_DIM = 128
C_HEADS_PER_GROUP = 4
IN_WIDTH = 12288
LANES = 128

PROJ_TILE = 512
TILE_NAMES = ("AQ", "AK", "AV", "BQ", "BK", "BV", "CQ0", "CQ1", "CQ2", "CK0", "CK1", "CK2",
              "CV0", "CV1", "CV2", "Z0", "Z1", "Z2", "G0a", "G0b", "G1a", "G1b", "G2a", "G2b")
PROJ_GROUPS = (("AQ", "AK", "BQ", "BK", "Z0", "G0a", "AV", "BV"),
               ("CQ0", "CK0", "CQ1", "Z1", "G0b", "G1a", "G1b", "CV0"),
               ("CK1", "CQ2", "CK2", "Z2", "G2a", "G2b", "CV1", "CV2"))
C_TOP_DIL = C_DILATIONS[-1]
RESIDUE_MAJOR_TILES = tuple(f"C{kind}{g}" for kind in "QKV" for g in (1, 2))
GROUP_TILES = len(PROJ_GROUPS[0])
GROUP_WIDTH = GROUP_TILES * PROJ_TILE
SLOT = {name: i for i, name in enumerate(n for grp in PROJ_GROUPS for n in grp)}


def _col_block(name, sub=0):
    return SLOT[name] * (PROJ_TILE // LANES) + sub

VMEM_LIMIT = 56 * 1024 * 1024

NA_QROWS = 8
NA_KROWS = 16
NA_STEP_BLOCKS = 2
A_SCORE_BYTES = 32 * 1024 * 1024
C_QBLK = 128
C_UNROLL = 32
C_G1_RUNS = C_TOP_DIL // C_DILATIONS[1]
C_G1_QRUN = C_QBLK // C_G1_RUNS
C_G1_REACH = C_HALF_WINDOW // C_DILATIONS[1]
C_G1_KRUN = C_G1_QRUN + 2 * C_G1_REACH
BF16_ROWS = 16
QK64_SCALE = A_QK_DIM ** -0.5


def _cparams(sem):
    return pltpu.CompilerParams(dimension_semantics=sem, vmem_limit_bytes=VMEM_LIMIT)


def _sigmoid(x):
    return 1.0 / (1.0 + jnp.exp(-x))


def _mod_kernel(c_ref, w_ref, b_ref, o_ref):
    c = c_ref[...]
    o_ref[0] = jnp.dot(c * _sigmoid(c), w_ref[0], preferred_element_type=F32) + b_ref[0]


def _modulation(c_all, w_ada, b_ada):
    depth = w_ada.shape[0]
    nb = c_all.shape[0]
    return pl.pallas_call(
        _mod_kernel,
        grid=(depth, 3),
        in_specs=[
            pl.BlockSpec((nb, D_MODEL), lambda l, j: (0, 0)),
            pl.BlockSpec((1, D_MODEL, D_MODEL), lambda l, j: (l, 0, j)),
            pl.BlockSpec((1, 1, D_MODEL), lambda l, j: (l, 0, j)),
        ],
        out_specs=pl.BlockSpec((1, nb, D_MODEL), lambda l, j: (l, 0, j)),
        out_shape=jax.ShapeDtypeStruct((depth, nb, 3 * D_MODEL), F32),
        compiler_params=_cparams(("arbitrary", "arbitrary")),
        name="adaln_mod",
    )(c_all, w_ada, b_ada.reshape(depth, 1, 3 * D_MODEL))


def _first_map_lanes(shape):
    return (lax.broadcasted_iota(jnp.int32, shape, 1) & (A_QK_DIM // 2)) == 0


def _first_head_lanes(shape):
    return lax.broadcasted_iota(jnp.int32, shape, 1) < 64


def _rms_halves(xb, first):
    sq = xb * xb
    s_a = jnp.sum(jnp.where(first, sq, 0.0), axis=-1, keepdims=True)
    s_b = jnp.sum(jnp.where(first, 0.0, sq), axis=-1, keepdims=True)
    return lax.rsqrt(jnp.where(first, s_a, s_b) * (1.0 / A_QK_DIM) + EPS)


def _rms128(xb):
    return lax.rsqrt(jnp.mean(xb * xb, axis=-1, keepdims=True) + EPS)


def _rope(xb, cos, sin_signed):
    return xb * cos + pltpu.roll(xb, 64, 1) * sin_signed


def _tile_epilogue(name, xb, gain, tabs):
    ca_ref, sa_ref, cc_ref, sc_ref, ccr_ref, scr_ref = tabs
    if name in RESIDUE_MAJOR_TILES:
        cc_ref, sc_ref = ccr_ref, scr_ref
    if name == "AQ":
        rms = _rms_halves(xb, _first_map_lanes(xb.shape))
        return _rope(xb * (rms * gain), ca_ref[...], sa_ref[...]) * QK64_SCALE
    if name == "AK":
        rms = _rms_halves(xb, _first_map_lanes(xb.shape))
        return _rope(xb * (rms * gain), ca_ref[...], sa_ref[...])
    if name == "BQ":
        return xb * gain * (_rms_halves(xb, _first_head_lanes(xb.shape)) * QK64_SCALE)
    if name == "BK":
        return xb * gain * _rms_halves(xb, _first_head_lanes(xb.shape))
    if name[:2] in ("CQ", "CK"):
        return _rope(xb * gain, cc_ref[...], sc_ref[...]) * _rms128(xb)
    if name[0] == "Z":
        return xb * _sigmoid(xb)
    if name[0] == "G":
        return _sigmoid(xb)
    return xb


def _proj_kernel(x_ref, mod_ref, lng_ref, w_ref, gain_ref, ca_ref, sa_ref, cc_ref, sc_ref,
                 ccr_ref, scr_ref, o_ref, h_ref, hr_ref, hstage_ref, acc_ref):
    grp = pl.program_id(1)
    tm = h_ref.shape[0]
    run = tm // C_TOP_DIL

    @pl.when(grp == 0)
    def _():
        x = x_ref[...]
        ms = jnp.mean(x * x, axis=-1, keepdims=True)
        y = x * lax.rsqrt(ms + EPS) * lng_ref[...]
        mod = mod_ref[0]
        shift = mod[:, :D_MODEL]
        scale = mod[:, D_MODEL:2 * D_MODEL]
        h = y * (1.0 + scale) + shift
        h_ref[...] = h.astype(BF16)
        for cb in range(D_MODEL // LANES):
            hstage_ref[cb] = h[:, cb * LANES:(cb + 1) * LANES]

    def regroup_rows():
        for cb in range(D_MODEL // LANES):
            for rho in range(C_TOP_DIL):
                hr_ref[rho * run:(rho + 1) * run, cb * LANES:(cb + 1) * LANES] = hstage_ref[
                    cb, pl.ds(rho, run, stride=C_TOP_DIL), :].astype(BF16)

    assert not any(name in RESIDUE_MAJOR_TILES for name in PROJ_GROUPS[0])
    tabs = (ca_ref, sa_ref, cc_ref, sc_ref, ccr_ref, scr_ref)
    for gi, names in enumerate(PROJ_GROUPS):
        @pl.when(grp == gi)
        def _(names=names, gi=gi):
            gain = gain_ref[0]
            for t, name in enumerate(names):
                acc = acc_ref.at[t % 2]
                lhs_ref = hr_ref if name in RESIDUE_MAJOR_TILES else h_ref
                acc[...] = jnp.dot(lhs_ref[...], w_ref[:, t * PROJ_TILE:(t + 1) * PROJ_TILE],
                                   preferred_element_type=F32)
                for cb in range(PROJ_TILE // LANES):
                    sl = slice(cb * LANES, (cb + 1) * LANES)
                    out = slice(t * PROJ_TILE + cb * LANES, t * PROJ_TILE + (cb + 1) * LANES)
                    o_ref[:, out] = _tile_epilogue(name, acc[:, sl], gain[:, out], tabs).astype(BF16)
                if gi == 0 and t == 0:
                    regroup_rows()


def _projection(x2, mod3, ln_g, w_in_bf, gains, tabs, seq, tm):
    ntok = x2.shape[0]
    tps = seq // tm
    tab_spec = pl.BlockSpec((tm, LANES), lambda i, j: (i % tps, 0))
    return pl.pallas_call(
        _proj_kernel,
        grid=(ntok // tm, len(PROJ_GROUPS)),
        in_specs=[
            pl.BlockSpec((tm, D_MODEL), lambda i, j: (i, 0)),
            pl.BlockSpec((1, 1, 3 * D_MODEL), lambda i, j: (i // tps, 0, 0)),
            pl.BlockSpec((1, D_MODEL), lambda i, j: (0, 0)),
            pl.BlockSpec((D_MODEL, GROUP_WIDTH), lambda i, j: (0, j)),
            pl.BlockSpec((1, 1, GROUP_WIDTH), lambda i, j: (j, 0, 0)),
            *([tab_spec] * len(tabs)),
        ],
        out_specs=pl.BlockSpec((tm, GROUP_WIDTH), lambda i, j: (i, j)),
        out_shape=jax.ShapeDtypeStruct((ntok, IN_WIDTH), BF16),
        scratch_shapes=[pltpu.VMEM((tm, D_MODEL), BF16), pltpu.VMEM((tm, D_MODEL), BF16),
                        pltpu.VMEM((D_MODEL // LANES, tm, LANES), F32),
                        pltpu.VMEM((2, tm, PROJ_TILE), F32)],
        compiler_params=_cparams(("arbitrary", "arbitrary")),
        name="in_proj",
    )(x2, mod3, ln_g, w_in_bf, gains, *tabs)


def _diff_attn_kernel(q_ref, k_ref, v_ref, lam_ref, sub_ref, o_ref, s_even_ref, s_odd_ref,
                      *, lam_init):
    t = pl.program_id(0)
    dn = (((1,), (1,)), ((), ()))

    def score(s_ref):
        q = q_ref[...]
        k = k_ref[...]
        lo = _first_map_lanes(q.shape)
        zero = jnp.zeros_like(q)
        s_ref[0] = lax.dot_general(jnp.where(lo, q, zero), k, dn, preferred_element_type=F32)
        s_ref[1] = lax.dot_general(jnp.where(lo, zero, q), k, dn, preferred_element_type=F32)

    def finish(s_ref):
        lq1, lk1, lq2, lk2 = (lam_ref[i:i + 1, :] for i in range(4))
        lam = (jnp.exp(jnp.sum(lq1 * lk1, axis=-1, keepdims=True))
               - jnp.exp(jnp.sum(lq2 * lk2, axis=-1, keepdims=True)) + lam_init)
        v = v_ref[...]
        v_ones = jnp.concatenate([v, jnp.ones_like(v)], axis=1)
        prods = []
        for m in range(2):
            s = s_ref[m]
            e = jnp.exp(s - jnp.max(s, axis=-1, keepdims=True))
            prods.append(jnp.dot(e.astype(BF16), v_ones, preferred_element_type=F32))
        (a0, a1) = prods
        o = a0[:, :LANES] / a0[:, LANES:] - lam * (a1[:, :LANES] / a1[:, LANES:])
        ms = jnp.mean(o * o, axis=-1, keepdims=True)
        o_ref[...] = (o * lax.rsqrt(ms + EPS) * sub_ref[...] * (1.0 - lam_init)).astype(BF16)

    @pl.when(t == 0)
    def _():
        s_odd_ref[...] = jnp.zeros_like(s_odd_ref)

    @pl.when(t % 2 == 0)
    def _():
        score(s_even_ref)
        finish(s_odd_ref)

    @pl.when(t % 2 == 1)
    def _():
        score(s_odd_ref)
        finish(s_even_ref)


def _diff_attention(proj3, lam_params, subln, lam_init, tq):
    nb, seq, _ = proj3.shape
    nq = seq // tq
    tiles = nb * A_HEADS * nq

    def split(t):
        return t // (A_HEADS * nq), (t // nq) % A_HEADS, t % nq

    def cur(t):
        return split(jnp.minimum(t, tiles - 1))

    def prev(t):
        return split(jnp.maximum(t - 1, 0))

    lam_spec = pl.BlockSpec((4, A_QK_DIM), lambda t: (0, 0))
    return pl.pallas_call(
        functools.partial(_diff_attn_kernel, lam_init=lam_init),
        grid=(tiles + 1,),
        in_specs=[
            pl.BlockSpec((None, tq, LANES), lambda t: (cur(t)[0], cur(t)[2], _col_block("AQ") + cur(t)[1])),
            pl.BlockSpec((None, seq, LANES), lambda t: (cur(t)[0], 0, _col_block("AK") + cur(t)[1])),
            pl.BlockSpec((None, seq, LANES), lambda t: (prev(t)[0], 0, _col_block("AV") + prev(t)[1])),
            lam_spec,
            pl.BlockSpec((1, LANES), lambda t: (0, 0)),
        ],
        out_specs=pl.BlockSpec((None, tq, LANES), lambda t: (prev(t)[0], prev(t)[2], prev(t)[1])),
        out_shape=jax.ShapeDtypeStruct((nb, seq, BRANCH_WIDTH), BF16),
        scratch_shapes=[pltpu.VMEM((2, tq, seq), F32), pltpu.VMEM((2, tq, seq), F32)],
        compiler_params=_cparams(("arbitrary",)),
        name="diff_attn",
    )(proj3, proj3, proj3, lam_params, subln)


def _na_bias_tables(rpb):
    rows = NA_QROWS + NA_KROWS
    qc = jnp.arange(GRID_W)
    kc = jnp.arange(GRID_W)
    cs = jnp.clip(qc - NA_KW // 2, 0, GRID_W - NA_KW)
    col_ok = (kc[None, :] >= cs[:, None]) & (kc[None, :] < cs[:, None] + NA_KW)
    col_off = jnp.clip(kc[None, :] - qc[:, None], -(NA_KW - 1), NA_KW - 1) + NA_KW - 1
    cols = jnp.where(col_ok, rpb.astype(F32)[:, :, col_off], NEG)
    pad = NA_KROWS
    cols = jnp.pad(cols, ((0, 0), (pad, pad), (0, 0), (0, 0)), constant_values=NEG)
    kh = min(NA_KH, rows)
    tables = []
    for r0, ws in ((0, 0), (NA_QROWS, NA_QROWS - kh // 2), (rows - NA_QROWS, rows - NA_KROWS)):
        slabs = []
        for qr in range(NA_QROWS):
            r = r0 + qr
            rs = min(max(r - kh // 2, 0), rows - kh)
            first = ws - r + NA_KH - 1 + pad
            slab = cols[:, first:first + NA_KROWS]
            kr = ws + jnp.arange(NA_KROWS)
            row_ok = (kr >= rs) & (kr < rs + kh)
            slabs.append(jnp.where(row_ok[None, :, None, None], slab, NEG))
        t = jnp.stack(slabs, axis=1)
        tables.append(t.transpose(0, 1, 3, 2, 4).reshape(
            B_HEADS, NA_QROWS * GRID_W, NA_KROWS * GRID_W))
    return jnp.stack(tables)


def _na_kernel(q_ref, k_ref, v_ref, bias_ref, o_ref, s_even_ref, s_odd_ref, *, rows, nrb):
    t = pl.program_id(0)
    nkeys = NA_KROWS * GRID_W
    dn = (((1,), (1,)), ((), ()))

    def key_start(i):
        ws = jnp.clip(i * NA_QROWS - NA_KH // 2, 0, rows - NA_KROWS)
        return pl.multiple_of(ws * GRID_W, GRID_W)

    nq = NA_QROWS * GRID_W
    tiles_per_image = nrb // NA_STEP_BLOCKS

    def score(s_ref):
        first = (jnp.minimum(t, pl.num_programs(0) - 2) % tiles_per_image) * NA_STEP_BLOCKS
        for sub in range(NA_STEP_BLOCKS):
            i = first + sub
            kind = jnp.where(i == 0, 0, jnp.where(i == nrb - 1, 2, 1))
            kw = k_ref[pl.ds(key_start(i), nkeys), :]
            q = q_ref[sub * nq:(sub + 1) * nq, :]
            lo = _first_head_lanes(q.shape)
            zero = jnp.zeros_like(q)
            for e in range(2):
                qe = jnp.where(lo, q, zero) if e == 0 else jnp.where(lo, zero, q)
                s_ref[2 * sub + e] = (lax.dot_general(qe, kw, dn, preferred_element_type=F32)
                                      + bias_ref[kind, e])

    def finish(s_ref):
        first = (jnp.maximum(t - 1, 0) % tiles_per_image) * NA_STEP_BLOCKS
        for sub in range(NA_STEP_BLOCKS):
            vw = v_ref[pl.ds(key_start(first + sub), nkeys), :]
            v_ones = jnp.concatenate([vw, jnp.ones_like(vw)], axis=1)
            outs = []
            for e in range(2):
                s = s_ref[2 * sub + e]
                ex = jnp.exp(s - jnp.max(s, axis=-1, keepdims=True))
                a = jnp.dot(ex.astype(BF16), v_ones, preferred_element_type=F32)
                outs.append(a[:, :LANES] / a[:, LANES:])
            o_ref[sub * nq:(sub + 1) * nq, :] = jnp.where(
                _first_head_lanes(outs[0].shape), outs[0], outs[1]).astype(BF16)

    @pl.when(t == 0)
    def _():
        s_odd_ref[...] = jnp.zeros_like(s_odd_ref)

    @pl.when(t % 2 == 0)
    def _():
        score(s_even_ref)
        finish(s_odd_ref)

    @pl.when(t % 2 == 1)
    def _():
        score(s_odd_ref)
        finish(s_even_ref)


def _neighborhood_attention(proj3, bias):
    nb, seq, _ = proj3.shape
    rows = seq // GRID_W
    nrb = rows // NA_QROWS
    nq = NA_QROWS * GRID_W
    nk = NA_KROWS * GRID_W
    assert nrb % NA_STEP_BLOCKS == 0
    per_image = nrb // NA_STEP_BLOCKS
    tiles = (B_HEADS // 2) * nb * per_image

    def split(t):
        return t // (nb * per_image), (t // per_image) % nb, t % per_image

    def cur(t):
        return split(jnp.minimum(t, tiles - 1))

    def prev(t):
        return split(jnp.maximum(t - 1, 0))

    return pl.pallas_call(
        functools.partial(_na_kernel, rows=rows, nrb=nrb),
        grid=(tiles + 1,),
        in_specs=[
            pl.BlockSpec((None, NA_STEP_BLOCKS * nq, LANES),
                         lambda t: (cur(t)[1], cur(t)[2], _col_block("BQ") + cur(t)[0])),
            pl.BlockSpec((None, seq, LANES), lambda t: (cur(t)[1], 0, _col_block("BK") + cur(t)[0])),
            pl.BlockSpec((None, seq, LANES), lambda t: (prev(t)[1], 0, _col_block("BV") + prev(t)[0])),
            pl.BlockSpec((3, 2, nq, nk), lambda t: (0, cur(t)[0], 0, 0), pipeline_mode=pl.Buffered(1)),
        ],
        out_specs=pl.BlockSpec((None, NA_STEP_BLOCKS * nq, LANES),
                               lambda t: (prev(t)[1], prev(t)[2], prev(t)[0])),
        out_shape=jax.ShapeDtypeStruct((nb, seq, BRANCH_WIDTH), BF16),
        scratch_shapes=[pltpu.VMEM((2 * NA_STEP_BLOCKS, nq, nk), F32)] * 2,
        compiler_params=_cparams(("arbitrary",)),
        name="nbr_attn",
    )(proj3, proj3, proj3, bias)


def _softmax_av(s, v):
    m = jnp.max(s, axis=-1, keepdims=True)
    e = jnp.exp(s - m)
    v_ones = jnp.concatenate([v, jnp.ones_like(v)], axis=1)
    acc = jnp.dot(e.astype(BF16), v_ones, preferred_element_type=F32)
    l = acc[:, LANES:]
    return acc[:, :LANES] / l, m + jnp.log(l)


def _dil_kernel(q0_ref, q1_ref, q2_ref, k0_ref, k1_ref, k2_ref, v0_ref, v1_ref, v2_ref, o_ref,
                o0_ref, l0_ref, o0r_ref, l0r_ref, o1_ref, l1_ref, o2_ref, l2_ref,
                band_ref, band4_ref, band16_ref, *, seq, tile):
    scale = C_HEAD_DIM ** -0.5
    dn = (((1,), (1,)), ((), ()))
    run = tile // C_TOP_DIL
    n_tiles = seq // tile
    sub16 = seq // C_TOP_DIL
    hw = C_HALF_WINDOW

    full_win = C_QBLK + 2 * hw
    pack = band16_ref.shape[0] // sub16

    @pl.when((pl.program_id(0) == 0) & (pl.program_id(1) == 0))
    def _():
        qi = lax.broadcasted_iota(jnp.int32, (C_QBLK, full_win), 0)
        kj = lax.broadcasted_iota(jnp.int32, (C_QBLK, full_win), 1)
        for lead in range(3):
            band_ref[lead] = jnp.where(jnp.abs(kj - qi - lead * hw) <= hw, 0.0, NEG)
        qi = lax.broadcasted_iota(jnp.int32, band16_ref.shape, 0)
        kj = lax.broadcasted_iota(jnp.int32, band16_ref.shape, 1)
        same_class = (qi // sub16) == (kj // sub16)
        band16_ref[...] = jnp.where(same_class & (jnp.abs(kj - qi) <= hw), 0.0, NEG)
        g1_keys = C_G1_RUNS * C_G1_KRUN
        qr = lax.broadcasted_iota(jnp.int32, (C_QBLK, g1_keys), 0)
        kc = lax.broadcasted_iota(jnp.int32, (C_QBLK, g1_keys), 1)
        q_pos = C_G1_RUNS * (qr % C_G1_QRUN) + qr // C_G1_QRUN
        k_pos = C_G1_RUNS * (kc % C_G1_KRUN) + kc // C_G1_KRUN
        for lead in range(3):
            shift = C_G1_RUNS * C_G1_REACH * lead
            band4_ref[lead] = jnp.where(jnp.abs(q_pos + shift - k_pos) <= hw, 0.0, NEG)

    def attend0(n):
        a = n * C_QBLK
        ws = jnp.clip(a - hw, 0, seq - full_win)
        rows = pl.ds(pl.multiple_of(a, C_QBLK), C_QBLK)
        kw = k0_ref[pl.ds(pl.multiple_of(ws, hw), full_win), :]
        vw = v0_ref[pl.ds(pl.multiple_of(ws, hw), full_win), :]
        s = lax.dot_general(q0_ref[rows, :], kw, dn, preferred_element_type=F32) * scale
        o0_ref[rows, :], l0_ref[rows, :] = _softmax_av(s + band_ref[(a - ws) // hw], vw)

    n_blocks = seq // C_QBLK
    unroll = min(C_UNROLL, n_blocks)

    def blocks0(it, carry):
        for u in range(unroll):
            attend0(it * unroll + u)
        return carry

    lax.fori_loop(0, n_blocks // unroll, blocks0, 0)

    def tile_row(j):
        return (j // run) * tile + j % run

    def attend1(n):
        blocks_per_class = sub16 // C_G1_QRUN
        rho4 = n // blocks_per_class
        j0 = (n % blocks_per_class) * C_G1_QRUN
        jw = jnp.clip(j0 - C_G1_REACH, 0, sub16 - C_G1_KRUN)
        class_row = [(C_DILATIONS[1] * c + rho4) * run for c in range(C_G1_RUNS)]
        q_base = tile_row(j0)
        k_base = [tile_row(jw + BF16_ROWS * p) for p in range(C_G1_KRUN // BF16_ROWS)]
        q_rows = [pl.ds(pl.multiple_of(q_base + class_row[c], C_G1_QRUN), C_G1_QRUN)
                  for c in range(C_G1_RUNS)]
        k_rows = [pl.ds(pl.multiple_of(base + class_row[c], BF16_ROWS), BF16_ROWS)
                  for c in range(C_G1_RUNS) for base in k_base]
        q = jnp.concatenate([q1_ref[r, :] for r in q_rows], axis=0)
        kw = jnp.concatenate([k1_ref[r, :] for r in k_rows], axis=0)
        vw = jnp.concatenate([v1_ref[r, :] for r in k_rows], axis=0)
        s = lax.dot_general(q, kw, dn, preferred_element_type=F32) * scale
        o, lse = _softmax_av(s + band4_ref[(j0 - jw) // C_G1_REACH], vw)
        for c, r in enumerate(q_rows):
            o1_ref[r, :] = o[c * C_G1_QRUN:(c + 1) * C_G1_QRUN]
            l1_ref[r, :] = lse[c * C_G1_QRUN:(c + 1) * C_G1_QRUN]

    def blocks1(it, carry):
        for u in range(unroll):
            attend1(it * unroll + u)
        return carry

    lax.fori_loop(0, n_blocks // unroll, blocks1, 0)

    def attend2(n):
        rows = [pl.ds(pl.multiple_of(t * tile + (n * pack + u) * run, run), run)
                for u in range(pack) for t in range(n_tiles)]
        q = jnp.concatenate([q2_ref[r, :] for r in rows], axis=0)
        kw = jnp.concatenate([k2_ref[r, :] for r in rows], axis=0)
        vw = jnp.concatenate([v2_ref[r, :] for r in rows], axis=0)
        s = lax.dot_general(q, kw, dn, preferred_element_type=F32) * scale
        o, lse = _softmax_av(s + band16_ref[...], vw)
        for i, r in enumerate(rows):
            o2_ref[r, :] = o[i * run:(i + 1) * run]
            l2_ref[r, :] = lse[i * run:(i + 1) * run]

    unroll2 = min(C_UNROLL, C_TOP_DIL // pack)

    def blocks2(it, carry):
        for u in range(unroll2):
            attend2(it * unroll2 + u)
        return carry

    lax.fori_loop(0, C_TOP_DIL // pack // unroll2, blocks2, 0)

    for t in range(n_tiles):
        for rho in range(C_TOP_DIL):
            dst = slice(t * tile + rho * run, t * tile + (rho + 1) * run)
            o0r_ref[dst, :] = o0_ref[pl.ds(t * tile + rho, run, stride=C_TOP_DIL), :]
            l0r_ref[dst, :] = l0_ref[pl.ds(t * tile + rho, run, stride=C_TOP_DIL), :]
    l0, l1, l2 = l0r_ref[...], l1_ref[...], l2_ref[...]
    l_max = jnp.maximum(jnp.maximum(l0, l1), l2)
    w0, w1, w2 = jnp.exp(l0 - l_max), jnp.exp(l1 - l_max), jnp.exp(l2 - l_max)
    o0_ref[...] = (w0 * o0r_ref[...] + w1 * o1_ref[...] + w2 * o2_ref[...]) * (1.0 / (w0 + w1 + w2))

    for t in range(n_tiles):
        def to_token_order(j, carry, t=t):
            src = o0_ref[pl.ds(t * tile + j, C_TOP_DIL, stride=run), :]
            dst = pl.ds(pl.multiple_of(t * tile + j * C_TOP_DIL, C_TOP_DIL), C_TOP_DIL)
            o_ref[dst, :] = src.astype(BF16)
            return carry

        lax.fori_loop(0, run, to_token_order, 0, unroll=8)


def _dilated_mixture(proj3, tile):
    nb, seq, _ = proj3.shape
    assert seq % tile == 0 and tile % (C_TOP_DIL * C_G1_QRUN) == 0
    assert seq >= C_QBLK + 2 * C_HALF_WINDOW and (seq // C_QBLK) % min(C_UNROLL, seq // C_QBLK) == 0
    assert (seq // C_TOP_DIL) % C_G1_KRUN == 0
    side2 = max(seq // C_TOP_DIL, C_QBLK + 2 * C_HALF_WINDOW)

    def spec(kind, g):
        cb0 = _col_block(f"{kind}{g}")
        return pl.BlockSpec((None, seq, LANES), lambda b, h, cb0=cb0: (b, 0, cb0 + h))

    in_specs = [spec(kind, g) for kind in ("CQ", "CK", "CV") for g in range(len(C_DILATIONS))]
    return pl.pallas_call(
        functools.partial(_dil_kernel, seq=seq, tile=tile),
        grid=(nb, C_HEADS_PER_GROUP),
        in_specs=in_specs,
        out_specs=pl.BlockSpec((None, seq, LANES), lambda b, h: (b, 0, h)),
        out_shape=jax.ShapeDtypeStruct((nb, seq, BRANCH_WIDTH), BF16),
        scratch_shapes=(
            [pltpu.VMEM((seq, LANES), F32)] * 8
            + [pltpu.VMEM((3, C_QBLK, C_QBLK + 2 * C_HALF_WINDOW), F32)] * 2
            + [pltpu.VMEM((side2, side2), F32)]),
        compiler_params=_cparams(("arbitrary", "arbitrary")),
        name="dil_attn",
    )(*([proj3] * 9))


def _tail_kernel(ya_ref, yb_ref, yc_ref, z0_ref, z1_ref, z2_ref, g0a_ref, g0b_ref, g1a_ref,
                 g1b_ref, g2a_ref, g2b_ref, x_ref, gate_ref, wbr_ref, wout_ref, o_ref):
    halves = [None, None]
    branches = ((ya_ref, z0_ref, (g0a_ref, g0b_ref)), (yb_ref, z1_ref, (g1a_ref, g1b_ref)),
                (yc_ref, z2_ref, (g2a_ref, g2b_ref)))
    for i, (y_ref, z_ref, g_refs) in enumerate(branches):
        yz = (y_ref[...].astype(F32) * z_ref[...].astype(F32)).astype(BF16)
        for half, g_ref in enumerate(g_refs):
            cols = slice(half * PROJ_TILE, (half + 1) * PROJ_TILE)
            u = g_ref[...].astype(F32) * jnp.dot(yz, wbr_ref[i, :, cols], preferred_element_type=F32)
            halves[half] = u if halves[half] is None else halves[half] + u
    merged = jnp.concatenate(halves, axis=1).astype(BF16)
    out = jnp.dot(merged, wout_ref[...], preferred_element_type=F32)
    o_ref[...] = x_ref[...] + gate_ref[0] * out


def _tail(ya, yb, yc, proj2, x2, mod3, w_br_bf, w_out_bf, seq, tm):
    ntok = x2.shape[0]
    tps = seq // tm
    y_spec = pl.BlockSpec((None, tm, BRANCH_WIDTH), lambda i: (i // tps, i % tps, 0))

    def tile_spec(name):
        return pl.BlockSpec((tm, PROJ_TILE), lambda i, s=SLOT[name]: (i, s))

    z_specs = [tile_spec(f"Z{t}") for t in range(3)]
    g_specs = [tile_spec(f"G{t}{half}") for t in range(3) for half in "ab"]
    return pl.pallas_call(
        _tail_kernel,
        grid=(ntok // tm,),
        in_specs=[y_spec, y_spec, y_spec, *z_specs, *g_specs,
                  pl.BlockSpec((tm, D_MODEL), lambda i: (i, 0)),
                  pl.BlockSpec((1, 1, D_MODEL), lambda i: (i // tps, 0, 2)),
                  pl.BlockSpec((3, BRANCH_WIDTH, D_MODEL), lambda i: (0, 0, 0)),
                  pl.BlockSpec((D_MODEL, D_MODEL), lambda i: (0, 0))],
        out_specs=pl.BlockSpec((tm, D_MODEL), lambda i: (i, 0)),
        out_shape=jax.ShapeDtypeStruct((ntok, D_MODEL), F32),
        compiler_params=_cparams(("arbitrary",)),
        name="tail",
    )(ya, yb, yc, *([proj2] * 9), x2, mod3, w_br_bf, w_out_bf)


def _rope_tables(seq, tile):
    def base(d):
        inv = ROPE_THETA ** (-jnp.arange(0, d, 2, dtype=F32) / d)
        ang = jnp.arange(seq, dtype=F32)[:, None] * inv[None, :]
        return jnp.cos(ang), jnp.sin(ang)

    cos_a, sin_a = base(A_QK_DIM)
    ca = jnp.tile(cos_a, (1, 4))
    sa = jnp.concatenate([-sin_a, -sin_a, sin_a, sin_a], axis=-1)
    cos_c, sin_c = base(C_HEAD_DIM)
    cc = jnp.tile(cos_c, (1, 2))
    sc = jnp.concatenate([-sin_c, sin_c], axis=-1)

    def residue_major(tab):
        t4 = tab.reshape(seq // tile, tile // C_TOP_DIL, C_TOP_DIL, LANES)
        return t4.transpose(0, 2, 1, 3).reshape(seq, LANES)

    return ca, sa, cc, sc, residue_major(cc), residue_major(sc)


def _to_stored_columns(w):
    lead = w.shape[:-1]
    tiles = {name: w[..., i * PROJ_TILE:(i + 1) * PROJ_TILE] for i, name in enumerate(TILE_NAMES)}
    for name in ("AQ", "AK"):
        t = tiles[name].reshape(*lead, A_HEADS, 2, 2, A_QK_DIM // 2)
        tiles[name] = jnp.swapaxes(t, -3, -2).reshape(*lead, PROJ_TILE)
    return jnp.concatenate([tiles[n] for grp in PROJ_GROUPS for n in grp], axis=-1)


def _proj_gains(qn_a, kn_a, qn_b, kn_b, qn_c, kn_c):
    per_tile = {"AQ": jnp.tile(qn_a, PROJ_TILE // A_QK_DIM), "AK": jnp.tile(kn_a, PROJ_TILE // A_QK_DIM),
                "BQ": jnp.tile(qn_b, PROJ_TILE // B_HEAD_DIM), "BK": jnp.tile(kn_b, PROJ_TILE // B_HEAD_DIM)}
    for g in range(len(C_DILATIONS)):
        per_tile[f"CQ{g}"] = jnp.tile(qn_c, PROJ_TILE // C_HEAD_DIM)
        per_tile[f"CK{g}"] = jnp.tile(kn_c, PROJ_TILE // C_HEAD_DIM)
    ones = jnp.ones((PROJ_TILE,), F32)
    full = jnp.concatenate([per_tile.get(name, ones).astype(F32) for name in TILE_NAMES])
    return _to_stored_columns(full).reshape(len(PROJ_GROUPS), 1, GROUP_WIDTH)


def _proj_token_tile(seq):
    return min(512, seq)


def _tail_token_tile(seq):
    return min(512, seq)


def _encoder_layer(x, mod3, layer_idx, tabs, ln_g, w_in_bf, gains, lam_params, subln, na_bias,
                   w_br_bf, w_out_bf):
    nb, seq, _ = x.shape
    x2 = x.reshape(nb * seq, D_MODEL)
    proj2 = _projection(x2, mod3, ln_g, w_in_bf, gains, tabs, seq, _proj_token_tile(seq))
    proj3 = proj2.reshape(nb, seq, IN_WIDTH)
    lam_init = 0.8 - 0.6 * math.exp(-0.3 * layer_idx)
    tq = A_SCORE_BYTES // (2 * 2 * seq * 4)
    ya = _diff_attention(proj3, lam_params, subln, lam_init, tq=tq)
    yb = _neighborhood_attention(proj3, na_bias)
    yc = _dilated_mixture(proj3, _proj_token_tile(seq))
    y2 = _tail(ya, yb, yc, proj2, x2, mod3, w_br_bf, w_out_bf, seq, _tail_token_tile(seq))
    return y2.reshape(nb, seq, D_MODEL)


def kernel(x_prompt, x_sample, c_prompt, c_sample, ln_g, w_ada, b_ada, w_in, qn_a, kn_a, lam_q1, lam_k1, lam_q2, lam_k2, subln_a, qn_b, kn_b, rpb_b, qn_c, kn_c, w_br, w_out):
    depth = w_in.shape[0]
    n_prompt = c_prompt.shape[0]
    mod_all = _modulation(jnp.concatenate([c_prompt, c_sample], axis=0), w_ada, b_ada)
    w_in_bf = _to_stored_columns(w_in.astype(BF16))
    w_br_bf = w_br.astype(BF16)
    w_out_bf = w_out.astype(BF16)
    gains = [_proj_gains(qn_a[l], kn_a[l], qn_b[l], kn_b[l], qn_c[l], kn_c[l]) for l in range(depth)]
    na_bias = [_na_bias_tables(rpb_b[l]) for l in range(depth)]

    def run(x, mod):
        nb, seq, _ = x.shape
        assert seq % (NA_QROWS * GRID_W) == 0 and seq // GRID_W >= NA_KROWS
        tabs = _rope_tables(seq, _proj_token_tile(seq))
        for l in range(depth):
            lam_params = jnp.stack([p[l] for p in (lam_q1, lam_k1, lam_q2, lam_k2)])
            x = _encoder_layer(x, mod[l].reshape(nb, 1, 3 * D_MODEL), l, tabs,
                               ln_g[l].reshape(1, D_MODEL), w_in_bf[l], gains[l], lam_params,
                               subln_a[l].reshape(1, LANES), na_bias[l], w_br_bf[l], w_out_bf[l])
        return x

    y_prompt = run(x_prompt, mod_all[:, :n_prompt])
    y_sample = run(x_sample, mod_all[:, n_prompt:])
    return (y_prompt, y_sample)
```

```python
import functools
import math

import jax
import jax.numpy as jnp
from jax import lax
from jax.experimental import pallas as pl
from jax.experimental.pallas import tpu as pltpu

F32 = jnp.float32
BF16 = jnp.bfloat16

D_MODEL = 1024
GRID_W = 64
BRANCH_WIDTH = 512
ROPE_THETA = 10000.0
EPS = 1e-6
NEG = -1e30
A_QK_DIM = 64
A_HEADS = 4
B_HEAD_DIM = 64
B_HEADS = 8
NA_KH = 8
NA_KW = 16
C_DILATIONS = (1, 4, 16)
C_HALF_WINDOW = 64
C_HEAD_DIM = 128
C_HEADS_PER_GROUP = 4
IN_WIDTH = 12288
LANES = 128

PROJ_TILE = 512
TILE_NAMES = ("AQ", "AK", "AV", "BQ", "BK", "BV", "CQ0", "CQ1", "CQ2", "CK0", "CK1", "CK2",
              "CV0", "CV1", "CV2", "Z0", "Z1", "Z2", "G0a", "G0b", "G1a", "G1b", "G2a", "G2b")
PROJ_GROUPS = (("AQ", "AK", "BQ", "BK", "Z0", "G0a", "AV", "BV"),
               ("CQ0", "CK0", "CQ1", "Z1", "G0b", "G1a", "G1b", "CV0"),
               ("CK1", "CQ2", "CK2", "Z2", "G2a", "G2b", "CV1", "CV2"))
C_TOP_DIL = C_DILATIONS[-1]
RESIDUE_MAJOR_TILES = tuple(f"C{kind}{g}" for kind in "QKV" for g in (1, 2))
GROUP_TILES = len(PROJ_GROUPS[0])
GROUP_WIDTH = GROUP_TILES * PROJ_TILE
SLOT = {name: i for i, name in enumerate(n for grp in PROJ_GROUPS for n in grp)}


def _col_block(name, sub=0):
    return SLOT[name] * (PROJ_TILE // LANES) + sub

VMEM_LIMIT = 56 * 1024 * 1024

NA_QROWS = 8
NA_KROWS = 16
NA_STEP_BLOCKS = 2
A_SCORE_BYTES = 32 * 1024 * 1024
C_QBLK = 128
C_UNROLL = 32
C_G1_RUNS = C_TOP_DIL // C_DILATIONS[1]
C_G1_QRUN = C_QBLK // C_G1_RUNS
C_G1_REACH = C_HALF_WINDOW // C_DILATIONS[1]
C_G1_KRUN = C_G1_QRUN + 2 * C_G1_REACH
BF16_ROWS = 16
QK64_SCALE = A_QK_DIM ** -0.5


def _cparams(sem):
    return pltpu.CompilerParams(dimension_semantics=sem, vmem_limit_bytes=VMEM_LIMIT)


def _sigmoid(x):
    return 1.0 / (1.0 + jnp.exp(-x))


def _mod_kernel(c_ref, w_ref, b_ref, o_ref):
    c = c_ref[...]
    o_ref[0] = jnp.dot(c * _sigmoid(c), w_ref[0], preferred_element_type=F32) + b_ref[0]


def _modulation(c_all, w_ada, b_ada):
    depth = w_ada.shape[0]
    nb = c_all.shape[0]
    return pl.pallas_call(
        _mod_kernel,
        grid=(depth, 3),
        in_specs=[
            pl.BlockSpec((nb, D_MODEL), lambda l, j: (0, 0)),
            pl.BlockSpec((1, D_MODEL, D_MODEL), lambda l, j: (l, 0, j)),
            pl.BlockSpec((1, 1, D_MODEL), lambda l, j: (l, 0, j)),
        ],
        out_specs=pl.BlockSpec((1, nb, D_MODEL), lambda l, j: (l, 0, j)),
        out_shape=jax.ShapeDtypeStruct((depth, nb, 3 * D_MODEL), F32),
        compiler_params=_cparams(("arbitrary", "arbitrary")),
        name="adaln_mod",
    )(c_all, w_ada, b_ada.reshape(depth, 1, 3 * D_MODEL))


def _first_map_lanes(shape):
    return (lax.broadcasted_iota(jnp.int32, shape, 1) & (A_QK_DIM // 2)) == 0


def _first_head_lanes(shape):
    return lax.broadcasted_iota(jnp.int32, shape, 1) < 64


def _rms_halves(xb, first):
    sq = xb * xb
    s_a = jnp.sum(jnp.where(first, sq, 0.0), axis=-1, keepdims=True)
    s_b = jnp.sum(jnp.where(first, 0.0, sq), axis=-1, keepdims=True)
    return lax.rsqrt(jnp.where(first, s_a, s_b) * (1.0 / A_QK_DIM) + EPS)


def _rms128(xb):
    return lax.rsqrt(jnp.mean(xb * xb, axis=-1, keepdims=True) + EPS)


def _rope(xb, cos, sin_signed):
    return xb * cos + pltpu.roll(xb, 64, 1) * sin_signed


def _tile_epilogue(name, xb, gain, tabs):
    ca_ref, sa_ref, cc_ref, sc_ref, ccr_ref, scr_ref = tabs
    if name in RESIDUE_MAJOR_TILES:
        cc_ref, sc_ref = ccr_ref, scr_ref
    if name == "AQ":
        rms = _rms_halves(xb, _first_map_lanes(xb.shape))
        return _rope(xb * (rms * gain), ca_ref[...], sa_ref[...]) * QK64_SCALE
    if name == "AK":
        rms = _rms_halves(xb, _first_map_lanes(xb.shape))
        return _rope(xb * (rms * gain), ca_ref[...], sa_ref[...])
    if name == "BQ":
        return xb * gain * (_rms_halves(xb, _first_head_lanes(xb.shape)) * QK64_SCALE)
    if name == "BK":
        return xb * gain * _rms_halves(xb, _first_head_lanes(xb.shape))
    if name[:2] in ("CQ", "CK"):
        return _rope(xb * gain, cc_ref[...], sc_ref[...]) * _rms128(xb)
    if name[0] == "Z":
        return xb * _sigmoid(xb)
    if name[0] == "G":
        return _sigmoid(xb)
    return xb


def _proj_kernel(x_ref, mod_ref, lng_ref, w_ref, gain_ref, ca_ref, sa_ref, cc_ref, sc_ref,
                 ccr_ref, scr_ref, o_ref, h_ref, hr_ref, hstage_ref, acc_ref):
    grp = pl.program_id(1)
    tm = h_ref.shape[0]
    run = tm // C_TOP_DIL

    @pl.when(grp == 0)
    def _():
        x = x_ref[...]
        ms = jnp.mean(x * x, axis=-1, keepdims=True)
        y = x * lax.rsqrt(ms + EPS) * lng_ref[...]
        mod = mod_ref[0]
        shift = mod[:, :D_MODEL]
        scale = mod[:, D_MODEL:2 * D_MODEL]
        h = y * (1.0 + scale) + shift
        h_ref[...] = h.astype(BF16)
        for cb in range(D_MODEL // LANES):
            hstage_ref[cb] = h[:, cb * LANES:(cb + 1) * LANES]

    def regroup_rows():
        for cb in range(D_MODEL // LANES):
            for rho in range(C_TOP_DIL):
                hr_ref[rho * run:(rho + 1) * run, cb * LANES:(cb + 1) * LANES] = hstage_ref[
                    cb, pl.ds(rho, run, stride=C_TOP_DIL), :].astype(BF16)

    assert not any(name in RESIDUE_MAJOR_TILES for name in PROJ_GROUPS[0])
    tabs = (ca_ref, sa_ref, cc_ref, sc_ref, ccr_ref, scr_ref)
    for gi, names in enumerate(PROJ_GROUPS):
        @pl.when(grp == gi)
        def _(names=names, gi=gi):
            gain = gain_ref[0]
            for t, name in enumerate(names):
                acc = acc_ref.at[t % 2]
                lhs_ref = hr_ref if name in RESIDUE_MAJOR_TILES else h_ref
                acc[...] = jnp.dot(lhs_ref[...], w_ref[:, t * PROJ_TILE:(t + 1) * PROJ_TILE],
                                   preferred_element_type=F32)
                for cb in range(PROJ_TILE // LANES):
                    sl = slice(cb * LANES, (cb + 1) * LANES)
                    out = slice(t * PROJ_TILE + cb * LANES, t * PROJ_TILE + (cb + 1) * LANES)
                    o_ref[:, out] = _tile_epilogue(name, acc[:, sl], gain[:, out], tabs).astype(BF16)
                if gi == 0 and t == 0:
                    regroup_rows()


def _projection(x2, mod3, ln_g, w_in_bf, gains, tabs, seq, tm):
    ntok = x2.shape[0]
    tps = seq // tm
    tab_spec = pl.BlockSpec((tm, LANES), lambda i, j: (i % tps, 0))
    return pl.pallas_call(
        _proj_kernel,
        grid=(ntok // tm, len(PROJ_GROUPS)),
        in_specs=[
            pl.BlockSpec((tm, D_MODEL), lambda i, j: (i, 0)),
            pl.BlockSpec((1, 1, 3 * D_MODEL), lambda i, j: (i // tps, 0, 0)),
            pl.BlockSpec((1, D_MODEL), lambda i, j: (0, 0)),
            pl.BlockSpec((D_MODEL, GROUP_WIDTH), lambda i, j: (0, j)),
            pl.BlockSpec((1, 1, GROUP_WIDTH), lambda i, j: (j, 0, 0)),
            *([tab_spec] * len(tabs)),
        ],
        out_specs=pl.BlockSpec((tm, GROUP_WIDTH), lambda i, j: (i, j)),
        out_shape=jax.ShapeDtypeStruct((ntok, IN_WIDTH), BF16),
        scratch_shapes=[pltpu.VMEM((tm, D_MODEL), BF16), pltpu.VMEM((tm, D_MODEL), BF16),
                        pltpu.VMEM((D_MODEL // LANES, tm, LANES), F32),
                        pltpu.VMEM((2, tm, PROJ_TILE), F32)],
        compiler_params=_cparams(("arbitrary", "arbitrary")),
        name="in_proj",
    )(x2, mod3, ln_g, w_in_bf, gains, *tabs)


def _diff_attn_kernel(q_ref, k_ref, v_ref, lam_ref, sub_ref, o_ref, s_even_ref, s_odd_ref,
                      *, lam_init):
    t = pl.program_id(0)
    dn = (((1,), (1,)), ((), ()))

    def score(s_ref):
        q = q_ref[...]
        k = k_ref[...]
        lo = _first_map_lanes(q.shape)
        zero = jnp.zeros_like(q)
        s_ref[0] = lax.dot_general(jnp.where(lo, q, zero), k, dn, preferred_element_type=F32)
        s_ref[1] = lax.dot_general(jnp.where(lo, zero, q), k, dn, preferred_element_type=F32)

    def finish(s_ref):
        lq1, lk1, lq2, lk2 = (lam_ref[i:i + 1, :] for i in range(4))
        lam = (jnp.exp(jnp.sum(lq1 * lk1, axis=-1, keepdims=True))
               - jnp.exp(jnp.sum(lq2 * lk2, axis=-1, keepdims=True)) + lam_init)
        v = v_ref[...]
        v_ones = jnp.concatenate([v, jnp.ones_like(v)], axis=1)
        prods = []
        for m in range(2):
            s = s_ref[m]
            e = jnp.exp(s - jnp.max(s, axis=-1, keepdims=True))
            prods.append(jnp.dot(e.astype(BF16), v_ones, preferred_element_type=F32))
        (a0, a1) = prods
        o = a0[:, :LANES] / a0[:, LANES:] - lam * (a1[:, :LANES] / a1[:, LANES:])
        ms = jnp.mean(o * o, axis=-1, keepdims=True)
        o_ref[...] = (o * lax.rsqrt(ms + EPS) * sub_ref[...] * (1.0 - lam_init)).astype(BF16)

    @pl.when(t == 0)
    def _():
        s_odd_ref[...] = jnp.zeros_like(s_odd_ref)

    @pl.when(t % 2 == 0)
    def _():
        score(s_even_ref)
        finish(s_odd_ref)

    @pl.when(t % 2 == 1)
    def _():
        score(s_odd_ref)
        finish(s_even_ref)


def _diff_attention(proj3, lam_params, subln, lam_init, tq):
    nb, seq, _ = proj3.shape
    nq = seq // tq
    tiles = nb * A_HEADS * nq

    def split(t):
        return t // (A_HEADS * nq), (t // nq) % A_HEADS, t % nq

    def cur(t):
        return split(jnp.minimum(t, tiles - 1))

    def prev(t):
        return split(jnp.maximum(t - 1, 0))

    lam_spec = pl.BlockSpec((4, A_QK_DIM), lambda t: (0, 0))
    return pl.pallas_call(
        functools.partial(_diff_attn_kernel, lam_init=lam_init),
        grid=(tiles + 1,),
        in_specs=[
            pl.BlockSpec((None, tq, LANES), lambda t: (cur(t)[0], cur(t)[2], _col_block("AQ") + cur(t)[1])),
            pl.BlockSpec((None, seq, LANES), lambda t: (cur(t)[0], 0, _col_block("AK") + cur(t)[1])),
            pl.BlockSpec((None, seq, LANES), lambda t: (prev(t)[0], 0, _col_block("AV") + prev(t)[1])),
            lam_spec,
            pl.BlockSpec((1, LANES), lambda t: (0, 0)),
        ],
        out_specs=pl.BlockSpec((None, tq, LANES), lambda t: (prev(t)[0], prev(t)[2], prev(t)[1])),
        out_shape=jax.ShapeDtypeStruct((nb, seq, BRANCH_WIDTH), BF16),
        scratch_shapes=[pltpu.VMEM((2, tq, seq), F32), pltpu.VMEM((2, tq, seq), F32)],
        compiler_params=_cparams(("arbitrary",)),
        name="diff_attn",
    )(proj3, proj3, proj3, lam_params, subln)


def _na_bias_tables(rpb):
    rows = NA_QROWS + NA_KROWS
    qc = jnp.arange(GRID_W)
    kc = jnp.arange(GRID_W)
    cs = jnp.clip(qc - NA_KW // 2, 0, GRID_W - NA_KW)
    col_ok = (kc[None, :] >= cs[:, None]) & (kc[None, :] < cs[:, None] + NA_KW)
    col_off = jnp.clip(kc[None, :] - qc[:, None], -(NA_KW - 1), NA_KW - 1) + NA_KW - 1
    cols = jnp.where(col_ok, rpb.astype(F32)[:, :, col_off], NEG)
    pad = NA_KROWS
    cols = jnp.pad(cols, ((0, 0), (pad, pad + 1), (0, 0), (0, 0)), constant_values=NEG)
    pairs = jnp.concatenate([cols[:, :-1], cols[:, 1:]], axis=-1)
    kh = min(NA_KH, rows)
    tables = []
    for r0, ws in ((0, 0), (NA_QROWS, NA_QROWS - kh // 2), (rows - NA_QROWS, rows - NA_KROWS)):
        slabs = []
        for qr in range(NA_QROWS):
            r = r0 + qr
            rs = min(max(r - kh // 2, 0), rows - kh)
            first = ws - r + NA_KH - 1 + pad
            slab = pairs[:, first:first + NA_KROWS:2]
            kr = ws + jnp.arange(NA_KROWS)
            row_ok = (kr >= rs) & (kr < rs + kh)
            row_ok = jnp.repeat(row_ok.reshape(NA_KROWS // 2, 2), GRID_W, axis=1)
            slabs.append(jnp.where(row_ok[None, :, None, :], slab, NEG))
        t = jnp.stack(slabs, axis=1)
        tables.append(t.transpose(0, 1, 3, 2, 4).reshape(
            B_HEADS, NA_QROWS * GRID_W, NA_KROWS * GRID_W))
    return jnp.stack(tables)


def _na_kernel(q_ref, k_ref, v_ref, bias_ref, o_ref, s_even_ref, s_odd_ref, *, rows, nrb):
    t = pl.program_id(0)
    nkeys = NA_KROWS * GRID_W
    dn = (((1,), (1,)), ((), ()))

    def key_start(i):
        ws = jnp.clip(i * NA_QROWS - NA_KH // 2, 0, rows - NA_KROWS)
        return pl.multiple_of(ws * GRID_W, GRID_W)

    nq = NA_QROWS * GRID_W
    tiles_per_image = nrb // NA_STEP_BLOCKS

    def score(s_ref):
        first = (jnp.minimum(t, pl.num_programs(0) - 2) % tiles_per_image) * NA_STEP_BLOCKS
        for sub in range(NA_STEP_BLOCKS):
            i = first + sub
            kind = jnp.where(i == 0, 0, jnp.where(i == nrb - 1, 2, 1))
            kw = k_ref[pl.ds(key_start(i), nkeys), :]
            q = q_ref[sub * nq:(sub + 1) * nq, :]
            lo = _first_head_lanes(q.shape)
            zero = jnp.zeros_like(q)
            for e in range(2):
                qe = jnp.where(lo, q, zero) if e == 0 else jnp.where(lo, zero, q)
                s_ref[2 * sub + e] = (lax.dot_general(qe, kw, dn, preferred_element_type=F32)
                                      + bias_ref[kind, e])

    def finish(s_ref):
        first = (jnp.maximum(t - 1, 0) % tiles_per_image) * NA_STEP_BLOCKS
        for sub in range(NA_STEP_BLOCKS):
            vw = v_ref[pl.ds(key_start(first + sub), nkeys), :]
            v_ones = jnp.concatenate([vw, jnp.ones_like(vw)], axis=1)
            outs = []
            for e in range(2):
                s = s_ref[2 * sub + e]
                ex = jnp.exp(s - jnp.max(s, axis=-1, keepdims=True))
                a = jnp.dot(ex.astype(BF16), v_ones, preferred_element_type=F32)
                outs.append(a[:, :LANES] / a[:, LANES:])
            o_ref[sub * nq:(sub + 1) * nq, :] = jnp.where(
                _first_head_lanes(outs[0].shape), outs[0], outs[1]).astype(BF16)

    @pl.when(t == 0)
    def _():
        s_odd_ref[...] = jnp.zeros_like(s_odd_ref)

    @pl.when(t % 2 == 0)
    def _():
        score(s_even_ref)
        finish(s_odd_ref)

    @pl.when(t % 2 == 1)
    def _():
        score(s_odd_ref)
        finish(s_even_ref)


def _neighborhood_attention(proj3, bias):
    nb, seq, _ = proj3.shape
    rows = seq // GRID_W
    nrb = rows // NA_QROWS
    nq = NA_QROWS * GRID_W
    nk = NA_KROWS * GRID_W
    assert nrb % NA_STEP_BLOCKS == 0
    per_image = nrb // NA_STEP_BLOCKS
    tiles = (B_HEADS // 2) * nb * per_image

    def split(t):
        return t // (nb * per_image), (t // per_image) % nb, t % per_image

    def cur(t):
        return split(jnp.minimum(t, tiles - 1))

    def prev(t):
        return split(jnp.maximum(t - 1, 0))

    return pl.pallas_call(
        functools.partial(_na_kernel, rows=rows, nrb=nrb),
        grid=(tiles + 1,),
        in_specs=[
            pl.BlockSpec((None, NA_STEP_BLOCKS * nq, LANES),
                         lambda t: (cur(t)[1], cur(t)[2], _col_block("BQ") + cur(t)[0])),
            pl.BlockSpec((None, seq, LANES), lambda t: (cur(t)[1], 0, _col_block("BK") + cur(t)[0])),
            pl.BlockSpec((None, seq, LANES), lambda t: (prev(t)[1], 0, _col_block("BV") + prev(t)[0])),
            pl.BlockSpec((3, 2, nq, nk), lambda t: (0, cur(t)[0], 0, 0), pipeline_mode=pl.Buffered(1)),
        ],
        out_specs=pl.BlockSpec((None, NA_STEP_BLOCKS * nq, LANES),
                               lambda t: (prev(t)[1], prev(t)[2], prev(t)[0])),
        out_shape=jax.ShapeDtypeStruct((nb, seq, BRANCH_WIDTH), BF16),
        scratch_shapes=[pltpu.VMEM((2 * NA_STEP_BLOCKS, nq, nk), F32)] * 2,
        compiler_params=_cparams(("arbitrary",)),
        name="nbr_attn",
    )(proj3, proj3, proj3, bias)


def _softmax_av(s, v):
    m = jnp.max(s, axis=-1, keepdims=True)
    e = jnp.exp(s - m)
    v_ones = jnp.concatenate([v, jnp.ones_like(v)], axis=1)
    acc = jnp.dot(e.astype(BF16), v_ones, preferred_element_type=F32)
    l = acc[:, LANES:]
    return acc[:, :LANES] / l, m + jnp.log(l)


def _dil_kernel(q0_ref, q1_ref, q2_ref, k0_ref, k1_ref, k2_ref, v0_ref, v1_ref, v2_ref, o_ref,
                o0_ref, l0_ref, o0r_ref, l0r_ref, o1_ref, l1_ref, o2_ref, l2_ref,
                band_ref, band4_ref, band16_ref, *, seq, tile):
    scale = C_HEAD_DIM ** -0.5
    dn = (((1,), (1,)), ((), ()))
    run = tile // C_TOP_DIL
    n_tiles = seq // tile
    sub16 = seq // C_TOP_DIL
    hw = C_HALF_WINDOW

    full_win = C_QBLK + 2 * hw
    pack = band16_ref.shape[0] // sub16

    @pl.when((pl.program_id(0) == 0) & (pl.program_id(1) == 0))
    def _():
        qi = lax.broadcasted_iota(jnp.int32, (C_QBLK, full_win), 0)
        kj = lax.broadcasted_iota(jnp.int32, (C_QBLK, full_win), 1)
        for lead in range(3):
            band_ref[lead] = jnp.where(jnp.abs(kj - qi - lead * hw) <= hw, 0.0, NEG)
        qi = lax.broadcasted_iota(jnp.int32, band16_ref.shape, 0)
        kj = lax.broadcasted_iota(jnp.int32, band16_ref.shape, 1)
        same_class = (qi // sub16) == (kj // sub16)
        band16_ref[...] = jnp.where(same_class & (jnp.abs(kj - qi) <= hw), 0.0, NEG)
        g1_keys = C_G1_RUNS * C_G1_KRUN
        qr = lax.broadcasted_iota(jnp.int32, (C_QBLK, g1_keys), 0)
        kc = lax.broadcasted_iota(jnp.int32, (C_QBLK, g1_keys), 1)
        q_pos = C_G1_RUNS * (qr % C_G1_QRUN) + qr // C_G1_QRUN
        k_pos = C_G1_RUNS * (kc % C_G1_KRUN) + kc // C_G1_KRUN
        for lead in range(3):
            shift = C_G1_RUNS * C_G1_REACH * lead
            band4_ref[lead] = jnp.where(jnp.abs(q_pos + shift - k_pos) <= hw, 0.0, NEG)

    def attend0(n):
        a = n * C_QBLK
        ws = jnp.clip(a - hw, 0, seq - full_win)
        rows = pl.ds(pl.multiple_of(a, C_QBLK), C_QBLK)
        kw = k0_ref[pl.ds(pl.multiple_of(ws, hw), full_win), :]
        vw = v0_ref[pl.ds(pl.multiple_of(ws, hw), full_win), :]
        s = lax.dot_general(q0_ref[rows, :], kw, dn, preferred_element_type=F32) * scale
        o0_ref[rows, :], l0_ref[rows, :] = _softmax_av(s + band_ref[(a - ws) // hw], vw)

    n_blocks = seq // C_QBLK
    unroll = min(C_UNROLL, n_blocks)

    def blocks0(it, carry):
        for u in range(unroll):
            attend0(it * unroll + u)
        return carry

    lax.fori_loop(0, n_blocks // unroll, blocks0, 0)

    def tile_row(j):
        return (j // run) * tile + j % run

    def attend1(n):
        blocks_per_class = sub16 // C_G1_QRUN
        rho4 = n // blocks_per_class
        j0 = (n % blocks_per_class) * C_G1_QRUN
        jw = jnp.clip(j0 - C_G1_REACH, 0, sub16 - C_G1_KRUN)
        class_row = [(C_DILATIONS[1] * c + rho4) * run for c in range(C_G1_RUNS)]
        q_base = tile_row(j0)
        k_base = [tile_row(jw + BF16_ROWS * p) for p in range(C_G1_KRUN // BF16_ROWS)]
        q_rows = [pl.ds(pl.multiple_of(q_base + class_row[c], C_G1_QRUN), C_G1_QRUN)
                  for c in range(C_G1_RUNS)]
        k_rows = [pl.ds(pl.multiple_of(base + class_row[c], BF16_ROWS), BF16_ROWS)
                  for c in range(C_G1_RUNS) for base in k_base]
        q = jnp.concatenate([q1_ref[r, :] for r in q_rows], axis=0)
        kw = jnp.concatenate([k1_ref[r, :] for r in k_rows], axis=0)
        vw = jnp.concatenate([v1_ref[r, :] for r in k_rows], axis=0)
        s = lax.dot_general(q, kw, dn, preferred_element_type=F32) * scale
        o, lse = _softmax_av(s + band4_ref[(j0 - jw) // C_G1_REACH], vw)
        for c, r in enumerate(q_rows):
            o1_ref[r, :] = o[c * C_G1_QRUN:(c + 1) * C_G1_QRUN]
            l1_ref[r, :] = lse[c * C_G1_QRUN:(c + 1) * C_G1_QRUN]

    def blocks1(it, carry):
        for u in range(unroll):
            attend1(it * unroll + u)
        return carry

    lax.fori_loop(0, n_blocks // unroll, blocks1, 0)

    def attend2(n):
        rows = [pl.ds(pl.multiple_of(t * tile + (n * pack + u) * run, run), run)
                for u in range(pack) for t in range(n_tiles)]
        q = jnp.concatenate([q2_ref[r, :] for r in rows], axis=0)
        kw = jnp.concatenate([k2_ref[r, :] for r in rows], axis=0)
        vw = jnp.concatenate([v2_ref[r, :] for r in rows], axis=0)
        s = lax.dot_general(q, kw, dn, preferred_element_type=F32) * scale
        o, lse = _softmax_av(s + band16_ref[...], vw)
        for i, r in enumerate(rows):
            o2_ref[r, :] = o[i * run:(i + 1) * run]
            l2_ref[r, :] = lse[i * run:(i + 1) * run]

    unroll2 = min(C_UNROLL, C_TOP_DIL // pack)

    def blocks2(it, carry):
        for u in range(unroll2):
            attend2(it * unroll2 + u)
        return carry

    lax.fori_loop(0, C_TOP_DIL // pack // unroll2, blocks2, 0)

    for t in range(n_tiles):
        for rho in range(C_TOP_DIL):
            dst = slice(t * tile + rho * run, t * tile + (rho + 1) * run)
            o0r_ref[dst, :] = o0_ref[pl.ds(t * tile + rho, run, stride=C_TOP_DIL), :]
            l0r_ref[dst, :] = l0_ref[pl.ds(t * tile + rho, run, stride=C_TOP_DIL), :]
    l0, l1, l2 = l0r_ref[...], l1_ref[...], l2_ref[...]
    l_max = jnp.maximum(jnp.maximum(l0, l1), l2)
    w0, w1, w2 = jnp.exp(l0 - l_max), jnp.exp(l1 - l_max), jnp.exp(l2 - l_max)
    o0_ref[...] = (w0 * o0r_ref[...] + w1 * o1_ref[...] + w2 * o2_ref[...]) * (1.0 / (w0 + w1 + w2))

    for t in range(n_tiles):
        def to_token_order(j, carry, t=t):
            src = o0_ref[pl.ds(t * tile + j, C_TOP_DIL, stride=run), :]
            dst = pl.ds(pl.multiple_of(t * tile + j * C_TOP_DIL, C_TOP_DIL), C_TOP_DIL)
            o_ref[dst, :] = src.astype(BF16)
            return carry

        lax.fori_loop(0, run, to_token_order, 0, unroll=8)


def _dilated_mixture(proj3, tile):
    nb, seq, _ = proj3.shape
    assert seq % tile == 0 and tile % (C_TOP_DIL * C_G1_QRUN) == 0
    assert seq >= C_QBLK + 2 * C_HALF_WINDOW and (seq // C_QBLK) % min(C_UNROLL, seq // C_QBLK) == 0
    assert (seq // C_TOP_DIL) % C_G1_KRUN == 0
    side2 = max(seq // C_TOP_DIL, C_QBLK + 2 * C_HALF_WINDOW)

    def spec(kind, g):
        cb0 = _col_block(f"{kind}{g}")
        return pl.BlockSpec((None, seq, LANES), lambda b, h, cb0=cb0: (b, 0, cb0 + h))

    in_specs = [spec(kind, g) for kind in ("CQ", "CK", "CV") for g in range(len(C_DILATIONS))]
    return pl.pallas_call(
        functools.partial(_dil_kernel, seq=seq, tile=tile),
        grid=(nb, C_HEADS_PER_GROUP),
        in_specs=in_specs,
        out_specs=pl.BlockSpec((None, seq, LANES), lambda b, h: (b, 0, h)),
        out_shape=jax.ShapeDtypeStruct((nb, seq, BRANCH_WIDTH), BF16),
        scratch_shapes=(
            [pltpu.VMEM((seq, LANES), F32)] * 8
            + [pltpu.VMEM((3, C_QBLK, C_QBLK + 2 * C_HALF_WINDOW), F32)] * 2
            + [pltpu.VMEM((side2, side2), F32)]),
        compiler_params=_cparams(("arbitrary", "arbitrary")),
        name="dil_attn",
    )(*([proj3] * 9))


def _tail_kernel(ya_ref, yb_ref, yc_ref, z0_ref, z1_ref, z2_ref, g0a_ref, g0b_ref, g1a_ref,
                 g1b_ref, g2a_ref, g2b_ref, x_ref, gate_ref, wbr_ref, wout_ref, o_ref):
    halves = [None, None]
    branches = ((ya_ref, z0_ref, (g0a_ref, g0b_ref)), (yb_ref, z1_ref, (g1a_ref, g1b_ref)),
                (yc_ref, z2_ref, (g2a_ref, g2b_ref)))
    for i, (y_ref, z_ref, g_refs) in enumerate(branches):
        yz = (y_ref[...].astype(F32) * z_ref[...].astype(F32)).astype(BF16)
        for half, g_ref in enumerate(g_refs):
            cols = slice(half * PROJ_TILE, (half + 1) * PROJ_TILE)
            u = g_ref[...].astype(F32) * jnp.dot(yz, wbr_ref[i, :, cols], preferred_element_type=F32)
            halves[half] = u if halves[half] is None else halves[half] + u
    merged = jnp.concatenate(halves, axis=1).astype(BF16)
    out = jnp.dot(merged, wout_ref[...], preferred_element_type=F32)
    o_ref[...] = x_ref[...] + gate_ref[0] * out


def _tail(ya, yb, yc, proj2, x2, mod3, w_br_bf, w_out_bf, seq, tm):
    ntok = x2.shape[0]
    tps = seq // tm
    y_spec = pl.BlockSpec((None, tm, BRANCH_WIDTH), lambda i: (i // tps, i % tps, 0))

    def tile_spec(name):
        return pl.BlockSpec((tm, PROJ_TILE), lambda i, s=SLOT[name]: (i, s))

    z_specs = [tile_spec(f"Z{t}") for t in range(3)]
    g_specs = [tile_spec(f"G{t}{half}") for t in range(3) for half in "ab"]
    return pl.pallas_call(
        _tail_kernel,
        grid=(ntok // tm,),
        in_specs=[y_spec, y_spec, y_spec, *z_specs, *g_specs,
                  pl.BlockSpec((tm, D_MODEL), lambda i: (i, 0)),
                  pl.BlockSpec((1, 1, D_MODEL), lambda i: (i // tps, 0, 2)),
                  pl.BlockSpec((3, BRANCH_WIDTH, D_MODEL), lambda i: (0, 0, 0)),
                  pl.BlockSpec((D_MODEL, D_MODEL), lambda i: (0, 0))],
        out_specs=pl.BlockSpec((tm, D_MODEL), lambda i: (i, 0)),
        out_shape=jax.ShapeDtypeStruct((ntok, D_MODEL), F32),
        compiler_params=_cparams(("arbitrary",)),
        name="tail",
    )(ya, yb, yc, *([proj2] * 9), x2, mod3, w_br_bf, w_out_bf)


def _rope_tables(seq, tile):
    def base(d):
        inv = ROPE_THETA ** (-jnp.arange(0, d, 2, dtype=F32) / d)
        ang = jnp.arange(seq, dtype=F32)[:, None] * inv[None, :]
        return jnp.cos(ang), jnp.sin(ang)

    cos_a, sin_a = base(A_QK_DIM)
    ca = jnp.tile(cos_a, (1, 4))
    sa = jnp.concatenate([-sin_a, -sin_a, sin_a, sin_a], axis=-1)
    cos_c, sin_c = base(C_HEAD_DIM)
    cc = jnp.tile(cos_c, (1, 2))
    sc = jnp.concatenate([-sin_c, sin_c], axis=-1)

    def residue_major(tab):
        t4 = tab.reshape(seq // tile, tile // C_TOP_DIL, C_TOP_DIL, LANES)
        return t4.transpose(0, 2, 1, 3).reshape(seq, LANES)

    return ca, sa, cc, sc, residue_major(cc), residue_major(sc)


def _to_stored_columns(w):
    lead = w.shape[:-1]
    tiles = {name: w[..., i * PROJ_TILE:(i + 1) * PROJ_TILE] for i, name in enumerate(TILE_NAMES)}
    for name in ("AQ", "AK"):
        t = tiles[name].reshape(*lead, A_HEADS, 2, 2, A_QK_DIM // 2)
        tiles[name] = jnp.swapaxes(t, -3, -2).reshape(*lead, PROJ_TILE)
    return jnp.concatenate([tiles[n] for grp in PROJ_GROUPS for n in grp], axis=-1)


def _proj_gains(qn_a, kn_a, qn_b, kn_b, qn_c, kn_c):
    per_tile = {"AQ": jnp.tile(qn_a, PROJ_TILE // A_QK_DIM), "AK": jnp.tile(kn_a, PROJ_TILE // A_QK_DIM),
                "BQ": jnp.tile(qn_b, PROJ_TILE // B_HEAD_DIM), "BK": jnp.tile(kn_b, PROJ_TILE // B_HEAD_DIM)}
    for g in range(len(C_DILATIONS)):
        per_tile[f"CQ{g}"] = jnp.tile(qn_c, PROJ_TILE // C_HEAD_DIM)
        per_tile[f"CK{g}"] = jnp.tile(kn_c, PROJ_TILE // C_HEAD_DIM)
    ones = jnp.ones((PROJ_TILE,), F32)
    full = jnp.concatenate([per_tile.get(name, ones).astype(F32) for name in TILE_NAMES])
    return _to_stored_columns(full).reshape(len(PROJ_GROUPS), 1, GROUP_WIDTH)


def _proj_token_tile(seq):
    return min(512, seq)


def _tail_token_tile(seq):
    return min(512, seq)


def _encoder_layer(x, mod3, layer_idx, tabs, ln_g, w_in_bf, gains, lam_params, subln, na_bias,
                   w_br_bf, w_out_bf):
    nb, seq, _ = x.shape
    x2 = x.reshape(nb * seq, D_MODEL)
    proj2 = _projection(x2, mod3, ln_g, w_in_bf, gains, tabs, seq, _proj_token_tile(seq))
    proj3 = proj2.reshape(nb, seq, IN_WIDTH)
    lam_init = 0.8 - 0.6 * math.exp(-0.3 * layer_idx)
    tq = A_SCORE_BYTES // (2 * 2 * seq * 4)
    ya = _diff_attention(proj3, lam_params, subln, lam_init, tq=tq)
    yb = _neighborhood_attention(proj3, na_bias)
    yc = _dilated_mixture(proj3, _proj_token_tile(seq))
    y2 = _tail(ya, yb, yc, proj2, x2, mod3, w_br_bf, w_out_bf, seq, _tail_token_tile(seq))
    return y2.reshape(nb, seq, D_MODEL)


def kernel(x_prompt, x_sample, c_prompt, c_sample, ln_g, w_ada, b_ada, w_in, qn_a, kn_a, lam_q1, lam_k1, lam_q2, lam_k2, subln_a, qn_b, kn_b, rpb_b, qn_c, kn_c, w_br, w_out):
    depth = w_in.shape[0]
    n_prompt = c_prompt.shape[0]
    mod_all = _modulation(jnp.concatenate([c_prompt, c_sample], axis=0), w_ada, b_ada)
    w_in_bf = _to_stored_columns(w_in.astype(BF16))
    w_br_bf = w_br.astype(BF16)
    w_out_bf = w_out.astype(BF16)
    gains = [_proj_gains(qn_a[l], kn_a[l], qn_b[l], kn_b[l], qn_c[l], kn_c[l]) for l in range(depth)]
    na_bias = [_na_bias_tables(rpb_b[l]) for l in range(depth)]

    def run(x, mod):
        nb, seq, _ = x.shape
        assert seq % (NA_QROWS * GRID_W) == 0 and seq // GRID_W >= NA_KROWS
        tabs = _rope_tables(seq, _proj_token_tile(seq))
        for l in range(depth):
            lam_params = jnp.stack([p[l] for p in (lam_q1, lam_k1, lam_q2, lam_k2)])
            x = _encoder_layer(x, mod[l].reshape(nb, 1, 3 * D_MODEL), l, tabs,
                               ln_g[l].reshape(1, D_MODEL), w_in_bf[l], gains[l], lam_params,
                               subln_a[l].reshape(1, LANES), na_bias[l], w_br_bf[l], w_out_bf[l])
        return x

    y_prompt = run(x_prompt, mod_all[:, :n_prompt])
    y_sample = run(x_sample, mod_all[:, n_prompt:])
    return (y_prompt, y_sample)
```

```python
import functools
import math

import jax
import jax.numpy as jnp
from jax import lax
from jax.experimental import pallas as pl
from jax.experimental.pallas import tpu as pltpu

F32 = jnp.float32
BF16 = jnp.bfloat16

D_MODEL = 1024
GRID_W = 64
BRANCH_WIDTH = 512
ROPE_THETA = 10000.0
EPS = 1e-6
NEG = -1e30
A_QK_DIM = 64
A_HEADS = 4
B_HEAD_DIM = 64
B_HEADS = 8
NA_KH = 8
NA_KW = 16
C_DILATIONS = (1, 4, 16)
C_HALF_WINDOW = 64
C_HEAD_DIM = 128
C_HEADS_PER_GROUP = 4
IN_WIDTH = 12288
LANES = 128

PROJ_TILE = 512
TILE_NAMES = ("AQ", "AK", "AV", "BQ", "BK", "BV", "CQ0", "CQ1", "CQ2", "CK0", "CK1", "CK2",
              "CV0", "CV1", "CV2", "Z0", "Z1", "Z2", "G0a", "G0b", "G1a", "G1b", "G2a", "G2b")
PROJ_GROUPS = (("AQ", "AK", "BQ", "BK", "Z0", "G0a", "AV", "BV"),
               ("CQ0", "CK0", "CQ1", "Z1", "G0b", "G1a", "G1b", "CV0"),
               ("CK1", "CQ2", "CK2", "Z2", "G2a", "G2b", "CV1", "CV2"))
C_TOP_DIL = C_DILATIONS[-1]
RESIDUE_MAJOR_TILES = tuple(f"C{kind}{g}" for kind in "QKV" for g in (1, 2))
GROUP_TILES = len(PROJ_GROUPS[0])
GROUP_WIDTH = GROUP_TILES * PROJ_TILE
SLOT = {name: i for i, name in enumerate(n for grp in PROJ_GROUPS for n in grp)}


def _col_block(name, sub=0):
    return SLOT[name] * (PROJ_TILE // LANES) + sub

VMEM_LIMIT = 56 * 1024 * 1024

NA_QROWS = 8
NA_KROWS = 16
NA_STEP_BLOCKS = 4
A_SCORE_BYTES = 32 * 1024 * 1024
C_QBLK = 128
C_UNROLL = 32
C_G1_RUNS = C_TOP_DIL // C_DILATIONS[1]
C_G1_QRUN = C_QBLK // C_G1_RUNS
C_G1_REACH = C_HALF_WINDOW // C_DILATIONS[1]
C_G1_KRUN = C_G1_QRUN + 2 * C_G1_REACH
BF16_ROWS = 16
QK64_SCALE = A_QK_DIM ** -0.5


def _cparams(sem):
    return pltpu.CompilerParams(dimension_semantics=sem, vmem_limit_bytes=VMEM_LIMIT)


def _sigmoid(x):
    return 1.0 / (1.0 + jnp.exp(-x))


def _mod_kernel(c_ref, w_ref, b_ref, o_ref):
    c = c_ref[...]
    o_ref[0] = jnp.dot(c * _sigmoid(c), w_ref[0], preferred_element_type=F32) + b_ref[0]


def _modulation(c_all, w_ada, b_ada):
    depth = w_ada.shape[0]
    nb = c_all.shape[0]
    return pl.pallas_call(
        _mod_kernel,
        grid=(depth, 3),
        in_specs=[
            pl.BlockSpec((nb, D_MODEL), lambda l, j: (0, 0)),
            pl.BlockSpec((1, D_MODEL, D_MODEL), lambda l, j: (l, 0, j)),
            pl.BlockSpec((1, 1, D_MODEL), lambda l, j: (l, 0, j)),
        ],
        out_specs=pl.BlockSpec((1, nb, D_MODEL), lambda l, j: (l, 0, j)),
        out_shape=jax.ShapeDtypeStruct((depth, nb, 3 * D_MODEL), F32),
        compiler_params=_cparams(("arbitrary", "arbitrary")),
        name="adaln_mod",
    )(c_all, w_ada, b_ada.reshape(depth, 1, 3 * D_MODEL))


def _first_map_lanes(shape):
    return (lax.broadcasted_iota(jnp.int32, shape, 1) & (A_QK_DIM // 2)) == 0


def _first_head_lanes(shape):
    return lax.broadcasted_iota(jnp.int32, shape, 1) < 64


def _rms_halves(xb, first):
    sq = xb * xb
    s_a = jnp.sum(jnp.where(first, sq, 0.0), axis=-1, keepdims=True)
    s_b = jnp.sum(jnp.where(first, 0.0, sq), axis=-1, keepdims=True)
    return lax.rsqrt(jnp.where(first, s_a, s_b) * (1.0 / A_QK_DIM) + EPS)


def _rms128(xb):
    return lax.rsqrt(jnp.mean(xb * xb, axis=-1, keepdims=True) + EPS)


def _rope(xb, cos, sin_signed):
    return xb * cos + pltpu.roll(xb, 64, 1) * sin_signed


def _tile_epilogue(name, xb, gain, tabs):
    ca_ref, sa_ref, cc_ref, sc_ref, ccr_ref, scr_ref = tabs
    if name in RESIDUE_MAJOR_TILES:
        cc_ref, sc_ref = ccr_ref, scr_ref
    if name == "AQ":
        rms = _rms_halves(xb, _first_map_lanes(xb.shape))
        return _rope(xb * (rms * gain), ca_ref[...], sa_ref[...]) * QK64_SCALE
    if name == "AK":
        rms = _rms_halves(xb, _first_map_lanes(xb.shape))
        return _rope(xb * (rms * gain), ca_ref[...], sa_ref[...])
    if name == "BQ":
        return xb * gain * (_rms_halves(xb, _first_head_lanes(xb.shape)) * QK64_SCALE)
    if name == "BK":
        return xb * gain * _rms_halves(xb, _first_head_lanes(xb.shape))
    if name[:2] in ("CQ", "CK"):
        return _rope(xb * gain, cc_ref[...], sc_ref[...]) * _rms128(xb)
    if name[0] == "Z":
        return xb * _sigmoid(xb)
    if name[0] == "G":
        return _sigmoid(xb)
    return xb


def _proj_kernel(x_ref, mod_ref, lng_ref, w_ref, gain_ref, ca_ref, sa_ref, cc_ref, sc_ref,
                 ccr_ref, scr_ref, o_ref, h_ref, hr_ref, hstage_ref, acc_ref):
    grp = pl.program_id(1)
    tm = h_ref.shape[0]
    run = tm // C_TOP_DIL

    @pl.when(grp == 0)
    def _():
        x = x_ref[...]
        ms = jnp.mean(x * x, axis=-1, keepdims=True)
        y = x * lax.rsqrt(ms + EPS) * lng_ref[...]
        mod = mod_ref[0]
        shift = mod[:, :D_MODEL]
        scale = mod[:, D_MODEL:2 * D_MODEL]
        h = y * (1.0 + scale) + shift
        h_ref[...] = h.astype(BF16)
        for cb in range(D_MODEL // LANES):
            hstage_ref[cb] = h[:, cb * LANES:(cb + 1) * LANES]

    def regroup_rows():
        for cb in range(D_MODEL // LANES):
            for rho in range(C_TOP_DIL):
                hr_ref[rho * run:(rho + 1) * run, cb * LANES:(cb + 1) * LANES] = hstage_ref[
                    cb, pl.ds(rho, run, stride=C_TOP_DIL), :].astype(BF16)

    assert not any(name in RESIDUE_MAJOR_TILES for name in PROJ_GROUPS[0])
    tabs = (ca_ref, sa_ref, cc_ref, sc_ref, ccr_ref, scr_ref)
    for gi, names in enumerate(PROJ_GROUPS):
        @pl.when(grp == gi)
        def _(names=names, gi=gi):
            gain = gain_ref[0]
            for t, name in enumerate(names):
                acc = acc_ref.at[t % 2]
                lhs_ref = hr_ref if name in RESIDUE_MAJOR_TILES else h_ref
                acc[...] = jnp.dot(lhs_ref[...], w_ref[:, t * PROJ_TILE:(t + 1) * PROJ_TILE],
                                   preferred_element_type=F32)
                for cb in range(PROJ_TILE // LANES):
                    sl = slice(cb * LANES, (cb + 1) * LANES)
                    out = slice(t * PROJ_TILE + cb * LANES, t * PROJ_TILE + (cb + 1) * LANES)
                    o_ref[:, out] = _tile_epilogue(name, acc[:, sl], gain[:, out], tabs).astype(BF16)
                if gi == 0 and t == 0:
                    regroup_rows()


def _projection(x2, mod3, ln_g, w_in_bf, gains, tabs, seq, tm):
    ntok = x2.shape[0]
    tps = seq // tm
    tab_spec = pl.BlockSpec((tm, LANES), lambda i, j: (i % tps, 0))
    return pl.pallas_call(
        _proj_kernel,
        grid=(ntok // tm, len(PROJ_GROUPS)),
        in_specs=[
            pl.BlockSpec((tm, D_MODEL), lambda i, j: (i, 0)),
            pl.BlockSpec((1, 1, 3 * D_MODEL), lambda i, j: (i // tps, 0, 0)),
            pl.BlockSpec((1, D_MODEL), lambda i, j: (0, 0)),
            pl.BlockSpec((D_MODEL, GROUP_WIDTH), lambda i, j: (0, j)),
            pl.BlockSpec((1, 1, GROUP_WIDTH), lambda i, j: (j, 0, 0)),
            *([tab_spec] * len(tabs)),
        ],
        out_specs=pl.BlockSpec((tm, GROUP_WIDTH), lambda i, j: (i, j)),
        out_shape=jax.ShapeDtypeStruct((ntok, IN_WIDTH), BF16),
        scratch_shapes=[pltpu.VMEM((tm, D_MODEL), BF16), pltpu.VMEM((tm, D_MODEL), BF16),
                        pltpu.VMEM((D_MODEL // LANES, tm, LANES), F32),
                        pltpu.VMEM((2, tm, PROJ_TILE), F32)],
        compiler_params=_cparams(("arbitrary", "arbitrary")),
        name="in_proj",
    )(x2, mod3, ln_g, w_in_bf, gains, *tabs)


def _diff_attn_kernel(q_ref, k_ref, v_ref, lam_ref, sub_ref, o_ref, s_even_ref, s_odd_ref,
                      *, lam_init):
    t = pl.program_id(0)
    dn = (((1,), (1,)), ((), ()))

    def score(s_ref):
        q = q_ref[...]
        k = k_ref[...]
        lo = _first_map_lanes(q.shape)
        zero = jnp.zeros_like(q)
        s_ref[0] = lax.dot_general(jnp.where(lo, q, zero), k, dn, preferred_element_type=F32)
        s_ref[1] = lax.dot_general(jnp.where(lo, zero, q), k, dn, preferred_element_type=F32)

    def finish(s_ref):
        lq1, lk1, lq2, lk2 = (lam_ref[i:i + 1, :] for i in range(4))
        lam = (jnp.exp(jnp.sum(lq1 * lk1, axis=-1, keepdims=True))
               - jnp.exp(jnp.sum(lq2 * lk2, axis=-1, keepdims=True)) + lam_init)
        v = v_ref[...]
        v_ones = jnp.concatenate([v, jnp.ones_like(v)], axis=1)
        prods = []
        for m in range(2):
            s = s_ref[m]
            e = jnp.exp(s - jnp.max(s, axis=-1, keepdims=True))
            prods.append(jnp.dot(e.astype(BF16), v_ones, preferred_element_type=F32))
        (a0, a1) = prods
        o = a0[:, :LANES] / a0[:, LANES:] - lam * (a1[:, :LANES] / a1[:, LANES:])
        ms = jnp.mean(o * o, axis=-1, keepdims=True)
        o_ref[...] = (o * lax.rsqrt(ms + EPS) * sub_ref[...] * (1.0 - lam_init)).astype(BF16)

    @pl.when(t == 0)
    def _():
        s_odd_ref[...] = jnp.zeros_like(s_odd_ref)

    @pl.when(t % 2 == 0)
    def _():
        score(s_even_ref)
        finish(s_odd_ref)

    @pl.when(t % 2 == 1)
    def _():
        score(s_odd_ref)
        finish(s_even_ref)


def _diff_attention(proj3, lam_params, subln, lam_init, tq):
    nb, seq, _ = proj3.shape
    nq = seq // tq
    tiles = nb * A_HEADS * nq

    def split(t):
        return t // (A_HEADS * nq), (t // nq) % A_HEADS, t % nq

    def cur(t):
        return split(jnp.minimum(t, tiles - 1))

    def prev(t):
        return split(jnp.maximum(t - 1, 0))

    lam_spec = pl.BlockSpec((4, A_QK_DIM), lambda t: (0, 0))
    return pl.pallas_call(
        functools.partial(_diff_attn_kernel, lam_init=lam_init),
        grid=(tiles + 1,),
        in_specs=[
            pl.BlockSpec((None, tq, LANES), lambda t: (cur(t)[0], cur(t)[2], _col_block("AQ") + cur(t)[1])),
            pl.BlockSpec((None, seq, LANES), lambda t: (cur(t)[0], 0, _col_block("AK") + cur(t)[1])),
            pl.BlockSpec((None, seq, LANES), lambda t: (prev(t)[0], 0, _col_block("AV") + prev(t)[1])),
            lam_spec,
            pl.BlockSpec((1, LANES), lambda t: (0, 0)),
        ],
        out_specs=pl.BlockSpec((None, tq, LANES), lambda t: (prev(t)[0], prev(t)[2], prev(t)[1])),
        out_shape=jax.ShapeDtypeStruct((nb, seq, BRANCH_WIDTH), BF16),
        scratch_shapes=[pltpu.VMEM((2, tq, seq), F32), pltpu.VMEM((2, tq, seq), F32)],
        compiler_params=_cparams(("arbitrary",)),
        name="diff_attn",
    )(proj3, proj3, proj3, lam_params, subln)


def _na_bias_tables(rpb):
    rows = NA_QROWS + NA_KROWS
    qc = jnp.arange(GRID_W)
    kc = jnp.arange(GRID_W)
    cs = jnp.clip(qc - NA_KW // 2, 0, GRID_W - NA_KW)
    col_ok = (kc[None, :] >= cs[:, None]) & (kc[None, :] < cs[:, None] + NA_KW)
    col_off = jnp.clip(kc[None, :] - qc[:, None], -(NA_KW - 1), NA_KW - 1) + NA_KW - 1
    cols = jnp.where(col_ok, rpb.astype(F32)[:, :, col_off], NEG)
    pad = NA_KROWS
    cols = jnp.pad(cols, ((0, 0), (pad, pad), (0, 0), (0, 0)), constant_values=NEG)
    kh = min(NA_KH, rows)
    tables = []
    for r0, ws in ((0, 0), (NA_QROWS, NA_QROWS - kh // 2), (rows - NA_QROWS, rows - NA_KROWS)):
        slabs = []
        for qr in range(NA_QROWS):
            r = r0 + qr
            rs = min(max(r - kh // 2, 0), rows - kh)
            first = ws - r + NA_KH - 1 + pad
            slab = cols[:, first:first + NA_KROWS]
            kr = ws + jnp.arange(NA_KROWS)
            row_ok = (kr >= rs) & (kr < rs + kh)
            slabs.append(jnp.where(row_ok[None, :, None, None], slab, NEG))
        t = jnp.stack(slabs, axis=1)
        tables.append(t.transpose(0, 1, 3, 2, 4).reshape(
            B_HEADS, NA_QROWS * GRID_W, NA_KROWS * GRID_W))
    return jnp.stack(tables)


def _na_kernel(q_ref, k_ref, v_ref, bias_ref, o_ref, s_even_ref, s_odd_ref, *, rows, nrb):
    t = pl.program_id(0)
    nkeys = NA_KROWS * GRID_W
    dn = (((1,), (1,)), ((), ()))

    def key_start(i):
        ws = jnp.clip(i * NA_QROWS - NA_KH // 2, 0, rows - NA_KROWS)
        return pl.multiple_of(ws * GRID_W, GRID_W)

    nq = NA_QROWS * GRID_W
    tiles_per_image = nrb // NA_STEP_BLOCKS

    def score(s_ref):
        first = (jnp.minimum(t, pl.num_programs(0) - 2) % tiles_per_image) * NA_STEP_BLOCKS
        for sub in range(NA_STEP_BLOCKS):
            i = first + sub
            kind = jnp.where(i == 0, 0, jnp.where(i == nrb - 1, 2, 1))
            kw = k_ref[pl.ds(key_start(i), nkeys), :]
            q = q_ref[sub * nq:(sub + 1) * nq, :]
            lo = _first_head_lanes(q.shape)
            zero = jnp.zeros_like(q)
            for e in range(2):
                qe = jnp.where(lo, q, zero) if e == 0 else jnp.where(lo, zero, q)
                s_ref[2 * sub + e] = (lax.dot_general(qe, kw, dn, preferred_element_type=F32)
                                      + bias_ref[kind, e])

    def finish(s_ref):
        first = (jnp.maximum(t - 1, 0) % tiles_per_image) * NA_STEP_BLOCKS
        for sub in range(NA_STEP_BLOCKS):
            vw = v_ref[pl.ds(key_start(first + sub), nkeys), :]
            v_ones = jnp.concatenate([vw, jnp.ones_like(vw)], axis=1)
            outs = []
            for e in range(2):
                s = s_ref[2 * sub + e]
                ex = jnp.exp(s - jnp.max(s, axis=-1, keepdims=True))
                a = jnp.dot(ex.astype(BF16), v_ones, preferred_element_type=F32)
                outs.append(a[:, :LANES] / a[:, LANES:])
            o_ref[sub * nq:(sub + 1) * nq, :] = jnp.where(
                _first_head_lanes(outs[0].shape), outs[0], outs[1]).astype(BF16)

    @pl.when(t == 0)
    def _():
        s_odd_ref[...] = jnp.zeros_like(s_odd_ref)

    @pl.when(t % 2 == 0)
    def _():
        score(s_even_ref)
        finish(s_odd_ref)

    @pl.when(t % 2 == 1)
    def _():
        score(s_odd_ref)
        finish(s_even_ref)


def _neighborhood_attention(proj3, bias):
    nb, seq, _ = proj3.shape
    rows = seq // GRID_W
    nrb = rows // NA_QROWS
    nq = NA_QROWS * GRID_W
    nk = NA_KROWS * GRID_W
    assert nrb % NA_STEP_BLOCKS == 0
    per_image = nrb // NA_STEP_BLOCKS
    tiles = (B_HEADS // 2) * nb * per_image

    def split(t):
        return t // (nb * per_image), (t // per_image) % nb, t % per_image

    def cur(t):
        return split(jnp.minimum(t, tiles - 1))

    def prev(t):
        return split(jnp.maximum(t - 1, 0))

    return pl.pallas_call(
        functools.partial(_na_kernel, rows=rows, nrb=nrb),
        grid=(tiles + 1,),
        in_specs=[
            pl.BlockSpec((None, NA_STEP_BLOCKS * nq, LANES),
                         lambda t: (cur(t)[1], cur(t)[2], _col_block("BQ") + cur(t)[0])),
            pl.BlockSpec((None, seq, LANES), lambda t: (cur(t)[1], 0, _col_block("BK") + cur(t)[0])),
            pl.BlockSpec((None, seq, LANES), lambda t: (prev(t)[1], 0, _col_block("BV") + prev(t)[0])),
            pl.BlockSpec((3, 2, nq, nk), lambda t: (0, cur(t)[0], 0, 0), pipeline_mode=pl.Buffered(1)),
        ],
        out_specs=pl.BlockSpec((None, NA_STEP_BLOCKS * nq, LANES),
                               lambda t: (prev(t)[1], prev(t)[2], prev(t)[0])),
        out_shape=jax.ShapeDtypeStruct((nb, seq, BRANCH_WIDTH), BF16),
        scratch_shapes=[pltpu.VMEM((2 * NA_STEP_BLOCKS, nq, nk), F32)] * 2,
        compiler_params=_cparams(("arbitrary",)),
        name="nbr_attn",
    )(proj3, proj3, proj3, bias)


def _softmax_av(s, v):
    m = jnp.max(s, axis=-1, keepdims=True)
    e = jnp.exp(s - m)
    v_ones = jnp.concatenate([v, jnp.ones_like(v)], axis=1)
    acc = jnp.dot(e.astype(BF16), v_ones, preferred_element_type=F32)
    l = acc[:, LANES:]
    return acc[:, :LANES] / l, m + jnp.log(l)


def _dil_kernel(q0_ref, q1_ref, q2_ref, k0_ref, k1_ref, k2_ref, v0_ref, v1_ref, v2_ref, o_ref,
                o0_ref, l0_ref, o0r_ref, l0r_ref, o1_ref, l1_ref, o2_ref, l2_ref,
                band_ref, band4_ref, band16_ref, *, seq, tile):
    scale = C_HEAD_DIM ** -0.5
    dn = (((1,), (1,)), ((), ()))
    run = tile // C_TOP_DIL
    n_tiles = seq // tile
    sub16 = seq // C_TOP_DIL
    hw = C_HALF_WINDOW

    full_win = C_QBLK + 2 * hw
    pack = band16_ref.shape[0] // sub16

    @pl.when((pl.program_id(0) == 0) & (pl.program_id(1) == 0))
    def _():
        qi = lax.broadcasted_iota(jnp.int32, (C_QBLK, full_win), 0)
        kj = lax.broadcasted_iota(jnp.int32, (C_QBLK, full_win), 1)
        for lead in range(3):
            band_ref[lead] = jnp.where(jnp.abs(kj - qi - lead * hw) <= hw, 0.0, NEG)
        qi = lax.broadcasted_iota(jnp.int32, band16_ref.shape, 0)
        kj = lax.broadcasted_iota(jnp.int32, band16_ref.shape, 1)
        same_class = (qi // sub16) == (kj // sub16)
        band16_ref[...] = jnp.where(same_class & (jnp.abs(kj - qi) <= hw), 0.0, NEG)
        g1_keys = C_G1_RUNS * C_G1_KRUN
        qr = lax.broadcasted_iota(jnp.int32, (C_QBLK, g1_keys), 0)
        kc = lax.broadcasted_iota(jnp.int32, (C_QBLK, g1_keys), 1)
        q_pos = C_G1_RUNS * (qr % C_G1_QRUN) + qr // C_G1_QRUN
        k_pos = C_G1_RUNS * (kc % C_G1_KRUN) + kc // C_G1_KRUN
        for lead in range(3):
            shift = C_G1_RUNS * C_G1_REACH * lead
            band4_ref[lead] = jnp.where(jnp.abs(q_pos + shift - k_pos) <= hw, 0.0, NEG)

    def attend0(n):
        a = n * C_QBLK
        ws = jnp.clip(a - hw, 0, seq - full_win)
        rows = pl.ds(pl.multiple_of(a, C_QBLK), C_QBLK)
        kw = k0_ref[pl.ds(pl.multiple_of(ws, hw), full_win), :]
        vw = v0_ref[pl.ds(pl.multiple_of(ws, hw), full_win), :]
        s = lax.dot_general(q0_ref[rows, :], kw, dn, preferred_element_type=F32) * scale
        o0_ref[rows, :], l0_ref[rows, :] = _softmax_av(s + band_ref[(a - ws) // hw], vw)

    n_blocks = seq // C_QBLK
    unroll = min(C_UNROLL, n_blocks)

    def blocks0(it, carry):
        for u in range(unroll):
            attend0(it * unroll + u)
        return carry

    lax.fori_loop(0, n_blocks // unroll, blocks0, 0)

    def tile_row(j):
        return (j // run) * tile + j % run

    def attend1(n):
        blocks_per_class = sub16 // C_G1_QRUN
        rho4 = n // blocks_per_class
        j0 = (n % blocks_per_class) * C_G1_QRUN
        jw = jnp.clip(j0 - C_G1_REACH, 0, sub16 - C_G1_KRUN)
        class_row = [(C_DILATIONS[1] * c + rho4) * run for c in range(C_G1_RUNS)]
        q_base = tile_row(j0)
        k_base = [tile_row(jw + BF16_ROWS * p) for p in range(C_G1_KRUN // BF16_ROWS)]
        q_rows = [pl.ds(pl.multiple_of(q_base + class_row[c], C_G1_QRUN), C_G1_QRUN)
                  for c in range(C_G1_RUNS)]
        k_rows = [pl.ds(pl.multiple_of(base + class_row[c], BF16_ROWS), BF16_ROWS)
                  for c in range(C_G1_RUNS) for base in k_base]
        q = jnp.concatenate([q1_ref[r, :] for r in q_rows], axis=0)
        kw = jnp.concatenate([k1_ref[r, :] for r in k_rows], axis=0)
        vw = jnp.concatenate([v1_ref[r, :] for r in k_rows], axis=0)
        s = lax.dot_general(q, kw, dn, preferred_element_type=F32) * scale
        o, lse = _softmax_av(s + band4_ref[(j0 - jw) // C_G1_REACH], vw)
        for c, r in enumerate(q_rows):
            o1_ref[r, :] = o[c * C_G1_QRUN:(c + 1) * C_G1_QRUN]
            l1_ref[r, :] = lse[c * C_G1_QRUN:(c + 1) * C_G1_QRUN]

    def blocks1(it, carry):
        for u in range(unroll):
            attend1(it * unroll + u)
        return carry

    lax.fori_loop(0, n_blocks // unroll, blocks1, 0)

    def attend2(n):
        rows = [pl.ds(pl.multiple_of(t * tile + (n * pack + u) * run, run), run)
                for u in range(pack) for t in range(n_tiles)]
        q = jnp.concatenate([q2_ref[r, :] for r in rows], axis=0)
        kw = jnp.concatenate([k2_ref[r, :] for r in rows], axis=0)
        vw = jnp.concatenate([v2_ref[r, :] for r in rows], axis=0)
        s = lax.dot_general(q, kw, dn, preferred_element_type=F32) * scale
        o, lse = _softmax_av(s + band16_ref[...], vw)
        for i, r in enumerate(rows):
            o2_ref[r, :] = o[i * run:(i + 1) * run]
            l2_ref[r, :] = lse[i * run:(i + 1) * run]

    unroll2 = min(C_UNROLL, C_TOP_DIL // pack)

    def blocks2(it, carry):
        for u in range(unroll2):
            attend2(it * unroll2 + u)
        return carry

    lax.fori_loop(0, C_TOP_DIL // pack // unroll2, blocks2, 0)

    for t in range(n_tiles):
        for rho in range(C_TOP_DIL):
            dst = slice(t * tile + rho * run, t * tile + (rho + 1) * run)
            o0r_ref[dst, :] = o0_ref[pl.ds(t * tile + rho, run, stride=C_TOP_DIL), :]
            l0r_ref[dst, :] = l0_ref[pl.ds(t * tile + rho, run, stride=C_TOP_DIL), :]
    l0, l1, l2 = l0r_ref[...], l1_ref[...], l2_ref[...]
    l_max = jnp.maximum(jnp.maximum(l0, l1), l2)
    w0, w1, w2 = jnp.exp(l0 - l_max), jnp.exp(l1 - l_max), jnp.exp(l2 - l_max)
    o0_ref[...] = (w0 * o0r_ref[...] + w1 * o1_ref[...] + w2 * o2_ref[...]) * (1.0 / (w0 + w1 + w2))

    for t in range(n_tiles):
        def to_token_order(j, carry, t=t):
            src = o0_ref[pl.ds(t * tile + j, C_TOP_DIL, stride=run), :]
            dst = pl.ds(pl.multiple_of(t * tile + j * C_TOP_DIL, C_TOP_DIL), C_TOP_DIL)
            o_ref[dst, :] = src.astype(BF16)
            return carry

        lax.fori_loop(0, run, to_token_order, 0, unroll=8)


def _dilated_mixture(proj3, tile):
    nb, seq, _ = proj3.shape
    assert seq % tile == 0 and tile % (C_TOP_DIL * C_G1_QRUN) == 0
    assert seq >= C_QBLK + 2 * C_HALF_WINDOW and (seq // C_QBLK) % min(C_UNROLL, seq // C_QBLK) == 0
    assert (seq // C_TOP_DIL) % C_G1_KRUN == 0
    side2 = max(seq // C_TOP_DIL, C_QBLK + 2 * C_HALF_WINDOW)

    def spec(kind, g):
        cb0 = _col_block(f"{kind}{g}")
        return pl.BlockSpec((None, seq, LANES), lambda b, h, cb0=cb0: (b, 0, cb0 + h))

    in_specs = [spec(kind, g) for kind in ("CQ", "CK", "CV") for g in range(len(C_DILATIONS))]
    return pl.pallas_call(
        functools.partial(_dil_kernel, seq=seq, tile=tile),
        grid=(nb, C_HEADS_PER_GROUP),
        in_specs=in_specs,
        out_specs=pl.BlockSpec((None, seq, LANES), lambda b, h: (b, 0, h)),
        out_shape=jax.ShapeDtypeStruct((nb, seq, BRANCH_WIDTH), BF16),
        scratch_shapes=(
            [pltpu.VMEM((seq, LANES), F32)] * 8
            + [pltpu.VMEM((3, C_QBLK, C_QBLK + 2 * C_HALF_WINDOW), F32)] * 2
            + [pltpu.VMEM((side2, side2), F32)]),
        compiler_params=_cparams(("arbitrary", "arbitrary")),
        name="dil_attn",
    )(*([proj3] * 9))


def _tail_kernel(ya_ref, yb_ref, yc_ref, z0_ref, z1_ref, z2_ref, g0a_ref, g0b_ref, g1a_ref,
                 g1b_ref, g2a_ref, g2b_ref, x_ref, gate_ref, wbr_ref, wout_ref, o_ref):
    halves = [None, None]
    branches = ((ya_ref, z0_ref, (g0a_ref, g0b_ref)), (yb_ref, z1_ref, (g1a_ref, g1b_ref)),
                (yc_ref, z2_ref, (g2a_ref, g2b_ref)))
    for i, (y_ref, z_ref, g_refs) in enumerate(branches):
        yz = (y_ref[...].astype(F32) * z_ref[...].astype(F32)).astype(BF16)
        for half, g_ref in enumerate(g_refs):
            cols = slice(half * PROJ_TILE, (half + 1) * PROJ_TILE)
            u = g_ref[...].astype(F32) * jnp.dot(yz, wbr_ref[i, :, cols], preferred_element_type=F32)
            halves[half] = u if halves[half] is None else halves[half] + u
    merged = jnp.concatenate(halves, axis=1).astype(BF16)
    out = jnp.dot(merged, wout_ref[...], preferred_element_type=F32)
    o_ref[...] = x_ref[...] + gate_ref[0] * out


def _tail(ya, yb, yc, proj2, x2, mod3, w_br_bf, w_out_bf, seq, tm):
    ntok = x2.shape[0]
    tps = seq // tm
    y_spec = pl.BlockSpec((None, tm, BRANCH_WIDTH), lambda i: (i // tps, i % tps, 0))

    def tile_spec(name):
        return pl.BlockSpec((tm, PROJ_TILE), lambda i, s=SLOT[name]: (i, s))

    z_specs = [tile_spec(f"Z{t}") for t in range(3)]
    g_specs = [tile_spec(f"G{t}{half}") for t in range(3) for half in "ab"]
    return pl.pallas_call(
        _tail_kernel,
        grid=(ntok // tm,),
        in_specs=[y_spec, y_spec, y_spec, *z_specs, *g_specs,
                  pl.BlockSpec((tm, D_MODEL), lambda i: (i, 0)),
                  pl.BlockSpec((1, 1, D_MODEL), lambda i: (i // tps, 0, 2)),
                  pl.BlockSpec((3, BRANCH_WIDTH, D_MODEL), lambda i: (0, 0, 0),
                               pipeline_mode=pl.Buffered(1)),
                  pl.BlockSpec((D_MODEL, D_MODEL), lambda i: (0, 0), pipeline_mode=pl.Buffered(1))],
        out_specs=pl.BlockSpec((tm, D_MODEL), lambda i: (i, 0)),
        out_shape=jax.ShapeDtypeStruct((ntok, D_MODEL), F32),
        compiler_params=_cparams(("arbitrary",)),
        name="tail",
    )(ya, yb, yc, *([proj2] * 9), x2, mod3, w_br_bf, w_out_bf)


def _rope_tables(seq, tile):
    def base(d):
        inv = ROPE_THETA ** (-jnp.arange(0, d, 2, dtype=F32) / d)
        ang = jnp.arange(seq, dtype=F32)[:, None] * inv[None, :]
        return jnp.cos(ang), jnp.sin(ang)

    cos_a, sin_a = base(A_QK_DIM)
    ca = jnp.tile(cos_a, (1, 4))
    sa = jnp.concatenate([-sin_a, -sin_a, sin_a, sin_a], axis=-1)
    cos_c, sin_c = base(C_HEAD_DIM)
    cc = jnp.tile(cos_c, (1, 2))
    sc = jnp.concatenate([-sin_c, sin_c], axis=-1)

    def residue_major(tab):
        t4 = tab.reshape(seq // tile, tile // C_TOP_DIL, C_TOP_DIL, LANES)
        return t4.transpose(0, 2, 1, 3).reshape(seq, LANES)

    return ca, sa, cc, sc, residue_major(cc), residue_major(sc)


def _to_stored_columns(w):
    lead = w.shape[:-1]
    tiles = {name: w[..., i * PROJ_TILE:(i + 1) * PROJ_TILE] for i, name in enumerate(TILE_NAMES)}
    for name in ("AQ", "AK"):
        t = tiles[name].reshape(*lead, A_HEADS, 2, 2, A_QK_DIM // 2)
        tiles[name] = jnp.swapaxes(t, -3, -2).reshape(*lead, PROJ_TILE)
    return jnp.concatenate([tiles[n] for grp in PROJ_GROUPS for n in grp], axis=-1)


def _proj_gains(qn_a, kn_a, qn_b, kn_b, qn_c, kn_c):
    per_tile = {"AQ": jnp.tile(qn_a, PROJ_TILE // A_QK_DIM), "AK": jnp.tile(kn_a, PROJ_TILE // A_QK_DIM),
                "BQ": jnp.tile(qn_b, PROJ_TILE // B_HEAD_DIM), "BK": jnp.tile(kn_b, PROJ_TILE // B_HEAD_DIM)}
    for g in range(len(C_DILATIONS)):
        per_tile[f"CQ{g}"] = jnp.tile(qn_c, PROJ_TILE // C_HEAD_DIM)
        per_tile[f"CK{g}"] = jnp.tile(kn_c, PROJ_TILE // C_HEAD_DIM)
    ones = jnp.ones((PROJ_TILE,), F32)
    full = jnp.concatenate([per_tile.get(name, ones).astype(F32) for name in TILE_NAMES])
    return _to_stored_columns(full).reshape(len(PROJ_GROUPS), 1, GROUP_WIDTH)


def _proj_token_tile(seq):
    return min(512, seq)


def _tail_token_tile(seq):
    return min(1024, seq)


def _encoder_layer(x, mod3, layer_idx, tabs, ln_g, w_in_bf, gains, lam_params, subln, na_bias,
                   w_br_bf, w_out_bf):
    nb, seq, _ = x.shape
    x2 = x.reshape(nb * seq, D_MODEL)
    proj2 = _projection(x2, mod3, ln_g, w_in_bf, gains, tabs, seq, _proj_token_tile(seq))
    proj3 = proj2.reshape(nb, seq, IN_WIDTH)
    lam_init = 0.8 - 0.6 * math.exp(-0.3 * layer_idx)
    tq = A_SCORE_BYTES // (2 * 2 * seq * 4)
    ya = _diff_attention(proj3, lam_params, subln, lam_init, tq=tq)
    yb = _neighborhood_attention(proj3, na_bias)
    yc = _dilated_mixture(proj3, _proj_token_tile(seq))
    y2 = _tail(ya, yb, yc, proj2, x2, mod3, w_br_bf, w_out_bf, seq, _tail_token_tile(seq))
    return y2.reshape(nb, seq, D_MODEL)


def kernel(x_prompt, x_sample, c_prompt, c_sample, ln_g, w_ada, b_ada, w_in, qn_a, kn_a, lam_q1, lam_k1, lam_q2, lam_k2, subln_a, qn_b, kn_b, rpb_b, qn_c, kn_c, w_br, w_out):
    depth = w_in.shape[0]
    n_prompt = c_prompt.shape[0]
    mod_all = _modulation(jnp.concatenate([c_prompt, c_sample], axis=0), w_ada, b_ada)
    w_in_bf = _to_stored_columns(w_in.astype(BF16))
    w_br_bf = w_br.astype(BF16)
    w_out_bf = w_out.astype(BF16)
    gains = [_proj_gains(qn_a[l], kn_a[l], qn_b[l], kn_b[l], qn_c[l], kn_c[l]) for l in range(depth)]
    na_bias = [_na_bias_tables(rpb_b[l]) for l in range(depth)]

    def run(x, mod):
        nb, seq, _ = x.shape
        assert seq % (NA_QROWS * GRID_W) == 0 and seq // GRID_W >= NA_KROWS
        tabs = _rope_tables(seq, _proj_token_tile(seq))
        for l in range(depth):
            lam_params = jnp.stack([p[l] for p in (lam_q1, lam_k1, lam_q2, lam_k2)])
            x = _encoder_layer(x, mod[l].reshape(nb, 1, 3 * D_MODEL), l, tabs,
                               ln_g[l].reshape(1, D_MODEL), w_in_bf[l], gains[l], lam_params,
                               subln_a[l].reshape(1, LANES), na_bias[l], w_br_bf[l], w_out_bf[l])
        return x

    y_prompt = run(x_prompt, mod_all[:, :n_prompt])
    y_sample = run(x_sample, mod_all[:, n_prompt:])
    return (y_prompt, y_sample)
```
